```python
import math
import jax
import jax.numpy as jnp
from jax import lax
import numpy as np

D_MODEL = 4096
BATCH = 4
SEQ = 2048
DEPTH = 2
DEC_BATCH = 8
DEC_SEQ = 4
PAST_LEN = 16384
PAGE_SIZE = 128

HEAD_DIM = 128
FOX_HEADS = 16
NSA_HEADS = 16
NSA_KV_HEADS = 2
NSA_GROUP = NSA_HEADS // NSA_KV_HEADS
CMP_BLOCK = 32
SEL_BLOCK = 64
N_SEL = 16
WINDOW = 512
SEL_FORCE = 1000.0
N_BUCKETS = 32
MAX_DISTANCE = 1024
Q_BLOCK = 128
LRU_WIDTH = D_MODEL
LRU_HEADS = 16
LRU_BLOCK = LRU_WIDTH // LRU_HEADS
LRU_C = 8.0
LRU_CONV = 4
D_FF = 2 * D_MODEL
FFN_CONV = 3
PLE_DIM = 256
EPS = 1e-6
NEG_INF = -1e30

FOX_W = FOX_HEADS * HEAD_DIM
NSA_W = NSA_HEADS * HEAD_DIM
NSA_KV_W = NSA_KV_HEADS * HEAD_DIM
L0_SPLITS = (FOX_W, FOX_W, FOX_W, FOX_HEADS, NSA_W, NSA_KV_W, NSA_KV_W, NSA_KV_W, NSA_KV_W, NSA_KV_W, NSA_KV_W, 3 * NSA_HEADS)
L0_IN = 3 * FOX_W + FOX_HEADS + NSA_W + 6 * NSA_KV_W + 3 * NSA_HEADS
MIX_W = FOX_W + NSA_W

kernel_name = 'fox_nsa_rglru_convffn_step'


def rmsnorm(x, g):
    xf = x.astype(jnp.float32)
    y = xf * lax.rsqrt(jnp.mean(xf * xf, axis=-1, keepdims=True) + EPS)
    return (y * g.astype(jnp.float32)).astype(x.dtype)


def masked_softmax(s, mask):
    s = jnp.where(mask, s, NEG_INF)
    m = jnp.max(s, axis=-1, keepdims=True)
    e = jnp.where(mask, jnp.exp(s - m), 0.0)
    return e / jnp.maximum(jnp.sum(e, axis=-1, keepdims=True), 1e-30)


def t5_bucket(dist):
    n = jnp.maximum(dist, 0)
    exact = N_BUCKETS // 2
    nf = jnp.maximum(n, 1).astype(jnp.float32)
    large = exact + (jnp.log(nf / exact) / math.log(MAX_DISTANCE / exact) * (N_BUCKETS - exact)).astype(jnp.int32)
    return jnp.where(n < exact, n, jnp.minimum(large, N_BUCKETS - 1))


def split_cols(x, sizes):
    outs = []
    off = 0
    for s in sizes:
        outs.append(x[..., off:off + s])
        off += s
    return outs


def gather_pages(pool, page_table):
    rows = pool[page_table]
    return rows.reshape((page_table.shape[0], page_table.shape[1] * pool.shape[1]) + pool.shape[2:])


def causal_dwconv(x, buf, w, b):
    k_w = w.shape[0]
    T = x.shape[1]
    xp = jnp.concatenate([buf.astype(x.dtype), x], axis=1)
    y = xp[:, k_w - 1:k_w - 1 + T] * w[k_w - 1] + b
    for k in range(k_w - 1):
        y = y + xp[:, k:k + T] * w[k]
    return y, xp[:, xp.shape[1] - (k_w - 1):]


def linear_scan(a, u, h0):
    u = u.at[:, 0].add(a[:, 0] * h0)

    def combine(left, right):
        a_l, u_l = left
        a_r, u_r = right
        return a_r * a_l, a_r * u_l + u_r

    _, h = lax.associative_scan(combine, (a, u), axis=1)
    return h


def compress_blocks(blocks, w1, pe, w2):
    hid = jax.nn.gelu(jnp.einsum('bclgd,lde->bcge', blocks + pe[:, None, :], w1))
    return jnp.einsum('bcge,ef->bcgf', hid, w2)


def fox_attend(q, c_q, q_pos, segments):
    b, T, H, D = q.shape
    qb = Q_BLOCK if T % Q_BLOCK == 0 else T
    scale = D ** -0.5
    segs = [(k, v, ck.astype(jnp.float32).transpose(0, 2, 1)[:, :, None, :], kp) for k, v, ck, kp in segments]
    k_pos = jnp.concatenate([kp for _, _, _, kp in segments])
    c_qt = c_q.astype(jnp.float32).transpose(0, 2, 1)

    def one_block(i):
        q0 = i * qb
        qi = lax.dynamic_slice_in_dim(q, q0, qb, axis=1)
        ci = lax.dynamic_slice_in_dim(c_qt, q0, qb, axis=2)[..., None]
        pi = lax.dynamic_slice_in_dim(q_pos, q0, qb, axis=0)
        s = jnp.concatenate([jnp.einsum('bqhd,bshd->bhqs', qi, k).astype(jnp.float32) * scale + (ci - ck)
                             for k, _, ck, _ in segs], axis=-1)
        p = masked_softmax(s, pi[:, None] >= k_pos[None, :])
        out = None
        off = 0
        for _, v, _, kp in segs:
            n = kp.shape[0]
            o = jnp.einsum('bhqs,bshd->bqhd', p[..., off:off + n].astype(v.dtype), v)
            out = o if out is None else out + o
            off += n
        return out

    o = lax.map(one_block, jnp.arange(T // qb))
    return jnp.moveaxis(o, 0, 1).reshape(b, T, H, D)


def nsa_attend(q, q_pos, k_cmp, v_cmp, k_slc, v_slc, k_win, v_win, win_pos0, gates,
               cmp_w1_k, cmp_pe_k, cmp_w2_k, cmp_w1_v, cmp_pe_v, cmp_w2_v, rel_bias):
    b, T, H, D = q.shape
    G, HG = NSA_KV_HEADS, NSA_GROUP
    L = k_cmp.shape[1]
    nc, ns = L // CMP_BLOCK, L // SEL_BLOCK
    scale = D ** -0.5
    qg = q.reshape(b, T, G, HG, D)
    table = rel_bias.reshape(N_BUCKETS, G, HG)

    ck = compress_blocks(k_cmp.reshape(b, nc, CMP_BLOCK, G, D), cmp_w1_k, cmp_pe_k, cmp_w2_k)
    cv = compress_blocks(v_cmp.reshape(b, nc, CMP_BLOCK, G, D), cmp_w1_v, cmp_pe_v, cmp_w2_v)
    c_end = (jnp.arange(nc, dtype=jnp.int32) + 1) * CMP_BLOCK - 1
    c_dist = q_pos[:, None] - c_end[None, :]
    s = jnp.einsum('btghd,bcgd->bghtc', qg, ck).astype(jnp.float32) * scale
    s = s + table[t5_bucket(c_dist)].transpose(2, 3, 0, 1).astype(jnp.float32)
    p_cmp = masked_softmax(s, c_dist >= 0)
    o_cmp = jnp.einsum('bghtc,bcgd->btghd', p_cmp.astype(cv.dtype), cv)

    imp = p_cmp.sum(axis=2).reshape(b, G, T, ns, SEL_BLOCK // CMP_BLOCK).sum(axis=-1)
    blk = jnp.arange(ns, dtype=jnp.int32)[None, :]
    cur = (q_pos // SEL_BLOCK)[:, None]
    valid = blk * SEL_BLOCK <= q_pos[:, None]
    forced = (blk == 0) | (blk == cur) | (blk == cur - 1)
    score = jnp.where(forced, SEL_FORCE, jnp.where(valid, imp, -1.0))
    n_top = min(N_SEL, ns)
    _, sel = lax.top_k(score, n_top)
    ks_blk = k_slc.reshape(b, ns, SEL_BLOCK, G, D).transpose(0, 3, 1, 2, 4)
    vs_blk = v_slc.reshape(b, ns, SEL_BLOCK, G, D).transpose(0, 3, 1, 2, 4)
    n_key = n_top * SEL_BLOCK
    p_w = k_win.shape[1] - T
    qb = Q_BLOCK if T % Q_BLOCK == 0 else T
    bi = jnp.arange(b)[:, None, None, None]
    gi = jnp.arange(G)[None, :, None, None]

    def one_block(i):
        q0 = i * qb
        qi = lax.dynamic_slice_in_dim(qg, q0, qb, axis=1)
        pi = lax.dynamic_slice_in_dim(q_pos, q0, qb, axis=0)
        si = lax.dynamic_slice_in_dim(sel, q0, qb, axis=2)
        kg = ks_blk[bi, gi, si].reshape(b, G, qb, n_key, D)
        vg = vs_blk[bi, gi, si].reshape(b, G, qb, n_key, D)
        kpos = (si[..., None] * SEL_BLOCK + jnp.arange(SEL_BLOCK, dtype=jnp.int32)).reshape(b, G, qb, n_key)
        dist = pi[None, None, :, None] - kpos
        bias = table[t5_bucket(dist), gi].transpose(0, 1, 4, 2, 3)
        s = jnp.einsum('bqghd,bgqkd->bghqk', qi, kg).astype(jnp.float32) * scale + bias.astype(jnp.float32)
        p = masked_softmax(s, (dist >= 0)[:, :, None])
        o_slc = jnp.einsum('bghqk,bgqkd->bqghd', p.astype(vg.dtype), vg)
        kw = lax.dynamic_slice_in_dim(k_win, q0, p_w + qb, axis=1)
        vw = lax.dynamic_slice_in_dim(v_win, q0, p_w + qb, axis=1)
        wpos = win_pos0 + q0 + jnp.arange(p_w + qb, dtype=jnp.int32)
        wd = pi[:, None] - wpos[None, :]
        wmask = (wd >= 0) & (wd < WINDOW) & (wpos[None, :] >= 0)
        s = jnp.einsum('bqghd,bkgd->bghqk', qi, kw).astype(jnp.float32) * scale
        s = s + table[t5_bucket(wd)].transpose(2, 3, 0, 1).astype(jnp.float32)
        p = masked_softmax(s, wmask)
        o_win = jnp.einsum('bghqk,bkgd->bqghd', p.astype(vw.dtype), vw)
        return o_slc, o_win

    o_slc, o_win = lax.map(one_block, jnp.arange(T // qb))
    o_slc = jnp.moveaxis(o_slc, 0, 1).reshape(b, T, G, HG, D)
    o_win = jnp.moveaxis(o_win, 0, 1).reshape(b, T, G, HG, D)
    g = gates[..., None]
    o = g[:, :, 0] * o_cmp + g[:, :, 1] * o_slc + g[:, :, 2] * o_win
    return o.astype(q.dtype).reshape(b, T, H, D)


def attn_mixer(h, q_pos, past, win_len, w_in0, b_forget, cmp_w1_k, cmp_pe_k, cmp_w2_k,
               cmp_w1_v, cmp_pe_v, cmp_w2_v, rel_bias, w_out0):
    b, T, _ = h.shape
    fq, fk, fv, fz, nq, kc, vc, ks, vs, kw, vw, ng = split_cols(h @ w_in0, L0_SPLITS)
    fq, fk, fv = [t.reshape(b, T, FOX_HEADS, HEAD_DIM) for t in (fq, fk, fv)]
    nq = nq.reshape(b, T, NSA_HEADS, HEAD_DIM)
    kc, vc, ks, vs, kw, vw = [t.reshape(b, T, NSA_KV_HEADS, HEAD_DIM) for t in (kc, vc, ks, vs, kw, vw)]
    lf = jax.nn.log_sigmoid(fz.astype(jnp.float32) + b_forget.astype(jnp.float32))
    c_new = jnp.cumsum(lf, axis=1)
    if past is None:
        segs = [(fk, fv, c_new, q_pos)]
        seqs = (kc, vc, ks, vs)
        pre_k = jnp.zeros((b, WINDOW, NSA_KV_HEADS, HEAD_DIM), kw.dtype)
        pre_v = jnp.zeros((b, WINDOW, NSA_KV_HEADS, HEAD_DIM), vw.dtype)
        win_pos0 = -WINDOW
    else:
        pk, pv, plf, pkc, pvc, pks, pvs, wbk, wbv = past
        P = pk.shape[1]
        plf = plf.astype(jnp.float32)
        c_past = plf - lax.cumsum(plf, axis=1, reverse=True)
        segs = [(pk, pv, c_past, jnp.arange(P, dtype=jnp.int32)), (fk, fv, c_new, q_pos)]
        seqs = tuple(jnp.concatenate([a.astype(n.dtype), n], axis=1)
                     for a, n in ((pkc, kc), (pvc, vc), (pks, ks), (pvs, vs)))
        pre_k, pre_v = wbk, wbv
        win_pos0 = P - wbk.shape[1]
    o_fox = fox_attend(fq, c_new, q_pos, segs)
    L = seqs[0].shape[1]
    Lp = -(-L // SEL_BLOCK) * SEL_BLOCK
    kc_a, vc_a, ks_a, vs_a = [jnp.pad(t, ((0, 0), (0, Lp - L), (0, 0), (0, 0))) for t in seqs]
    kw_a = jnp.concatenate([pre_k.astype(kw.dtype), kw], axis=1)
    vw_a = jnp.concatenate([pre_v.astype(vw.dtype), vw], axis=1)
    gates = jax.nn.sigmoid(ng.astype(jnp.float32)).reshape(b, T, 3, NSA_KV_HEADS, NSA_GROUP)
    o_nsa = nsa_attend(nq, q_pos, kc_a, vc_a, ks_a, vs_a, kw_a, vw_a, win_pos0, gates,
                       cmp_w1_k, cmp_pe_k, cmp_w2_k, cmp_w1_v, cmp_pe_v, cmp_w2_v, rel_bias)
    y = jnp.concatenate([o_fox.reshape(b, T, FOX_W), o_nsa.reshape(b, T, NSA_W)], axis=-1) @ w_out0
    new = (fk, fv, lf, kc, vc, ks, vs, kw_a[:, kw_a.shape[1] - win_len:], vw_a[:, vw_a.shape[1] - win_len:])
    return y, new


def rglru_mixer(h, conv_buf, h0, w_in1, conv_w, conv_b, w_a, b_a, w_x, b_x, lam, w_out1):
    b, T, _ = h.shape
    gate_in, xr = split_cols(h @ w_in1, (LRU_WIDTH, LRU_WIDTH))
    xc, new_buf = causal_dwconv(xr, conv_buf, conv_w, conv_b)
    xb = xc.reshape(b, T, LRU_HEADS, LRU_BLOCK)
    r = jax.nn.sigmoid(jnp.einsum('bthi,hij->bthj', xb, w_a).reshape(b, T, LRU_WIDTH).astype(jnp.float32)
                       + b_a.astype(jnp.float32))
    ig = jax.nn.sigmoid(jnp.einsum('bthi,hij->bthj', xb, w_x).reshape(b, T, LRU_WIDTH).astype(jnp.float32)
                        + b_x.astype(jnp.float32))
    log_a = -LRU_C * r * jax.nn.softplus(-lam.astype(jnp.float32))
    u = jnp.sqrt(-jnp.expm1(2.0 * log_a)) * (ig * xc.astype(jnp.float32))
    hs = linear_scan(jnp.exp(log_a), u, h0.astype(jnp.float32))
    y = (jax.nn.gelu(gate_in) * hs.astype(h.dtype)) @ w_out1
    return y, new_buf, hs[:, -1]


def conv_ffn(h, conv_buf, w_up, conv_w, conv_b, w_down):
    g, u = split_cols(h @ w_up, (D_FF, D_FF))
    gc, new_buf = causal_dwconv(g, conv_buf, conv_w, conv_b)
    return (jax.nn.gelu(gc) * u) @ w_down, new_buf


def setup_inputs(seed: int = 0) -> dict:
    key = jax.random.key(seed)
    ks = iter(jax.random.split(key, 64))

    def nrm(shape, scale=1.0):
        return jax.random.normal(next(ks), shape, jnp.float32) * scale

    n_pages = PAST_LEN // PAGE_SIZE
    n_phys = (DEC_BATCH * n_pages * 5 + 3) // 4
    win_buf = min(WINDOW, PAST_LEN)
    fox_pool = (n_phys, PAGE_SIZE, FOX_HEADS, HEAD_DIM)
    kv_pool = (n_phys, PAGE_SIZE, NSA_KV_HEADS, HEAD_DIM)
    page_table = jax.random.permutation(next(ks), n_phys)[:DEC_BATCH * n_pages].reshape(DEC_BATCH, n_pages).astype(jnp.int32)
    u = jax.random.uniform(next(ks), (LRU_WIDTH,), jnp.float32, 0.9, 0.999)
    s = u ** (1.0 / LRU_C)
    lru_lambda = jnp.log(s) - jnp.log1p(-s)
    return {
        'x_prompt': nrm((BATCH, SEQ, D_MODEL)),
        'x_sample': nrm((DEC_BATCH, DEC_SEQ, D_MODEL)),
        'cache_fox_k': nrm(fox_pool),
        'cache_fox_v': nrm(fox_pool),
        'cache_fox_lf': jax.nn.log_sigmoid(nrm((n_phys, PAGE_SIZE, FOX_HEADS)) + 2.0),
        'cache_cmp_k': nrm(kv_pool),
        'cache_cmp_v': nrm(kv_pool),
        'cache_slc_k': nrm(kv_pool),
        'cache_slc_v': nrm(kv_pool),
        'state_win_k': nrm((DEC_BATCH, win_buf, NSA_KV_HEADS, HEAD_DIM)),
        'state_win_v': nrm((DEC_BATCH, win_buf, NSA_KV_HEADS, HEAD_DIM)),
        'state_lru_conv': nrm((DEC_BATCH, LRU_CONV - 1, LRU_WIDTH)),
        'state_lru_h': nrm((DEC_BATCH, LRU_WIDTH), 0.5),
        'state_ffn_conv': nrm((DEPTH, DEC_BATCH, FFN_CONV - 1, D_FF)),
        'page_table': page_table,
        'p_prompt': nrm((DEPTH, BATCH, SEQ, PLE_DIM)),
        'p_sample': nrm((DEPTH, DEC_BATCH, DEC_SEQ, PLE_DIM)),
        'norm_mix': 1.0 + nrm((DEPTH, D_MODEL), 0.02),
        'norm_ffn': 1.0 + nrm((DEPTH, D_MODEL), 0.02),
        'final_norm': 1.0 + nrm((D_MODEL,), 0.02),
        'w_in0': nrm((D_MODEL, L0_IN), D_MODEL ** -0.5),
        'b_forget': 2.0 + nrm((FOX_HEADS,), 0.1),
        'cmp_w1_k': nrm((CMP_BLOCK, HEAD_DIM, HEAD_DIM), (CMP_BLOCK * HEAD_DIM) ** -0.5),
        'cmp_pe_k': nrm((CMP_BLOCK, HEAD_DIM), 0.5),
        'cmp_w2_k': nrm((HEAD_DIM, HEAD_DIM), HEAD_DIM ** -0.5),
        'cmp_w1_v': nrm((CMP_BLOCK, HEAD_DIM, HEAD_DIM), (CMP_BLOCK * HEAD_DIM) ** -0.5),
        'cmp_pe_v': nrm((CMP_BLOCK, HEAD_DIM), 0.5),
        'cmp_w2_v': nrm((HEAD_DIM, HEAD_DIM), HEAD_DIM ** -0.5),
        'rel_bias': nrm((N_BUCKETS, NSA_HEADS), 0.5),
        'w_out0': nrm((MIX_W, D_MODEL), MIX_W ** -0.5),
        'w_in1': nrm((D_MODEL, 2 * LRU_WIDTH), D_MODEL ** -0.5),
        'lru_conv_w': nrm((LRU_CONV, LRU_WIDTH), LRU_CONV ** -0.5),
        'lru_conv_b': nrm((LRU_WIDTH,), 0.02),
        'lru_w_a': nrm((LRU_HEADS, LRU_BLOCK, LRU_BLOCK), LRU_BLOCK ** -0.5),
        'lru_b_a': nrm((LRU_WIDTH,), 0.02),
        'lru_w_x': nrm((LRU_HEADS, LRU_BLOCK, LRU_BLOCK), LRU_BLOCK ** -0.5),
        'lru_b_x': nrm((LRU_WIDTH,), 0.02),
        'lru_lambda': lru_lambda,
        'w_out1': nrm((LRU_WIDTH, D_MODEL), LRU_WIDTH ** -0.5),
        'ffn_w_up': nrm((DEPTH, D_MODEL, 2 * D_FF), D_MODEL ** -0.5),
        'ffn_conv_w': nrm((DEPTH, FFN_CONV, D_FF), FFN_CONV ** -0.5),
        'ffn_conv_b': nrm((DEPTH, D_FF), 0.02),
        'ffn_w_down': nrm((DEPTH, D_FF, D_MODEL), D_FF ** -0.5),
        'ple_w_proj': nrm((DEPTH, PLE_DIM, D_MODEL), PLE_DIM ** -0.5),
        'ple_w_gate': nrm((DEPTH, D_MODEL, D_MODEL), D_MODEL ** -0.5),
        'ple_norm': 1.0 + nrm((DEPTH, D_MODEL), 0.02),
    }


def reference(x_prompt, x_sample, cache_fox_k, cache_fox_v, cache_fox_lf, cache_cmp_k, cache_cmp_v,
              cache_slc_k, cache_slc_v, state_win_k, state_win_v, state_lru_conv, state_lru_h, state_ffn_conv,
              page_table, p_prompt, p_sample, norm_mix, norm_ffn, final_norm, w_in0, b_forget,
              cmp_w1_k, cmp_pe_k, cmp_w2_k, cmp_w1_v, cmp_pe_v, cmp_w2_v, rel_bias, w_out0,
              w_in1, lru_conv_w, lru_conv_b, lru_w_a, lru_b_a, lru_w_x, lru_b_x, lru_lambda, w_out1,
              ffn_w_up, ffn_conv_w, ffn_conv_b, ffn_w_down, ple_w_proj, ple_w_gate, ple_norm):
    win_len = state_win_k.shape[1]

    def trunk(x, p, past, lru_conv0, lru_h0, ffn_conv0):
        T = x.shape[1]
        pos0 = 0 if past is None else past[0].shape[1]
        q_pos = pos0 + jnp.arange(T, dtype=jnp.int32)
        ffn_bufs = []
        for i in range(DEPTH):
            h = rmsnorm(x, norm_mix[i])
            if i % 2 == 0:
                y, attn_new = attn_mixer(h, q_pos, past, win_len, w_in0, b_forget, cmp_w1_k, cmp_pe_k, cmp_w2_k,
                                         cmp_w1_v, cmp_pe_v, cmp_w2_v, rel_bias, w_out0)
            else:
                y, lru_conv_new, lru_h_new = rglru_mixer(h, lru_conv0, lru_h0, w_in1, lru_conv_w, lru_conv_b,
                                                         lru_w_a, lru_b_a, lru_w_x, lru_b_x, lru_lambda, w_out1)
            x = x + y
            y, buf = conv_ffn(rmsnorm(x, norm_ffn[i]), ffn_conv0[i], ffn_w_up[i], ffn_conv_w[i], ffn_conv_b[i], ffn_w_down[i])
            ffn_bufs.append(buf)
            x = x + y
            gate = jax.nn.sigmoid(rmsnorm(x, ple_norm[i]) @ ple_w_gate[i])
            x = x + gate * (p[i] @ ple_w_proj[i])
        return rmsnorm(x, final_norm), attn_new, lru_conv_new, lru_h_new, jnp.stack(ffn_bufs)

    bp = x_prompt.shape[0]
    y_prompt, attn_p, lru_conv_p, lru_h_p, ffn_conv_p = trunk(
        x_prompt, p_prompt, None,
        jnp.zeros((bp, LRU_CONV - 1, LRU_WIDTH), x_prompt.dtype),
        jnp.zeros((bp, LRU_WIDTH), jnp.float32),
        jnp.zeros((DEPTH, bp, FFN_CONV - 1, D_FF), x_prompt.dtype))
    past = (gather_pages(cache_fox_k, page_table), gather_pages(cache_fox_v, page_table),
            gather_pages(cache_fox_lf, page_table), gather_pages(cache_cmp_k, page_table),
            gather_pages(cache_cmp_v, page_table), gather_pages(cache_slc_k, page_table),
            gather_pages(cache_slc_v, page_table), state_win_k, state_win_v)
    y_sample, attn_s, lru_conv_s, lru_h_s, ffn_conv_s = trunk(
        x_sample, p_sample, past, state_lru_conv, state_lru_h, state_ffn_conv)
    fox_k_p, fox_v_p, fox_lf_p, cmp_k_p, cmp_v_p, slc_k_p, slc_v_p, win_k_p, win_v_p = attn_p
    fox_k_s, fox_v_s, fox_lf_s, cmp_k_s, cmp_v_s, slc_k_s, slc_v_s, win_k_s, win_v_s = attn_s
    return (y_prompt, y_sample, fox_k_p, fox_k_s, fox_v_p, fox_v_s, fox_lf_p, fox_lf_s,
            cmp_k_p, cmp_k_s, cmp_v_p, cmp_v_s, slc_k_p, slc_k_s, slc_v_p, slc_v_s,
            win_k_p, win_k_s, win_v_p, win_v_s, lru_conv_p, lru_conv_s, lru_h_p, lru_h_s,
            ffn_conv_p, ffn_conv_s)
```

```python
import functools
import math

import numpy as np
import jax
import jax.numpy as jnp
from jax import lax
from jax.experimental import pallas as pl
from jax.experimental.pallas import tpu as pltpu

PAGE_SIZE = 128
HEAD_DIM = 128
FOX_HEADS = 16
NSA_HEADS = 16
NSA_KV_HEADS = 2
NSA_GROUP = NSA_HEADS // NSA_KV_HEADS
CMP_BLOCK = 32
SEL_BLOCK = 64
N_SEL = 16
WINDOW = 512
SEL_FORCE = 1000.0
N_BUCKETS = 32
MAX_DISTANCE = 1024
LRU_HEADS = 16
LRU_C = 8.0
LRU_CONV = 4
FFN_CONV = 3
EPS = 1e-6
NEG_INF = -1e30
VISIBLE_MIN = -5e29

FOX_W = FOX_HEADS * HEAD_DIM
NSA_W = NSA_HEADS * HEAD_DIM
NSA_KV_W = NSA_KV_HEADS * HEAD_DIM
GROUP_W = NSA_GROUP * HEAD_DIM
SCALE = HEAD_DIM ** -0.5

COL_FQ, COL_FK, COL_FV = 0, FOX_W, 2 * FOX_W
COL_NQ = 3 * FOX_W
COL_KV = COL_NQ + NSA_W
COL_SMALL = COL_KV + 6 * NSA_KV_W
L0_PACKED = -(-(COL_SMALL + 128) // 1024) * 1024

F32 = jnp.float32
BF16 = jnp.bfloat16

VMEM_LIMIT_BYTES = 56 * 1024 * 1024


def _params(*sem):
    return pltpu.CompilerParams(dimension_semantics=sem, vmem_limit_bytes=VMEM_LIMIT_BYTES)


def _dot(a, b):
    return jnp.dot(a, b, preferred_element_type=F32)


def _dot_nt(a, b):
    return lax.dot_general(a, b, (((1,), (1,)), ((), ())), preferred_element_type=F32)


def _gelu(x):
    c = math.sqrt(2.0 / math.pi)
    return 0.5 * x * (1.0 + jnp.tanh(c * (x + 0.044715 * (x * x * x))))


def _sigmoid(x):
    return 1.0 / (1.0 + jnp.exp(-x))


def _bucket_np(dist):
    n = np.maximum(np.asarray(dist, np.int64), 0)
    exact = N_BUCKETS // 2
    nf = np.maximum(n, 1).astype(np.float64)
    large = exact + (np.log(nf / exact) / math.log(MAX_DISTANCE / exact) * (N_BUCKETS - exact)).astype(np.int64)
    return np.where(n < exact, n, np.minimum(large, N_BUCKETS - 1)).astype(np.int32)


def _rmsnorm_kernel(x_ref, g_ref, o_ref):
    x = x_ref[...]
    y = x * lax.rsqrt(jnp.mean(x * x, axis=-1, keepdims=True) + EPS)
    o_ref[...] = (y * g_ref[...]).astype(o_ref.dtype)


def rmsnorm(x, g, out_dtype):
    m, d = x.shape
    tm = min(m, 256)
    return pl.pallas_call(
        _rmsnorm_kernel,
        out_shape=jax.ShapeDtypeStruct((m, d), out_dtype),
        grid=(m // tm,),
        in_specs=[pl.BlockSpec((tm, d), lambda i: (i, 0)), pl.BlockSpec((1, d), lambda i: (0, 0))],
        out_specs=pl.BlockSpec((tm, d), lambda i: (i, 0)),
        compiler_params=_params("parallel"),
        name="rmsnorm",
    )(x, g.reshape(1, d))


def _mm_kernel(*refs, nk, epilogue):
    a_ref, w_ref = refs[0], refs[1]
    o_ref = refs[-1]
    k = pl.program_id(2)

    @pl.when(k == 0)
    def _():
        o_ref[...] = jnp.zeros_like(o_ref)

    o_ref[...] += _dot(a_ref[...], w_ref[...])

    if epilogue is not None:
        @pl.when(k == nk - 1)
        def _():
            acc = o_ref[...]
            if epilogue == "res":
                o_ref[...] = refs[2][...] + acc
            else:
                o_ref[...] = refs[2][...] + _sigmoid(acc) * refs[3][...]


def matmul(a, w, *, res=None, aux=None, tm=1024, tn=1024, tk=2048):
    m, kdim = a.shape
    n = w.shape[1]
    tm, tn, tk = min(tm, m), min(tn, n), min(tk, kdim)
    assert m % tm == 0 and n % tn == 0 and kdim % tk == 0
    nk = kdim // tk
    epilogue = None if res is None else ("res" if aux is None else "ple")
    in_specs = [pl.BlockSpec((tm, tk), lambda i, j, k: (i, k)), pl.BlockSpec((tk, tn), lambda i, j, k: (k, j))]
    args = [a, w]
    for extra in (res, aux):
        if extra is not None:
            in_specs.append(pl.BlockSpec((tm, tn), lambda i, j, k: (i, j)))
            args.append(extra)
    return pl.pallas_call(
        functools.partial(_mm_kernel, nk=nk, epilogue=epilogue),
        out_shape=jax.ShapeDtypeStruct((m, n), F32),
        grid=(m // tm, n // tn, nk),
        in_specs=in_specs,
        out_specs=pl.BlockSpec((tm, tn), lambda i, j, k: (i, j)),
        compiler_params=_params("parallel", "parallel", "arbitrary"),
        name="matmul_" + (epilogue or "plain"),
    )(*args)


def _ffn_up_kernel(a_ref, wg_ref, wu_ref, cw_ref, cb_ref, buf_ref, act_ref, tail_ref,
                   accg, accu, gbuf, *, nk, tiles_per_seq, tm):
    i = pl.program_id(1)
    k = pl.program_id(2)
    kw = FFN_CONV

    @pl.when(k == 0)
    def _():
        accg[...] = jnp.zeros_like(accg)
        accu[...] = jnp.zeros_like(accu)

    a = a_ref[...]
    accg[...] += _dot(a, wg_ref[...])
    accu[...] += _dot(a, wu_ref[...])

    @pl.when(k == nk - 1)
    def _():
        @pl.when(i % tiles_per_seq == 0)
        def _():
            gbuf[8 - (kw - 1):8, :] = buf_ref[0]

        g = accg[...]
        gbuf[8:8 + tm, :] = g
        y = g * cw_ref[kw - 1:kw, :] + cb_ref[...]
        for t in range(kw - 1):
            off = 8 - (kw - 1) + t
            y = y + gbuf[off:off + tm, :] * cw_ref[t:t + 1, :]
        act_ref[...] = (_gelu(y) * accu[...]).astype(act_ref.dtype)
        tail = gbuf[8 + tm - (kw - 1):8 + tm, :]
        gbuf[8 - (kw - 1):8, :] = tail
        tail_ref[0] = tail


def ffn_up_fused(a, w_up, conv_w, conv_b, conv_buf, seq_len, *, tm=1024, tn=1024, tk=2048):
    m, kdim = a.shape
    dff = w_up.shape[1] // 2
    tm, tn, tk = min(tm, seq_len), min(tn, dff), min(tk, kdim)
    assert seq_len % tm == 0 and dff % tn == 0 and kdim % tk == 0 and tm % 8 == 0
    nb = m // seq_len
    tps = seq_len // tm
    nj, nk = dff // tn, kdim // tk
    kw = FFN_CONV
    kern = functools.partial(_ffn_up_kernel, nk=nk, tiles_per_seq=tps, tm=tm)
    return pl.pallas_call(
        kern,
        out_shape=(jax.ShapeDtypeStruct((m, dff), BF16), jax.ShapeDtypeStruct((nb, kw - 1, dff), F32)),
        grid=(nj, m // tm, nk),
        in_specs=[
            pl.BlockSpec((tm, tk), lambda j, i, k: (i, k)),
            pl.BlockSpec((tk, tn), lambda j, i, k: (k, j)),
            pl.BlockSpec((tk, tn), lambda j, i, k: (k, j + nj)),
            pl.BlockSpec((kw, tn), lambda j, i, k: (0, j)),
            pl.BlockSpec((1, tn), lambda j, i, k: (0, j)),
            pl.BlockSpec((1, kw - 1, tn), lambda j, i, k: (i // tps, 0, j)),
        ],
        out_specs=(
            pl.BlockSpec((tm, tn), lambda j, i, k: (i, j)),
            pl.BlockSpec((1, kw - 1, tn), lambda j, i, k: (i // tps, 0, j)),
        ),
        scratch_shapes=[pltpu.VMEM((tm, tn), F32), pltpu.VMEM((tm, tn), F32), pltpu.VMEM((8 + tm, tn), F32)],
        compiler_params=_params("parallel", "arbitrary", "arbitrary"),
        name="ffn_up_fused",
    )(a, w_up, w_up, conv_w, conv_b.reshape(1, dff), conv_buf)


def _convgate_kernel(s0_ref, s1_ref, s2_ref, u_ref, cw_ref, cb_ref, o_ref):
    y = s0_ref[...] * cw_ref[2:3, :] + cb_ref[...]
    y = y + s2_ref[...] * cw_ref[0:1, :]
    y = y + s1_ref[...] * cw_ref[1:2, :]
    o_ref[...] = (_gelu(y) * u_ref[...]).astype(o_ref.dtype)


def convgate(s0, s1, s2, u, conv_w, conv_b, *, tn=2048):
    m, dff = s0.shape
    tn = min(tn, dff)
    row = pl.BlockSpec((m, tn), lambda j: (0, j))
    return pl.pallas_call(
        _convgate_kernel,
        out_shape=jax.ShapeDtypeStruct((m, dff), BF16),
        grid=(dff // tn,),
        in_specs=[row, row, row, row, pl.BlockSpec((FFN_CONV, tn), lambda j: (0, j)),
                  pl.BlockSpec((1, tn), lambda j: (0, j))],
        out_specs=row,
        compiler_params=_params("parallel"),
        name="convgate",
    )(s0, s1, s2, u, conv_w, conv_b.reshape(1, dff))


def _fox_prompt_kernel(q_ref, k_ref, v_ref, cq_ref, ck_ref, o_ref, m_sc, l_sc, acc_sc, *, nk, tq, tk):
    qi = pl.program_id(2)
    kj = pl.program_id(3)

    @pl.when(kj == 0)
    def _():
        m_sc[...] = jnp.full_like(m_sc, NEG_INF)
        l_sc[...] = jnp.zeros_like(l_sc)
        acc_sc[...] = jnp.zeros_like(acc_sc)

    @pl.when(kj * tk <= qi * tq + tq - 1)
    def _():
        q = q_ref[0].astype(BF16)
        k = k_ref[0].astype(BF16)
        s = _dot_nt(q, k) * SCALE + (cq_ref[0, 0] - ck_ref[0, 0])
        row = qi * tq + lax.broadcasted_iota(jnp.int32, (tq, tk), 0)
        col = kj * tk + lax.broadcasted_iota(jnp.int32, (tq, tk), 1)
        mask = row >= col
        s = jnp.where(mask, s, NEG_INF)
        m_prev = m_sc[...]
        m_new = jnp.maximum(m_prev, jnp.max(s, axis=-1, keepdims=True))
        alpha = jnp.exp(m_prev - m_new)
        p = jnp.where(mask, jnp.exp(s - m_new), 0.0)
        l_sc[...] = alpha * l_sc[...] + jnp.sum(p, axis=-1, keepdims=True)
        acc_sc[...] = alpha * acc_sc[...] + _dot(p.astype(BF16), v_ref[0].astype(BF16))
        m_sc[...] = m_new

    @pl.when(kj == nk - 1)
    def _():
        o_ref[0] = (acc_sc[...] / jnp.maximum(l_sc[...], 1e-30)).astype(o_ref.dtype)


def fox_prompt(proj3, c_new):
    b, t, _ = proj3.shape
    tq = tk = min(512, t)
    nq, nk = t // tq, t // tk
    h = FOX_HEADS
    c_col = c_new.transpose(0, 2, 1).reshape(b, h, t, 1)
    c_row = c_new.transpose(0, 2, 1).reshape(b, h, 1, t)
    kq, kk, kv = COL_FQ // HEAD_DIM, COL_FK // HEAD_DIM, COL_FV // HEAD_DIM

    def kidx(qi, kj):
        return jnp.minimum(kj, (qi * tq + tq - 1) // tk)

    kern = functools.partial(_fox_prompt_kernel, nk=nk, tq=tq, tk=tk)
    return pl.pallas_call(
        kern,
        out_shape=jax.ShapeDtypeStruct((b, t, FOX_W), BF16),
        grid=(b, h, nq, nk),
        in_specs=[
            pl.BlockSpec((1, tq, HEAD_DIM), lambda bi, hi, qi, kj: (bi, qi, kq + hi)),
            pl.BlockSpec((1, tk, HEAD_DIM), lambda bi, hi, qi, kj: (bi, kidx(qi, kj), kk + hi)),
            pl.BlockSpec((1, tk, HEAD_DIM), lambda bi, hi, qi, kj: (bi, kidx(qi, kj), kv + hi)),
            pl.BlockSpec((1, 1, tq, 1), lambda bi, hi, qi, kj: (bi, hi, qi, 0)),
            pl.BlockSpec((1, 1, 1, tk), lambda bi, hi, qi, kj: (bi, hi, 0, kidx(qi, kj))),
        ],
        out_specs=pl.BlockSpec((1, tq, HEAD_DIM), lambda bi, hi, qi, kj: (bi, qi, hi)),
        scratch_shapes=[pltpu.VMEM((tq, 1), F32), pltpu.VMEM((tq, 1), F32), pltpu.VMEM((tq, HEAD_DIM), F32)],
        compiler_params=_params("parallel", "parallel", "parallel", "arbitrary"),
        name="fox_prompt",
    )(proj3, proj3, proj3, c_col, c_row)


def _compress_kernel(x_ref, pe_ref, w1_ref, w2_ref, o_ref):
    x = (x_ref[...] + pe_ref[...]).astype(BF16)
    hid = _gelu(_dot(x, w1_ref[...]))
    o_ref[...] = _dot(hid.astype(BF16), w2_ref[...])


def compress(x, pe_row, w1g, w2g):
    r, kdim = x.shape
    tr = min(r, 512)
    assert r % tr == 0
    n = w1g.shape[1]
    return pl.pallas_call(
        _compress_kernel,
        out_shape=jax.ShapeDtypeStruct((r, n), F32),
        grid=(r // tr,),
        in_specs=[pl.BlockSpec((tr, kdim), lambda i: (i, 0)), pl.BlockSpec((1, kdim), lambda i: (0, 0)),
                  pl.BlockSpec((kdim, n), lambda i: (0, 0)), pl.BlockSpec((n, n), lambda i: (0, 0))],
        out_specs=pl.BlockSpec((tr, n), lambda i: (i, 0)),
        compiler_params=_params("parallel"),
        name="nsa_compress",
    )(x, pe_row, w1g, w2g)


def _compress_weights(w1, pe, w2):
    g = NSA_KV_HEADS
    eye = jnp.eye(g, dtype=w1.dtype)
    w1g = jnp.einsum("lde,gh->lgdhe", w1, eye).reshape(CMP_BLOCK * g * HEAD_DIM, g * HEAD_DIM).astype(BF16)
    w2g = jnp.einsum("de,gh->gdhe", w2, eye).reshape(g * HEAD_DIM, g * HEAD_DIM).astype(BF16)
    pe_row = jnp.broadcast_to(pe[:, None, :], (CMP_BLOCK, g, HEAD_DIM)).reshape(1, -1)
    return w1g, pe_row, w2g


def _cmp_prompt_kernel(q_ref, ck_ref, cv_ref, bias_ref, exp_ref, o_ref, sel_ref, *, tq, nc, n_top):
    qi = pl.program_id(2)
    ns = nc // 2
    t_col = qi * tq + lax.broadcasted_iota(jnp.int32, (tq, nc), 0)
    lane = lax.broadcasted_iota(jnp.int32, (tq, nc), 1)
    blk_c = jnp.where(lane < ns, 2 * lane, 2 * (lane - ns) + 1)
    mask = t_col >= (blk_c + 1) * CMP_BLOCK - 1
    ck = ck_ref[0].astype(BF16)
    cv = cv_ref[0].astype(BF16)
    imp = jnp.zeros((tq, nc), F32)
    for hg in range(NSA_GROUP):
        sl = slice(hg * HEAD_DIM, (hg + 1) * HEAD_DIM)
        s = _dot_nt(q_ref[0, :, sl].astype(BF16), ck) * SCALE + bias_ref[hg]
        s = jnp.where(mask, s, NEG_INF)
        m = jnp.max(s, axis=-1, keepdims=True)
        e = jnp.where(mask, jnp.exp(s - m), 0.0)
        p = e / jnp.maximum(jnp.sum(e, axis=-1, keepdims=True), 1e-30)
        imp = imp + p
        o_ref[0, :, sl] = _dot(p.astype(BF16), cv)
    imp_s = imp[:, :ns] + imp[:, ns:]
    t_s = qi * tq + lax.broadcasted_iota(jnp.int32, (tq, ns), 0)
    blk = lax.broadcasted_iota(jnp.int32, (tq, ns), 1)
    cur = t_s // SEL_BLOCK
    valid = blk * SEL_BLOCK <= t_s
    forced = (blk == 0) | (blk == cur) | (blk == cur - 1)
    score = jnp.where(forced, SEL_FORCE, jnp.where(valid, imp_s, -1.0))
    rank = jnp.zeros((tq, ns), F32)
    for j in range(ns):
        cj = score[:, j:j + 1]
        ahead = (cj > score) | ((cj == score) & (blk > j))
        rank = rank + jnp.where(ahead, 1.0, 0.0)
    sel = jnp.where(rank < n_top, 1.0, 0.0).astype(BF16)
    sel_ref[0, 0] = _dot(sel, exp_ref[...]).astype(sel_ref.dtype)


def cmp_prompt(proj3, ck_perm, cv_perm, bias_cmp):
    b, t, _ = proj3.shape
    nc = t // CMP_BLOCK
    ns = t // SEL_BLOCK
    tq = min(256, t)
    g = NSA_KV_HEADS
    n_top = min(N_SEL, ns)
    expand = jnp.asarray(np.repeat(np.eye(ns, dtype=np.float32), SEL_BLOCK, axis=1), BF16)
    kq = COL_NQ // GROUP_W
    kern = functools.partial(_cmp_prompt_kernel, tq=tq, nc=nc, n_top=n_top)
    return pl.pallas_call(
        kern,
        out_shape=(jax.ShapeDtypeStruct((b, t, NSA_W), F32), jax.ShapeDtypeStruct((b, g, t, t), BF16)),
        grid=(b, g, t // tq),
        in_specs=[
            pl.BlockSpec((1, tq, GROUP_W), lambda bi, gi, qi: (bi, qi, kq + gi)),
            pl.BlockSpec((1, nc, HEAD_DIM), lambda bi, gi, qi: (bi, 0, gi)),
            pl.BlockSpec((1, nc, HEAD_DIM), lambda bi, gi, qi: (bi, 0, gi)),
            pl.BlockSpec((NSA_GROUP, tq, nc), lambda bi, gi, qi: (gi, qi, 0)),
            pl.BlockSpec((ns, t), lambda bi, gi, qi: (0, 0)),
        ],
        out_specs=(
            pl.BlockSpec((1, tq, GROUP_W), lambda bi, gi, qi: (bi, qi, gi)),
            pl.BlockSpec((1, 1, tq, t), lambda bi, gi, qi: (bi, gi, qi, 0)),
        ),
        compiler_params=_params("parallel", "parallel", "parallel"),
        name="nsa_cmp_prompt",
    )(proj3, ck_perm, cv_perm, bias_cmp, expand)


NSA_TILE = 128


def _nsa_band_kernel(*refs, nr, mode, has_sel):
    if has_sel:
        q_ref, k_ref, v_ref, bias_ref, sel_ref, o_ref, qs, m_sc, l_sc, acc_sc = refs
    else:
        q_ref, k_ref, v_ref, bias_ref, o_ref, qs, m_sc, l_sc, acc_sc = refs
        sel_ref = None
    qi = pl.program_id(2)
    r = pl.program_id(3)
    tl = NSA_TILE
    hg_n = NSA_GROUP

    @pl.when(r == 0)
    def _():
        for hg in range(hg_n):
            qs[hg * tl:(hg + 1) * tl, :] = q_ref[0, :, hg * HEAD_DIM:(hg + 1) * HEAD_DIM].astype(BF16)
        m_sc[...] = jnp.full_like(m_sc, NEG_INF)
        l_sc[...] = jnp.zeros_like(l_sc)
        acc_sc[...] = jnp.zeros_like(acc_sc)

    active = (r <= qi) if mode == "causal" else (qi - (nr - 1) + r >= 0)

    @pl.when(active)
    def _():
        s = _dot_nt(qs[...], k_ref[0].astype(BF16)) * SCALE
        s3 = s.reshape(hg_n, tl, tl) + bias_ref[:, 0]
        if has_sel:
            s3 = jnp.where((sel_ref[0, 0] > 0.5)[None], s3, NEG_INF)
        s = s3.reshape(hg_n * tl, tl)
        vis = s > VISIBLE_MIN
        m_prev = m_sc[...]
        m_new = jnp.maximum(m_prev, jnp.max(s, axis=-1, keepdims=True))
        alpha = jnp.exp(m_prev - m_new)
        p = jnp.where(vis, jnp.exp(s - m_new), 0.0)
        l_sc[...] = alpha * l_sc[...] + jnp.sum(p, axis=-1, keepdims=True)
        acc_sc[...] = alpha * acc_sc[...] + _dot(p.astype(BF16), v_ref[0].astype(BF16))
        m_sc[...] = m_new

    @pl.when(r == nr - 1)
    def _():
        o = acc_sc[...] / jnp.maximum(l_sc[...], 1e-30)
        for hg in range(hg_n):
            o_ref[0, :, hg * HEAD_DIM:(hg + 1) * HEAD_DIM] = o[hg * tl:(hg + 1) * tl, :]


def nsa_band(proj3, bias_tiles, col_k, col_v, mode, selmask=None):
    b, t, _ = proj3.shape
    tl = NSA_TILE
    assert t % tl == 0
    nq = t // tl
    g = NSA_KV_HEADS
    n_tiles = bias_tiles.shape[1]
    nr = nq if mode == "causal" else n_tiles
    kq = COL_NQ // GROUP_W
    ck, cv = col_k // HEAD_DIM, col_v // HEAD_DIM
    if mode == "causal":
        def ktile(qi, r):
            return jnp.minimum(r, qi)

        def btile(qi, r):
            return jnp.clip(qi - r, 0, n_tiles - 1)
    else:
        def ktile(qi, r):
            return jnp.maximum(qi - (nr - 1) + r, 0)

        def btile(qi, r):
            return r
    in_specs = [
        pl.BlockSpec((1, tl, GROUP_W), lambda bi, gi, qi, r: (bi, qi, kq + gi)),
        pl.BlockSpec((1, tl, HEAD_DIM), lambda bi, gi, qi, r: (bi, ktile(qi, r), ck + gi)),
        pl.BlockSpec((1, tl, HEAD_DIM), lambda bi, gi, qi, r: (bi, ktile(qi, r), cv + gi)),
        pl.BlockSpec((NSA_GROUP, 1, tl, tl), lambda bi, gi, qi, r: (gi, btile(qi, r), 0, 0)),
    ]
    args = [proj3, proj3, proj3, bias_tiles]
    if selmask is not None:
        in_specs.append(pl.BlockSpec((1, 1, tl, tl), lambda bi, gi, qi, r: (bi, gi, qi, ktile(qi, r))))
        args.append(selmask)
    kern = functools.partial(_nsa_band_kernel, nr=nr, mode=mode, has_sel=selmask is not None)
    rows = NSA_GROUP * tl
    return pl.pallas_call(
        kern,
        out_shape=jax.ShapeDtypeStruct((b, t, NSA_W), F32),
        grid=(b, g, nq, nr),
        in_specs=in_specs,
        out_specs=pl.BlockSpec((1, tl, GROUP_W), lambda bi, gi, qi, r: (bi, qi, gi)),
        scratch_shapes=[pltpu.VMEM((rows, HEAD_DIM), BF16), pltpu.VMEM((rows, 1), F32),
                        pltpu.VMEM((rows, 1), F32), pltpu.VMEM((rows, HEAD_DIM), F32)],
        compiler_params=_params("parallel", "parallel", "parallel", "arbitrary"),
        name="nsa_band_" + mode,
    )(*args)


def _bias_tiles(rel_bias, dists, valid):
    idx = _bucket_np(dists)
    tiles = jnp.moveaxis(rel_bias.astype(F32)[idx], -1, 0)
    return jnp.where(jnp.asarray(valid)[None], tiles, NEG_INF)


def _combine_kernel(small_ref, oc_ref, os_ref, ow_ref, o_ref):
    gates = _sigmoid(small_ref[...])
    for h in range(NSA_HEADS):
        sl = slice(h * HEAD_DIM, (h + 1) * HEAD_DIM)
        base = FOX_HEADS + h
        g0 = gates[:, base:base + 1]
        g1 = gates[:, base + NSA_HEADS:base + NSA_HEADS + 1]
        g2 = gates[:, base + 2 * NSA_HEADS:base + 2 * NSA_HEADS + 1]
        o = g0 * oc_ref[:, sl] + g1 * os_ref[:, sl] + g2 * ow_ref[:, sl]
        o_ref[:, sl] = o.astype(o_ref.dtype)


def nsa_combine(proj, o_cmp, o_slc, o_win):
    m = proj.shape[0]
    tm = min(m, 256)
    wide = pl.BlockSpec((tm, NSA_W), lambda i: (i, 0))
    return pl.pallas_call(
        _combine_kernel,
        out_shape=jax.ShapeDtypeStruct((m, NSA_W), BF16),
        grid=(m // tm,),
        in_specs=[pl.BlockSpec((tm, 128), lambda i: (i, COL_SMALL // 128)), wide, wide, wide],
        out_specs=wide,
        compiler_params=_params("parallel"),
        name="nsa_combine",
    )(proj, o_cmp, o_slc, o_win)


def _lru_kernel(xr_ref, gate_ref, cw_ref, cb_ref, wa_ref, wx_ref, ba_ref, bx_ref, lam_ref, buf_ref, h0_ref,
                y_ref, hlast_ref, nbuf_ref, xbuf, hcar, *, nt, tt, last_row):
    ti = pl.program_id(2)
    kw = LRU_CONV

    @pl.when(ti == 0)
    def _():
        xbuf[8 - (kw - 1):8, :] = buf_ref[0]
        hcar[...] = h0_ref[0]

    x = xr_ref[0]
    xbuf[8:8 + tt, :] = x
    xc = x * cw_ref[kw - 1:kw, :] + cb_ref[...]
    for t in range(kw - 1):
        off = 8 - (kw - 1) + t
        xc = xc + xbuf[off:off + tt, :] * cw_ref[t:t + 1, :]
    xcb = xc.astype(BF16)
    r = _sigmoid(_dot(xcb, wa_ref[0]) + ba_ref[...])
    ig = _sigmoid(_dot(xcb, wx_ref[0]) + bx_ref[...])
    neg_lam = -lam_ref[...]
    softplus = jnp.maximum(neg_lam, 0.0) + jnp.log1p(jnp.exp(-jnp.abs(neg_lam)))
    log_a = -LRU_C * r * softplus
    a = jnp.exp(log_a)
    th = jnp.tanh(log_a)
    u = jnp.sqrt(-2.0 * th / (1.0 - th)) * (ig * xc)
    row = lax.broadcasted_iota(jnp.int32, a.shape, 0)
    step = 1
    while step < tt:
        keep = row >= step
        a_sh = jnp.where(keep, pltpu.roll(a, step, 0), 1.0)
        u_sh = jnp.where(keep, pltpu.roll(u, step, 0), 0.0)
        u = a * u_sh + u
        a = a * a_sh
        step *= 2
    hs = a * hcar[...] + u
    y_ref[0] = (_gelu(gate_ref[0]) * hs).astype(y_ref.dtype)
    hcar[...] = hs[tt - 1:tt, :]

    @pl.when(ti == nt - 1)
    def _():
        hlast_ref[0] = hs[last_row:last_row + 1, :]
        nbuf_ref[0] = xbuf[8 + last_row - (kw - 2):8 + last_row + 1, :]

    xbuf[8 - (kw - 1):8, :] = xbuf[8 + tt - (kw - 1):8 + tt, :]


def lru_block(proj1, conv_w, conv_b, w_a, b_a, w_x, b_x, lam, conv_buf, h0, t_valid):
    b, tp, w2 = proj1.shape
    w = w2 // 2
    nh = LRU_HEADS
    bw = w // nh
    tt = min(tp, 256)
    assert tp % tt == 0 and tt % 8 == 0
    nt = tp // tt
    last_row = (t_valid - 1) - (nt - 1) * tt
    assert 0 <= last_row < tt and (nt > 1 or last_row >= 0)
    kw = LRU_CONV
    vec = lambda a: a.reshape(1, w)
    chan = pl.BlockSpec((1, bw), lambda bi, hi, ti: (0, hi))
    kern = functools.partial(_lru_kernel, nt=nt, tt=tt, last_row=last_row)
    return pl.pallas_call(
        kern,
        out_shape=(jax.ShapeDtypeStruct((b, tp, w), BF16), jax.ShapeDtypeStruct((b, 1, w), F32),
                   jax.ShapeDtypeStruct((b, kw - 1, w), F32)),
        grid=(b, nh, nt),
        in_specs=[
            pl.BlockSpec((1, tt, bw), lambda bi, hi, ti: (bi, ti, nh + hi)),
            pl.BlockSpec((1, tt, bw), lambda bi, hi, ti: (bi, ti, hi)),
            pl.BlockSpec((kw, bw), lambda bi, hi, ti: (0, hi)),
            chan,
            pl.BlockSpec((1, bw, bw), lambda bi, hi, ti: (hi, 0, 0)),
            pl.BlockSpec((1, bw, bw), lambda bi, hi, ti: (hi, 0, 0)),
            chan, chan, chan,
            pl.BlockSpec((1, kw - 1, bw), lambda bi, hi, ti: (bi, 0, hi)),
            pl.BlockSpec((1, 1, bw), lambda bi, hi, ti: (bi, 0, hi)),
        ],
        out_specs=(
            pl.BlockSpec((1, tt, bw), lambda bi, hi, ti: (bi, ti, hi)),
            pl.BlockSpec((1, 1, bw), lambda bi, hi, ti: (bi, 0, hi)),
            pl.BlockSpec((1, kw - 1, bw), lambda bi, hi, ti: (bi, 0, hi)),
        ),
        scratch_shapes=[pltpu.VMEM((8 + tt, bw), F32), pltpu.VMEM((1, bw), F32)],
        compiler_params=_params("parallel", "parallel", "arbitrary"),
        name="rglru",
    )(proj1, proj1, conv_w, vec(conv_b), w_a.astype(BF16), w_x.astype(BF16), vec(b_a), vec(b_x), vec(lam),
      conv_buf, h0.reshape(b, 1, w))


GATHER_PAGES = 8


def _gather_kernel(pt_ref, *refs, n_pools, pg):
    ins, outs = refs[:n_pools * pg], refs[n_pools * pg:]
    for c in range(n_pools):
        for i in range(pg):
            outs[c][0, i * PAGE_SIZE:(i + 1) * PAGE_SIZE, :] = ins[c * pg + i][0]


def gather_pools(pools, page_table, extra_rows):
    b, n_pages = page_table.shape
    pg = math.gcd(GATHER_PAGES, n_pages)
    p = n_pages * PAGE_SIZE
    in_specs, args = [], []
    for pool in pools:
        c = pool.shape[-1]
        for i in range(pg):
            in_specs.append(pl.BlockSpec((1, PAGE_SIZE, c), lambda bi, j, pt, i=i: (pt[bi, j * pg + i], 0, 0)))
            args.append(pool)
    out_shape = tuple(jax.ShapeDtypeStruct((b, p + extra_rows, pool.shape[-1]), pool.dtype) for pool in pools)
    out_specs = tuple(pl.BlockSpec((1, pg * PAGE_SIZE, pool.shape[-1]), lambda bi, j, pt: (bi, j, 0)) for pool in pools)
    return pl.pallas_call(
        functools.partial(_gather_kernel, n_pools=len(pools), pg=pg),
        out_shape=out_shape,
        grid_spec=pltpu.PrefetchScalarGridSpec(
            num_scalar_prefetch=1, grid=(b, n_pages // pg), in_specs=in_specs, out_specs=out_specs),
        compiler_params=_params("parallel", "arbitrary"),
        name="gather_pages",
    )(page_table, *args)


FOX_DECODE_PAGES = 2


def _fox_decode_kernel(pt_ref, q_ref, cq_ref, ck_ref, *refs, n_steps, pf, t_new):
    kp, vp = refs[:pf], refs[pf:2 * pf]
    knew_ref, vnew_ref, o_ref, m_sc, l_sc, acc_sc = refs[2 * pf:]
    j = pl.program_id(1)
    rows = q_ref.shape[1]

    @pl.when(j == 0)
    def _():
        m_sc[...] = jnp.full_like(m_sc, NEG_INF)
        l_sc[...] = jnp.zeros_like(l_sc)
        acc_sc[...] = jnp.zeros_like(acc_sc)

    def update(k, v, ck, mask):
        s = _dot_nt(q_ref[0], k.astype(BF16)) * SCALE + (cq_ref[0] - jnp.concatenate([ck] * t_new, axis=0))
        if mask is not None:
            s = jnp.where(mask, s, NEG_INF)
        m_prev = m_sc[...]
        m_new = jnp.maximum(m_prev, jnp.max(s, axis=-1, keepdims=True))
        alpha = jnp.exp(m_prev - m_new)
        p = jnp.exp(s - m_new)
        if mask is not None:
            p = jnp.where(mask, p, 0.0)
        l_sc[...] = alpha * l_sc[...] + jnp.sum(p, axis=-1, keepdims=True)
        acc_sc[...] = alpha * acc_sc[...] + _dot(p.astype(BF16), v.astype(BF16))
        m_sc[...] = m_new

    @pl.when(j < n_steps)
    def _():
        for i in range(pf):
            update(kp[i][0], vp[i][0], ck_ref[0, :, i * PAGE_SIZE:(i + 1) * PAGE_SIZE], None)

    @pl.when(j == n_steps)
    def _():
        col = lax.broadcasted_iota(jnp.int32, (rows, PAGE_SIZE), 1)
        t_row = lax.broadcasted_iota(jnp.int32, (rows, PAGE_SIZE), 0) // FOX_HEADS
        update(knew_ref[0], vnew_ref[0], ck_ref[0, :, 0:PAGE_SIZE], col <= t_row)
        o_ref[0] = acc_sc[...] / jnp.maximum(l_sc[...], 1e-30)


def fox_decode(qbd, cq_rows, ck_all, k_pool, v_pool, k_new, v_new, page_table, t_new):
    b, n_pages = page_table.shape
    pf = math.gcd(FOX_DECODE_PAGES, n_pages)
    n_steps = n_pages // pf
    rows = qbd.shape[1]
    blk = pf * PAGE_SIZE

    def page(i):
        return lambda bi, j, pt: (pt[bi, jnp.minimum(j * pf + i, n_pages - 1)], 0, 0)

    pool_spec = [pl.BlockSpec((1, PAGE_SIZE, FOX_W), page(i)) for i in range(pf)]
    per_seq = lambda shape: pl.BlockSpec(shape, lambda bi, j, pt: (bi, 0, 0))
    kern = functools.partial(_fox_decode_kernel, n_steps=n_steps, pf=pf, t_new=t_new)
    return pl.pallas_call(
        kern,
        out_shape=jax.ShapeDtypeStruct((b, rows, FOX_W), F32),
        grid_spec=pltpu.PrefetchScalarGridSpec(
            num_scalar_prefetch=1,
            grid=(b, n_steps + 1),
            in_specs=[per_seq((1, rows, FOX_W)), per_seq((1, rows, 1)),
                      pl.BlockSpec((1, FOX_HEADS, blk), lambda bi, j, pt: (bi, 0, j))]
                     + pool_spec + pool_spec
                     + [per_seq((1, PAGE_SIZE, FOX_W)), per_seq((1, PAGE_SIZE, FOX_W))],
            out_specs=per_seq((1, rows, FOX_W)),
            scratch_shapes=[pltpu.VMEM((rows, 1), F32), pltpu.VMEM((rows, 1), F32), pltpu.VMEM((rows, FOX_W), F32)],
        ),
        compiler_params=_params("parallel", "arbitrary"),
        name="fox_decode",
    )(page_table, qbd, cq_rows, ck_all, *([k_pool] * pf), *([v_pool] * pf), k_new, v_new)


def _cmp_small_kernel(q_ref, k_ref, v_ref, bias_ref, o_ref, imp_ref, *, t_new):
    s = _dot_nt(q_ref[0, 0], k_ref[0].astype(BF16)) * SCALE + bias_ref[0]
    mask = s > VISIBLE_MIN
    m = jnp.max(s, axis=-1, keepdims=True)
    e = jnp.where(mask, jnp.exp(s - m), 0.0)
    p = e / jnp.maximum(jnp.sum(e, axis=-1, keepdims=True), 1e-30)
    o_ref[0, 0] = _dot(p.astype(BF16), v_ref[0].astype(BF16))
    imp_ref[0, 0] = jnp.sum(p.reshape(t_new, NSA_GROUP, p.shape[-1]), axis=1)


def cmp_small(q_rows, ck, cv, bias, t_new):
    b, g, rows, d = q_rows.shape
    s = ck.shape[1]
    return pl.pallas_call(
        functools.partial(_cmp_small_kernel, t_new=t_new),
        out_shape=(jax.ShapeDtypeStruct((b, g, rows, d), F32), jax.ShapeDtypeStruct((b, g, t_new, s), F32)),
        grid=(b, g),
        in_specs=[
            pl.BlockSpec((1, 1, rows, d), lambda bi, gi: (bi, gi, 0, 0)),
            pl.BlockSpec((1, s, d), lambda bi, gi: (bi, 0, gi)),
            pl.BlockSpec((1, s, d), lambda bi, gi: (bi, 0, gi)),
            pl.BlockSpec((1, rows, s), lambda bi, gi: (gi, 0, 0)),
        ],
        out_specs=(pl.BlockSpec((1, 1, rows, d), lambda bi, gi: (bi, gi, 0, 0)),
                   pl.BlockSpec((1, 1, t_new, s), lambda bi, gi: (bi, gi, 0, 0))),
        compiler_params=_params("parallel", "parallel"),
        name="nsa_cmp_sample",
    )(q_rows, ck, cv, bias)


def _flash_small_kernel(q_ref, k_ref, v_ref, bias_ref, o_ref, m_sc, l_sc, acc_sc, *, nk):
    j = pl.program_id(2)

    @pl.when(j == 0)
    def _():
        m_sc[...] = jnp.full_like(m_sc, NEG_INF)
        l_sc[...] = jnp.zeros_like(l_sc)
        acc_sc[...] = jnp.zeros_like(acc_sc)

    s = _dot_nt(q_ref[0, 0], k_ref[0].astype(BF16)) * SCALE + bias_ref[0, 0]
    vis = s > VISIBLE_MIN
    m_prev = m_sc[...]
    m_new = jnp.maximum(m_prev, jnp.max(s, axis=-1, keepdims=True))
    alpha = jnp.exp(m_prev - m_new)
    p = jnp.where(vis, jnp.exp(s - m_new), 0.0)
    l_sc[...] = alpha * l_sc[...] + jnp.sum(p, axis=-1, keepdims=True)
    acc_sc[...] = alpha * acc_sc[...] + _dot(p.astype(BF16), v_ref[0].astype(BF16))
    m_sc[...] = m_new

    @pl.when(j == nk - 1)
    def _():
        o_ref[0, 0] = acc_sc[...] / jnp.maximum(l_sc[...], 1e-30)


def flash_small(q_rows, k, v, bias, tk):
    b, g, rows, d = q_rows.shape
    s = k.shape[1]
    assert s % tk == 0
    nk = s // tk
    return pl.pallas_call(
        functools.partial(_flash_small_kernel, nk=nk),
        out_shape=jax.ShapeDtypeStruct((b, g, rows, d), F32),
        grid=(b, g, nk),
        in_specs=[
            pl.BlockSpec((1, 1, rows, d), lambda bi, gi, j: (bi, gi, 0, 0)),
            pl.BlockSpec((1, tk, d), lambda bi, gi, j: (bi, j, gi)),
            pl.BlockSpec((1, tk, d), lambda bi, gi, j: (bi, j, gi)),
            pl.BlockSpec((1, 1, rows, tk), lambda bi, gi, j: (bi, gi, 0, j)),
        ],
        out_specs=pl.BlockSpec((1, 1, rows, d), lambda bi, gi, j: (bi, gi, 0, 0)),
        scratch_shapes=[pltpu.VMEM((rows, 1), F32), pltpu.VMEM((rows, 1), F32), pltpu.VMEM((rows, d), F32)],
        compiler_params=_params("parallel", "parallel", "arbitrary"),
        name="nsa_flash_sample",
    )(q_rows, k, v, bias)


def _topk_kernel(col_ref, row_ref, o_ref, *, n_top):
    a = col_ref[...]
    bb = row_ref[...]
    shape = (a.shape[0], a.shape[1], bb.shape[2])
    j_idx = lax.broadcasted_iota(jnp.int32, shape, 1)
    i_idx = lax.broadcasted_iota(jnp.int32, shape, 2)
    ahead = (a > bb) | ((a == bb) & (j_idx < i_idx))
    rank = jnp.sum(jnp.where(ahead, 1.0, 0.0), axis=1, keepdims=True)
    o_ref[...] = jnp.where(rank < n_top, 1.0, 0.0)


def topk_mask(score, n_top):
    r, ns = score.shape
    tr = 8 if r % 8 == 0 else r
    out = pl.pallas_call(
        functools.partial(_topk_kernel, n_top=n_top),
        out_shape=jax.ShapeDtypeStruct((r, 1, ns), F32),
        grid=(r // tr,),
        in_specs=[pl.BlockSpec((tr, ns, 1), lambda i: (i, 0, 0)), pl.BlockSpec((tr, 1, ns), lambda i: (i, 0, 0))],
        out_specs=pl.BlockSpec((tr, 1, ns), lambda i: (i, 0, 0)),
        compiler_params=_params("parallel"),
        name="nsa_topk",
    )(score.reshape(r, ns, 1), score.reshape(r, 1, ns))
    return out.reshape(r, ns)


def _pack_w_in0(w_in0):
    d = w_in0.shape[0]
    off_fz = 3 * FOX_W
    off_nq = off_fz + FOX_HEADS
    off_kv = off_nq + NSA_W
    off_ng = off_kv + 6 * NSA_KV_W
    parts = [w_in0[:, :off_fz], w_in0[:, off_nq:off_ng], w_in0[:, off_fz:off_nq], w_in0[:, off_ng:]]
    used = sum(p.shape[1] for p in parts)
    parts.append(jnp.zeros((d, L0_PACKED - used), w_in0.dtype))
    return jnp.concatenate(parts, axis=1).astype(BF16)


def _forget_terms(proj3, b_forget):
    fz = proj3[:, :, COL_SMALL:COL_SMALL + FOX_HEADS]
    lf = jax.nn.log_sigmoid(fz + b_forget.astype(F32))
    return lf, jnp.cumsum(lf, axis=1)


def _kv_cols(proj3, idx):
    off = COL_KV + idx * NSA_KV_W
    return proj3[:, :, off:off + NSA_KV_W]


def attn_prompt(h_bf16, b, t, wts):
    m = b * t
    g = NSA_KV_HEADS
    proj = matmul(h_bf16, wts["w_in0"])
    proj3 = proj.reshape(b, t, L0_PACKED)
    lf, c_new = _forget_terms(proj3, wts["b_forget"])
    o_fox = fox_prompt(proj3, c_new)

    kc, vc, ks, vs, kw, vw = [_kv_cols(proj3, i) for i in range(6)]
    nc, ns = t // CMP_BLOCK, t // SEL_BLOCK
    assert t % (2 * CMP_BLOCK) == 0 and t % NSA_TILE == 0
    row_w = CMP_BLOCK * NSA_KV_W
    ck = compress(kc.reshape(b * nc, row_w), *wts["cmp_k"]).reshape(b, nc, NSA_KV_W)
    cv = compress(vc.reshape(b * nc, row_w), *wts["cmp_v"]).reshape(b, nc, NSA_KV_W)
    perm = np.concatenate([np.arange(0, nc, 2), np.arange(1, nc, 2)])
    rel = wts["rel_bias"].astype(F32)
    c_end = (perm + 1) * CMP_BLOCK - 1
    bias_cmp = jnp.moveaxis(rel[_bucket_np(np.arange(t)[:, None] - c_end[None, :])], -1, 0)
    o_cmp, selmask = cmp_prompt(proj3, ck[:, perm], cv[:, perm], bias_cmp)

    tl = NSA_TILE
    ii = np.arange(tl)[:, None] - np.arange(tl)[None, :]
    n_near = 1
    while _bucket_np(n_near * tl - (tl - 1)) < N_BUCKETS - 1:
        n_near += 1
    d_slc = np.stack([dd * tl + ii for dd in range(n_near)] + [np.full((tl, tl), MAX_DISTANCE * 4)])
    tiles_slc = _bias_tiles(rel, d_slc, d_slc >= 0)
    o_slc = nsa_band(proj3, tiles_slc, COL_KV + 2 * NSA_KV_W, COL_KV + 3 * NSA_KV_W, "causal", selmask)
    n_win = WINDOW // tl + 1
    d_win = np.stack([(n_win - 1 - r) * tl + ii for r in range(n_win)])
    tiles_win = _bias_tiles(rel, d_win, (d_win >= 0) & (d_win < WINDOW))
    o_win = nsa_band(proj3, tiles_win, COL_KV + 4 * NSA_KV_W, COL_KV + 5 * NSA_KV_W, "window")

    o_nsa = nsa_combine(proj, o_cmp.reshape(m, NSA_W), o_slc.reshape(m, NSA_W), o_win.reshape(m, NSA_W))
    mix = jnp.concatenate([o_fox.reshape(m, FOX_W), o_nsa], axis=1)
    win_len = wts["win_len"]
    zeros = jnp.zeros((b, WINDOW, NSA_KV_W), F32)
    kw_a = jnp.concatenate([zeros, kw], axis=1)
    vw_a = jnp.concatenate([zeros, vw], axis=1)
    fk = proj3[:, :, COL_FK:COL_FK + FOX_W]
    fv = proj3[:, :, COL_FV:COL_FV + FOX_W]
    new = (fk, fv, lf, kc, vc, ks, vs, kw_a[:, kw_a.shape[1] - win_len:], vw_a[:, vw_a.shape[1] - win_len:])
    return mix, new


def attn_sample(h_bf16, b, t, wts, caches, page_table, state_win_k, state_win_v):
    m = b * t
    g, hg = NSA_KV_HEADS, NSA_GROUP
    n_pages = page_table.shape[1]
    p = n_pages * PAGE_SIZE
    assert t <= PAGE_SIZE and p % SEL_BLOCK == 0
    proj = matmul(h_bf16, wts["w_in0"])
    proj3 = proj.reshape(b, t, L0_PACKED)
    lf, c_new = _forget_terms(proj3, wts["b_forget"])
    kc, vc, ks, vs, kw, vw = [_kv_cols(proj3, i) for i in range(6)]
    fq = proj3[:, :, COL_FQ:COL_FQ + FOX_W]
    fk = proj3[:, :, COL_FK:COL_FK + FOX_W]
    fv = proj3[:, :, COL_FV:COL_FV + FOX_W]
    q_pos = p + np.arange(t)

    cache_fox_k, cache_fox_v, cache_fox_lf, cache_cmp_k, cache_cmp_v, cache_slc_k, cache_slc_v = caches
    tk_slc = min(2048, p)
    pools = [cache_cmp_k.reshape(-1, PAGE_SIZE, NSA_KV_W), cache_cmp_v.reshape(-1, PAGE_SIZE, NSA_KV_W)]
    pcmp_k, pcmp_v, plf = gather_pools(pools + [cache_fox_lf], page_table, 0)
    pslc_k, pslc_v = gather_pools([cache_slc_k.reshape(-1, PAGE_SIZE, NSA_KV_W),
                                   cache_slc_v.reshape(-1, PAGE_SIZE, NSA_KV_W)], page_table, tk_slc)

    c_past = plf - lax.cumsum(plf, axis=1, reverse=True)
    pad_t = jnp.zeros((b, FOX_HEADS, FOX_DECODE_PAGES * PAGE_SIZE - t), F32)
    ck_all = jnp.concatenate([c_past.transpose(0, 2, 1), c_new.transpose(0, 2, 1), pad_t], axis=2)
    eye = jnp.eye(FOX_HEADS, dtype=F32)
    qbd = jnp.einsum("bthd,hg->bthgd", fq.reshape(b, t, FOX_HEADS, HEAD_DIM), eye)
    qbd = qbd.reshape(b, t * FOX_HEADS, FOX_W).astype(BF16)
    cq_rows = c_new.reshape(b, t * FOX_HEADS, 1)
    pad_rows = jnp.zeros((b, PAGE_SIZE - t, FOX_W), F32)
    o_full = fox_decode(qbd, cq_rows, ck_all, cache_fox_k.reshape(-1, PAGE_SIZE, FOX_W),
                        cache_fox_v.reshape(-1, PAGE_SIZE, FOX_W),
                        jnp.concatenate([fk, pad_rows], axis=1), jnp.concatenate([fv, pad_rows], axis=1),
                        page_table, t)
    o_fox = jnp.einsum("bthhd->bthd", o_full.reshape(b, t, FOX_HEADS, FOX_HEADS, HEAD_DIM)).reshape(m, FOX_W)

    rel = wts["rel_bias"].astype(F32)
    nq = proj3[:, :, COL_NQ:COL_NQ + NSA_W].reshape(b, t, g, hg, HEAD_DIM)
    q_rows = nq.transpose(0, 2, 1, 3, 4).reshape(b, g, t * hg, HEAD_DIM).astype(BF16)

    def head_rows(x):
        tt, ss = x.shape[0], x.shape[1]
        return x.reshape(tt, ss, g, hg).transpose(2, 0, 3, 1).reshape(g, tt * hg, ss)

    l_tot = p + t
    l_pad = -(-l_tot // SEL_BLOCK) * SEL_BLOCK
    nc, ns = l_pad // CMP_BLOCK, l_pad // SEL_BLOCK
    nc_past = p // CMP_BLOCK
    assert nc_past % 2 == 0 and (nc_past + 1) * CMP_BLOCK - 1 > q_pos[-1]
    row_w = CMP_BLOCK * NSA_KV_W
    ck = compress(pcmp_k.reshape(b * nc_past, row_w), *wts["cmp_k"]).reshape(b, nc_past, NSA_KV_W)
    cv = compress(pcmp_v.reshape(b * nc_past, row_w), *wts["cmp_v"]).reshape(b, nc_past, NSA_KV_W)
    perm = np.concatenate([np.arange(0, nc_past, 2), np.arange(1, nc_past, 2)])
    c_dist = q_pos[:, None] - ((perm + 1) * CMP_BLOCK - 1)[None, :]
    bias_cmp = jnp.where(jnp.asarray(c_dist >= 0)[..., None], rel[_bucket_np(c_dist)], NEG_INF)
    o_cmp, imp = cmp_small(q_rows, ck[:, perm], cv[:, perm], head_rows(bias_cmp), t)
    ns_past = nc_past // 2
    imp_s = imp[..., :ns_past] + imp[..., ns_past:]
    imp_s = jnp.concatenate([imp_s, jnp.zeros((b, g, t, ns - ns_past), F32)], axis=-1)
    blk = np.arange(ns)[None, :]
    cur = (q_pos // SEL_BLOCK)[:, None]
    valid = blk * SEL_BLOCK <= q_pos[:, None]
    forced = (blk == 0) | (blk == cur) | (blk == cur - 1)
    score = jnp.where(jnp.asarray(forced), SEL_FORCE, jnp.where(jnp.asarray(valid), imp_s, -1.0))
    sel = topk_mask(score.reshape(b * g * t, ns), min(N_SEL, ns)).reshape(b, g, t, ns)

    s_slc = p + tk_slc
    key_pos = np.arange(s_slc)
    d_slc = q_pos[:, None] - key_pos[None, :]
    ok = (d_slc >= 0) & (key_pos[None, :] < l_tot)
    bias_slc = head_rows(jnp.where(jnp.asarray(ok)[..., None], rel[_bucket_np(d_slc)], NEG_INF))
    key_blk = np.minimum(key_pos // SEL_BLOCK, ns - 1)
    sel_keys = sel[..., key_blk]
    sel_rows = jnp.broadcast_to(sel_keys[:, :, :, None, :], (b, g, t, hg, s_slc)).reshape(b, g, t * hg, s_slc)
    bias_slc = jnp.where(sel_rows > 0.5, bias_slc[None], NEG_INF)
    tail = jnp.zeros((b, tk_slc - t, NSA_KV_W), F32)
    pslc_k = lax.dynamic_update_slice(pslc_k, jnp.concatenate([ks, tail], axis=1), (0, p, 0))
    pslc_v = lax.dynamic_update_slice(pslc_v, jnp.concatenate([vs, tail], axis=1), (0, p, 0))
    o_slc = flash_small(q_rows, pslc_k, pslc_v, bias_slc, tk_slc)

    p_w = state_win_k.shape[1]
    s_win = -(-(p_w + t) // 128) * 128
    wpad = jnp.zeros((b, s_win - p_w - t, NSA_KV_W), F32)
    kw_a = jnp.concatenate([state_win_k.reshape(b, p_w, NSA_KV_W), kw], axis=1)
    vw_a = jnp.concatenate([state_win_v.reshape(b, p_w, NSA_KV_W), vw], axis=1)
    wpos = (p - p_w) + np.arange(s_win)
    wd = q_pos[:, None] - wpos[None, :]
    wok = (wd >= 0) & (wd < WINDOW) & (wpos[None, :] >= 0) & (np.arange(s_win)[None, :] < p_w + t)
    bias_win = head_rows(jnp.where(jnp.asarray(wok)[..., None], rel[_bucket_np(wd)], NEG_INF))
    bias_win = jnp.broadcast_to(bias_win[None], (b,) + bias_win.shape)
    o_win = flash_small(q_rows, jnp.concatenate([kw_a, wpad], axis=1), jnp.concatenate([vw_a, wpad], axis=1),
                        bias_win, s_win)

    def to_tokens(o):
        return o.reshape(b, g, t, hg, HEAD_DIM).transpose(0, 2, 1, 3, 4).reshape(m, NSA_W)

    o_nsa = nsa_combine(proj, to_tokens(o_cmp), to_tokens(o_slc), to_tokens(o_win))
    mix = jnp.concatenate([o_fox.astype(BF16), o_nsa], axis=1)
    win_len = wts["win_len"]
    new = (fk, fv, lf, kc, vc, ks, vs, kw_a[:, kw_a.shape[1] - win_len:], vw_a[:, vw_a.shape[1] - win_len:])
    return mix, new


def trunk(x, p_emb, wts, lru_conv0, lru_h0, ffn_conv0, past=None):
    b, t, d = x.shape
    m = b * t
    x2 = x.reshape(m, d)
    ffn_bufs = []
    for i in range(2):
        h = rmsnorm(x2, wts["norm_mix"][i], BF16)
        if i == 0:
            if past is None:
                mix, attn_new = attn_prompt(h, b, t, wts)
            else:
                mix, attn_new = attn_sample(h, b, t, wts, *past)
            x2 = matmul(mix, wts["w_out0"], res=x2)
        else:
            proj1 = matmul(h, wts["w_in1"]).reshape(b, t, -1)
            tp = -(-t // 8) * 8
            if tp != t:
                proj1 = jnp.concatenate([proj1, jnp.zeros((b, tp - t, proj1.shape[-1]), F32)], axis=1)
            gated, h_last, lru_buf = lru_block(proj1, wts["lru_conv_w"], wts["lru_conv_b"], wts["lru_w_a"],
                                               wts["lru_b_a"], wts["lru_w_x"], wts["lru_b_x"], wts["lru_lambda"],
                                               lru_conv0, lru_h0, t)
            x2 = matmul(gated[:, :t].reshape(m, -1), wts["w_out1"], res=x2)
        hf = rmsnorm(x2, wts["norm_ffn"][i], BF16)
        dff = wts["ffn_w_up"][i].shape[1] // 2
        if t % 8 == 0:
            act, buf = ffn_up_fused(hf, wts["ffn_w_up"][i], wts["ffn_conv_w"][i], wts["ffn_conv_b"][i],
                                    ffn_conv0[i], t)
        else:
            gu = matmul(hf, wts["ffn_w_up"][i])
            xp = jnp.concatenate([ffn_conv0[i], gu[:, :dff].reshape(b, t, dff)], axis=1)
            taps = [xp[:, FFN_CONV - 1 - s:FFN_CONV - 1 - s + t].reshape(m, dff) for s in range(FFN_CONV)]
            act = convgate(taps[0], taps[1], taps[2], gu[:, dff:], wts["ffn_conv_w"][i], wts["ffn_conv_b"][i])
            buf = xp[:, xp.shape[1] - (FFN_CONV - 1):]
        ffn_bufs.append(buf)
        x2 = matmul(act, wts["ffn_w_down"][i], res=x2)
        hp = rmsnorm(x2, wts["ple_norm"][i], BF16)
        emb = matmul(p_emb[i].reshape(m, -1).astype(BF16), wts["ple_w_proj"][i])
        x2 = matmul(hp, wts["ple_w_gate"][i], res=x2, aux=emb)
    y = rmsnorm(x2, wts["final_norm"], F32).reshape(b, t, d)
    return y, attn_new, lru_buf, h_last.reshape(b, -1), jnp.stack(ffn_bufs)


def kernel(x_prompt, x_sample, cache_fox_k, cache_fox_v, cache_fox_lf, cache_cmp_k, cache_cmp_v, cache_slc_k, cache_slc_v, state_win_k, state_win_v, state_lru_conv, state_lru_h, state_ffn_conv, page_table, p_prompt, p_sample, norm_mix, norm_ffn, final_norm, w_in0, b_forget, cmp_w1_k, cmp_pe_k, cmp_w2_k, cmp_w1_v, cmp_pe_v, cmp_w2_v, rel_bias, w_out0, w_in1, lru_conv_w, lru_conv_b, lru_w_a, lru_b_a, lru_w_x, lru_b_x, lru_lambda, w_out1, ffn_w_up, ffn_conv_w, ffn_conv_b, ffn_w_down, ple_w_proj, ple_w_gate, ple_norm):
    depth = norm_mix.shape[0]
    assert depth == 2
    d = x_prompt.shape[-1]
    dff = ffn_w_down.shape[1]
    wts = {
        "norm_mix": norm_mix, "norm_ffn": norm_ffn, "ple_norm": ple_norm, "final_norm": final_norm,
        "w_in0": _pack_w_in0(w_in0), "b_forget": b_forget, "rel_bias": rel_bias,
        "win_len": state_win_k.shape[1],
        "w_out0": w_out0.astype(BF16), "w_in1": w_in1.astype(BF16), "w_out1": w_out1.astype(BF16),
        "lru_conv_w": lru_conv_w, "lru_conv_b": lru_conv_b, "lru_w_a": lru_w_a, "lru_b_a": lru_b_a,
        "lru_w_x": lru_w_x, "lru_b_x": lru_b_x, "lru_lambda": lru_lambda,
        "ffn_w_up": ffn_w_up.astype(BF16), "ffn_conv_w": ffn_conv_w, "ffn_conv_b": ffn_conv_b,
        "ffn_w_down": ffn_w_down.astype(BF16), "ple_w_proj": ple_w_proj.astype(BF16),
        "ple_w_gate": ple_w_gate.astype(BF16),
    }
    w1g, pe_row, w2g = _compress_weights(cmp_w1_k, cmp_pe_k, cmp_w2_k)
    wts["cmp_k"] = (pe_row, w1g, w2g)
    w1g, pe_row, w2g = _compress_weights(cmp_w1_v, cmp_pe_v, cmp_w2_v)
    wts["cmp_v"] = (pe_row, w1g, w2g)

    bp = x_prompt.shape[0]
    y_p, attn_p, lru_conv_p, lru_h_p, ffn_conv_p = trunk(
        x_prompt, p_prompt, wts,
        jnp.zeros((bp, LRU_CONV - 1, d), F32), jnp.zeros((bp, d), F32),
        jnp.zeros((depth, bp, FFN_CONV - 1, dff), F32))
    caches = (cache_fox_k, cache_fox_v, cache_fox_lf, cache_cmp_k, cache_cmp_v, cache_slc_k, cache_slc_v)
    y_s, attn_s, lru_conv_s, lru_h_s, ffn_conv_s = trunk(
        x_sample, p_sample, wts, state_lru_conv, state_lru_h, state_ffn_conv,
        past=(caches, page_table, state_win_k, state_win_v))

    def shape_attn(new, b, t):
        fk, fv, lf, kc, vc, ks, vs, wk, wv = new
        h4 = lambda a: a.reshape(b, a.shape[1], FOX_HEADS, HEAD_DIM)
        g4 = lambda a: a.reshape(b, a.shape[1], NSA_KV_HEADS, HEAD_DIM)
        return h4(fk), h4(fv), lf, g4(kc), g4(vc), g4(ks), g4(vs), g4(wk), g4(wv)

    ap = shape_attn(attn_p, bp, x_prompt.shape[1])
    asmp = shape_attn(attn_s, x_sample.shape[0], x_sample.shape[1])
    out = [y_p, y_s]
    for a, s in zip(ap, asmp):
        out += [a, s]
    out += [lru_conv_p, lru_conv_s, lru_h_p, lru_h_s, ffn_conv_p, ffn_conv_s]
    return tuple(out)
```

```python
import functools
import math

import numpy as np
import jax
import jax.numpy as jnp
from jax import lax
from jax.experimental import pallas as pl
from jax.experimental.pallas import tpu as pltpu

PAGE_SIZE = 128
HEAD_DIM = 128
FOX_HEADS = 16
NSA_HEADS = 16
NSA_KV_HEADS = 2
NSA_GROUP = NSA_HEADS // NSA_KV_HEADS
CMP_BLOCK = 32
SEL_BLOCK = 64
N_SEL = 16
WINDOW = 512
SEL_FORCE = 1000.0
N_BUCKETS = 32
MAX_DISTANCE = 1024
LRU_HEADS = 16
LRU_C = 8.0
LRU_CONV = 4
FFN_CONV = 3
EPS = 1e-6
NEG_INF = -1e30
VISIBLE_MIN = -5e29

FOX_W = FOX_HEADS * HEAD_DIM
NSA_W = NSA_HEADS * HEAD_DIM
NSA_KV_W = NSA_KV_HEADS * HEAD_DIM
GROUP_W = NSA_GROUP * HEAD_DIM
SCALE = HEAD_DIM ** -0.5

COL_FQ, COL_FK, COL_FV = 0, FOX_W, 2 * FOX_W
COL_NQ = 3 * FOX_W
COL_KV = COL_NQ + NSA_W
COL_SMALL = COL_KV + 6 * NSA_KV_W
L0_PACKED = -(-(COL_SMALL + 128) // 1024) * 1024

F32 = jnp.float32
BF16 = jnp.bfloat16

VMEM_LIMIT_BYTES = 56 * 1024 * 1024


def _params(*sem):
    return pltpu.CompilerParams(dimension_semantics=sem, vmem_limit_bytes=VMEM_LIMIT_BYTES)


def _dot(a, b):
    return jnp.dot(a, b, preferred_element_type=F32)


def _dot_nt(a, b):
    return lax.dot_general(a, b, (((1,), (1,)), ((), ())), preferred_element_type=F32)


def _gelu(x):
    c = math.sqrt(2.0 / math.pi)
    return 0.5 * x * (1.0 + jnp.tanh(c * (x + 0.044715 * (x * x * x))))


def _sigmoid(x):
    return 1.0 / (1.0 + jnp.exp(-x))


def _bucket_np(dist):
    n = np.maximum(np.asarray(dist, np.int64), 0)
    exact = N_BUCKETS // 2
    nf = np.maximum(n, 1).astype(np.float64)
    large = exact + (np.log(nf / exact) / math.log(MAX_DISTANCE / exact) * (N_BUCKETS - exact)).astype(np.int64)
    return np.where(n < exact, n, np.minimum(large, N_BUCKETS - 1)).astype(np.int32)


def _rmsnorm_kernel(x_ref, g_ref, o_ref):
    x = x_ref[...]
    y = x * lax.rsqrt(jnp.mean(x * x, axis=-1, keepdims=True) + EPS)
    o_ref[...] = (y * g_ref[...]).astype(o_ref.dtype)


def rmsnorm(x, g, out_dtype):
    m, d = x.shape
    tm = min(m, 256)
    return pl.pallas_call(
        _rmsnorm_kernel,
        out_shape=jax.ShapeDtypeStruct((m, d), out_dtype),
        grid=(m // tm,),
        in_specs=[pl.BlockSpec((tm, d), lambda i: (i, 0)), pl.BlockSpec((1, d), lambda i: (0, 0))],
        out_specs=pl.BlockSpec((tm, d), lambda i: (i, 0)),
        compiler_params=_params("parallel"),
        name="rmsnorm",
    )(x, g.reshape(1, d))


def _mm_kernel(*refs, nk, epilogue):
    a_ref, w_ref = refs[0], refs[1]
    o_ref = refs[-1]
    k = pl.program_id(2)

    @pl.when(k == 0)
    def _():
        o_ref[...] = jnp.zeros_like(o_ref)

    o_ref[...] += _dot(a_ref[...], w_ref[...])

    if epilogue is not None:
        @pl.when(k == nk - 1)
        def _():
            acc = o_ref[...]
            if epilogue == "res":
                o_ref[...] = refs[2][...] + acc
            else:
                o_ref[...] = refs[2][...] + _sigmoid(acc) * refs[3][...]


def matmul(a, w, *, res=None, aux=None, tm=1024, tn=1024, tk=2048):
    m, kdim = a.shape
    n = w.shape[1]
    tm, tn, tk = min(tm, m), min(tn, n), min(tk, kdim)
    assert m % tm == 0 and n % tn == 0 and kdim % tk == 0
    nk = kdim // tk
    epilogue = None if res is None else ("res" if aux is None else "ple")
    in_specs = [pl.BlockSpec((tm, tk), lambda i, j, k: (i, k)), pl.BlockSpec((tk, tn), lambda i, j, k: (k, j))]
    args = [a, w]
    for extra in (res, aux):
        if extra is not None:
            in_specs.append(pl.BlockSpec((tm, tn), lambda i, j, k: (i, j)))
            args.append(extra)
    return pl.pallas_call(
        functools.partial(_mm_kernel, nk=nk, epilogue=epilogue),
        out_shape=jax.ShapeDtypeStruct((m, n), F32),
        grid=(m // tm, n // tn, nk),
        in_specs=in_specs,
        out_specs=pl.BlockSpec((tm, tn), lambda i, j, k: (i, j)),
        compiler_params=_params("parallel", "parallel", "arbitrary"),
        name="matmul_" + (epilogue or "plain"),
    )(*args)


def _mm_pair_kernel(a0_ref, a1_ref, w_ref, res_ref, o_ref):
    k = pl.program_id(2)

    @pl.when(k == 0)
    def _():
        o_ref[...] = res_ref[...] + _dot(a0_ref[...], w_ref[...])

    @pl.when(k == 1)
    def _():
        o_ref[...] += _dot(a1_ref[...], w_ref[...])


def matmul_pair(a0, a1, w, res, *, tm=1024, tn=1024):
    m, k0 = a0.shape
    assert a1.shape == (m, k0) and w.shape[0] == 2 * k0
    n = w.shape[1]
    tm, tn = min(tm, m), min(tn, n)
    assert m % tm == 0 and n % tn == 0
    piece = pl.BlockSpec((tm, k0), lambda i, j, k: (i, 0))
    tile = pl.BlockSpec((tm, tn), lambda i, j, k: (i, j))
    return pl.pallas_call(
        _mm_pair_kernel,
        out_shape=jax.ShapeDtypeStruct((m, n), F32),
        grid=(m // tm, n // tn, 2),
        in_specs=[piece, piece, pl.BlockSpec((k0, tn), lambda i, j, k: (k, j)), tile],
        out_specs=tile,
        compiler_params=_params("parallel", "parallel", "arbitrary"),
        name="matmul_pair",
    )(a0, a1, w, res)


def _ffn_up_kernel(a_ref, wg_ref, wu_ref, cw_ref, cb_ref, buf_ref, act_ref, tail_ref,
                   accg, accu, gbuf, *, nk, tiles_per_seq, tm):
    i = pl.program_id(1)
    k = pl.program_id(2)
    kw = FFN_CONV

    @pl.when(k == 0)
    def _():
        accg[...] = jnp.zeros_like(accg)
        accu[...] = jnp.zeros_like(accu)

    a = a_ref[...]
    accg[...] += _dot(a, wg_ref[...])
    accu[...] += _dot(a, wu_ref[...])

    @pl.when(k == nk - 1)
    def _():
        @pl.when(i % tiles_per_seq == 0)
        def _():
            gbuf[8 - (kw - 1):8, :] = buf_ref[0]

        g = accg[...]
        gbuf[8:8 + tm, :] = g
        y = g * cw_ref[kw - 1:kw, :] + cb_ref[...]
        for t in range(kw - 1):
            off = 8 - (kw - 1) + t
            y = y + gbuf[off:off + tm, :] * cw_ref[t:t + 1, :]
        act_ref[...] = (_gelu(y) * accu[...]).astype(act_ref.dtype)
        tail = gbuf[8 + tm - (kw - 1):8 + tm, :]
        gbuf[8 - (kw - 1):8, :] = tail
        tail_ref[0] = tail


def ffn_up_fused(a, w_up, conv_w, conv_b, conv_buf, seq_len, *, tm=1024, tn=1024, tk=2048):
    m, kdim = a.shape
    dff = w_up.shape[1] // 2
    tm, tn, tk = min(tm, seq_len), min(tn, dff), min(tk, kdim)
    assert seq_len % tm == 0 and dff % tn == 0 and kdim % tk == 0 and tm % 8 == 0
    nb = m // seq_len
    tps = seq_len // tm
    nj, nk = dff // tn, kdim // tk
    kw = FFN_CONV
    kern = functools.partial(_ffn_up_kernel, nk=nk, tiles_per_seq=tps, tm=tm)
    return pl.pallas_call(
        kern,
        out_shape=(jax.ShapeDtypeStruct((m, dff), BF16), jax.ShapeDtypeStruct((nb, kw - 1, dff), F32)),
        grid=(nj, m // tm, nk),
        in_specs=[
            pl.BlockSpec((tm, tk), lambda j, i, k: (i, k)),
            pl.BlockSpec((tk, tn), lambda j, i, k: (k, j)),
            pl.BlockSpec((tk, tn), lambda j, i, k: (k, j + nj)),
            pl.BlockSpec((kw, tn), lambda j, i, k: (0, j)),
            pl.BlockSpec((1, tn), lambda j, i, k: (0, j)),
            pl.BlockSpec((1, kw - 1, tn), lambda j, i, k: (i // tps, 0, j)),
        ],
        out_specs=(
            pl.BlockSpec((tm, tn), lambda j, i, k: (i, j)),
            pl.BlockSpec((1, kw - 1, tn), lambda j, i, k: (i // tps, 0, j)),
        ),
        scratch_shapes=[pltpu.VMEM((tm, tn), F32), pltpu.VMEM((tm, tn), F32), pltpu.VMEM((8 + tm, tn), F32)],
        compiler_params=_params("parallel", "arbitrary", "arbitrary"),
        name="ffn_up_fused",
    )(a, w_up, w_up, conv_w, conv_b.reshape(1, dff), conv_buf)


def _convgate_kernel(s0_ref, s1_ref, s2_ref, u_ref, cw_ref, cb_ref, o_ref):
    y = s0_ref[...] * cw_ref[2:3, :] + cb_ref[...]
    y = y + s2_ref[...] * cw_ref[0:1, :]
    y = y + s1_ref[...] * cw_ref[1:2, :]
    o_ref[...] = (_gelu(y) * u_ref[...]).astype(o_ref.dtype)


def convgate(s0, s1, s2, u, conv_w, conv_b, *, tn=2048):
    m, dff = s0.shape
    tn = min(tn, dff)
    row = pl.BlockSpec((m, tn), lambda j: (0, j))
    return pl.pallas_call(
        _convgate_kernel,
        out_shape=jax.ShapeDtypeStruct((m, dff), BF16),
        grid=(dff // tn,),
        in_specs=[row, row, row, row, pl.BlockSpec((FFN_CONV, tn), lambda j: (0, j)),
                  pl.BlockSpec((1, tn), lambda j: (0, j))],
        out_specs=row,
        compiler_params=_params("parallel"),
        name="convgate",
    )(s0, s1, s2, u, conv_w, conv_b.reshape(1, dff))


def _fox_prompt_kernel(q_ref, k_ref, v_ref, cq_ref, ck_ref, o_ref, m_sc, l_sc, acc_sc, cq_sc, *, nk, tq, tk):
    qi = pl.program_id(2)
    kj = pl.program_id(3)

    @pl.when(kj == 0)
    def _():
        m_sc[...] = jnp.full_like(m_sc, NEG_INF)
        l_sc[...] = jnp.zeros_like(l_sc)
        acc_sc[...] = jnp.zeros_like(acc_sc)
        cq_sc[...] = jnp.transpose(jnp.broadcast_to(cq_ref[0, 0], (128, tq)))

    @pl.when(kj * tk <= qi * tq + tq - 1)
    def _():
        q = q_ref[0].astype(BF16)
        k = k_ref[0].astype(BF16)
        cq = jnp.concatenate([cq_sc[...]] * (tk // 128), axis=1)
        s = _dot_nt(q, k) * SCALE + (cq - ck_ref[0, 0])
        row = qi * tq + lax.broadcasted_iota(jnp.int32, (tq, tk), 0)
        col = kj * tk + lax.broadcasted_iota(jnp.int32, (tq, tk), 1)
        s = jnp.where(row >= col, s, NEG_INF)
        m_prev = m_sc[...]
        m_new = jnp.maximum(m_prev, jnp.max(s, axis=-1, keepdims=True))
        alpha = jnp.exp(m_prev - m_new)
        p = jnp.exp(s - m_new)
        l_sc[...] = alpha * l_sc[...] + jnp.sum(p, axis=-1, keepdims=True)
        acc_sc[...] = alpha * acc_sc[...] + _dot(p.astype(BF16), v_ref[0].astype(BF16))
        m_sc[...] = m_new

    @pl.when(kj == nk - 1)
    def _():
        o_ref[0] = (acc_sc[...] / jnp.maximum(l_sc[...], 1e-30)).astype(o_ref.dtype)


def fox_prompt(proj3, c_new):
    b, t, _ = proj3.shape
    tq = min(512, t)
    tk = min(1024, t)
    assert tq % 128 == 0 and tk % 128 == 0
    nq, nk = t // tq, t // tk
    h = FOX_HEADS
    c_row = c_new.transpose(0, 2, 1).reshape(b, h, 1, t)
    kq, kk, kv = COL_FQ // HEAD_DIM, COL_FK // HEAD_DIM, COL_FV // HEAD_DIM

    def kidx(qi, kj):
        return jnp.minimum(kj, (qi * tq + tq - 1) // tk)

    kern = functools.partial(_fox_prompt_kernel, nk=nk, tq=tq, tk=tk)
    return pl.pallas_call(
        kern,
        out_shape=jax.ShapeDtypeStruct((b, t, FOX_W), BF16),
        grid=(b, h, nq, nk),
        in_specs=[
            pl.BlockSpec((1, tq, HEAD_DIM), lambda bi, hi, qi, kj: (bi, qi, kq + hi)),
            pl.BlockSpec((1, tk, HEAD_DIM), lambda bi, hi, qi, kj: (bi, kidx(qi, kj), kk + hi)),
            pl.BlockSpec((1, tk, HEAD_DIM), lambda bi, hi, qi, kj: (bi, kidx(qi, kj), kv + hi)),
            pl.BlockSpec((1, 1, 1, tq), lambda bi, hi, qi, kj: (bi, hi, 0, qi)),
            pl.BlockSpec((1, 1, 1, tk), lambda bi, hi, qi, kj: (bi, hi, 0, kidx(qi, kj))),
        ],
        out_specs=pl.BlockSpec((1, tq, HEAD_DIM), lambda bi, hi, qi, kj: (bi, qi, hi)),
        scratch_shapes=[pltpu.VMEM((tq, 1), F32), pltpu.VMEM((tq, 1), F32), pltpu.VMEM((tq, HEAD_DIM), F32),
                        pltpu.VMEM((tq, 128), F32)],
        compiler_params=_params("parallel", "parallel", "parallel", "arbitrary"),
        name="fox_prompt",
    )(proj3, proj3, proj3, c_row, c_row)


def _compress_kernel(x_ref, pe_ref, w1_ref, w2_ref, o_ref, *, tr):
    acc = jnp.zeros((tr, HEAD_DIM), F32)
    for l in range(CMP_BLOCK):
        xl = x_ref[pl.ds(l, tr, stride=CMP_BLOCK), :] + pe_ref[l:l + 1, :]
        acc = acc + _dot(xl.astype(BF16), w1_ref[l])
    o_ref[0] = _dot(_gelu(acc).astype(BF16), w2_ref[...])


def compress(x2d, col_block, n_groups, cmp_w):
    w1, pe, w2 = cmp_w
    rows = x2d.shape[0]
    r = rows // CMP_BLOCK
    tr = min(r, 256)
    assert r % tr == 0 and rows % CMP_BLOCK == 0
    return pl.pallas_call(
        functools.partial(_compress_kernel, tr=tr),
        out_shape=jax.ShapeDtypeStruct((n_groups, r, HEAD_DIM), F32),
        grid=(n_groups, r // tr),
        in_specs=[pl.BlockSpec((tr * CMP_BLOCK, HEAD_DIM), lambda g, i: (i, col_block + g)),
                  pl.BlockSpec((CMP_BLOCK, HEAD_DIM), lambda g, i: (0, 0)),
                  pl.BlockSpec((CMP_BLOCK, HEAD_DIM, HEAD_DIM), lambda g, i: (0, 0, 0)),
                  pl.BlockSpec((HEAD_DIM, HEAD_DIM), lambda g, i: (0, 0))],
        out_specs=pl.BlockSpec((1, tr, HEAD_DIM), lambda g, i: (g, i, 0)),
        compiler_params=_params("parallel", "parallel"),
        name="nsa_compress",
    )(x2d, pe, w1.astype(BF16), w2.astype(BF16))


def _cmp_prompt_kernel(q_ref, ck_ref, cv_ref, bias_ref, exp_ref, o_ref, sel_ref, *, tq, nc, n_top):
    qi = pl.program_id(2)
    ns = nc // 2
    t_col = qi * tq + lax.broadcasted_iota(jnp.int32, (tq, nc), 0)
    lane = lax.broadcasted_iota(jnp.int32, (tq, nc), 1)
    blk_c = jnp.where(lane < ns, 2 * lane, 2 * (lane - ns) + 1)
    mask = t_col >= (blk_c + 1) * CMP_BLOCK - 1
    ck = ck_ref[0, 0].astype(BF16)
    cv = cv_ref[0, 0].astype(BF16)
    imp = jnp.zeros((tq, nc), F32)
    for hg in range(NSA_GROUP):
        sl = slice(hg * HEAD_DIM, (hg + 1) * HEAD_DIM)
        s = _dot_nt(q_ref[0, :, sl].astype(BF16), ck) * SCALE + bias_ref[hg]
        s = jnp.where(mask, s, NEG_INF)
        m = jnp.max(s, axis=-1, keepdims=True)
        e = jnp.where(mask, jnp.exp(s - m), 0.0)
        p = e / jnp.maximum(jnp.sum(e, axis=-1, keepdims=True), 1e-30)
        imp = imp + p
        o_ref[0, :, sl] = _dot(p.astype(BF16), cv)
    imp_s = imp[:, :ns] + imp[:, ns:]
    t_s = qi * tq + lax.broadcasted_iota(jnp.int32, (tq, ns), 0)
    blk = lax.broadcasted_iota(jnp.int32, (tq, ns), 1)
    cur = t_s // SEL_BLOCK
    valid = blk * SEL_BLOCK <= t_s
    forced = (blk == 0) | (blk == cur) | (blk == cur - 1)
    score = jnp.where(forced, SEL_FORCE, jnp.where(valid, imp_s, -1.0))
    rank = jnp.zeros((tq, ns), F32)
    for j in range(ns):
        cj = score[:, j:j + 1]
        ahead = (cj > score) | ((cj == score) & (blk > j))
        rank = rank + jnp.where(ahead, 1.0, 0.0)
    sel = jnp.where(rank < n_top, 1.0, 0.0).astype(BF16)
    sel_ref[0, 0] = _dot(sel, exp_ref[...]).astype(sel_ref.dtype)


def cmp_prompt(proj3, ck_perm, cv_perm, bias_cmp):
    b, t, _ = proj3.shape
    nc = t // CMP_BLOCK
    ns = t // SEL_BLOCK
    tq = min(256, t)
    g = NSA_KV_HEADS
    n_top = min(N_SEL, ns)
    expand = jnp.asarray(np.repeat(np.eye(ns, dtype=np.float32), SEL_BLOCK, axis=1), BF16)
    kq = COL_NQ // GROUP_W
    kern = functools.partial(_cmp_prompt_kernel, tq=tq, nc=nc, n_top=n_top)
    return pl.pallas_call(
        kern,
        out_shape=(jax.ShapeDtypeStruct((b, t, NSA_W), F32), jax.ShapeDtypeStruct((b, g, t, t), BF16)),
        grid=(b, g, t // tq),
        in_specs=[
            pl.BlockSpec((1, tq, GROUP_W), lambda bi, gi, qi: (bi, qi, kq + gi)),
            pl.BlockSpec((1, 1, nc, HEAD_DIM), lambda bi, gi, qi: (gi, bi, 0, 0)),
            pl.BlockSpec((1, 1, nc, HEAD_DIM), lambda bi, gi, qi: (gi, bi, 0, 0)),
            pl.BlockSpec((NSA_GROUP, tq, nc), lambda bi, gi, qi: (gi, qi, 0)),
            pl.BlockSpec((ns, t), lambda bi, gi, qi: (0, 0)),
        ],
        out_specs=(
            pl.BlockSpec((1, tq, GROUP_W), lambda bi, gi, qi: (bi, qi, gi)),
            pl.BlockSpec((1, 1, tq, t), lambda bi, gi, qi: (bi, gi, qi, 0)),
        ),
        compiler_params=_params("parallel", "parallel", "parallel"),
        name="nsa_cmp_prompt",
    )(proj3, ck_perm, cv_perm, bias_cmp, expand)


NSA_TQ = 128
NSA_TK = 512


def _nsa_band_kernel(*refs, nr, ratio, mode, has_sel):
    if has_sel:
        q_ref, k_ref, v_ref, bias_ref, sel_ref, o_ref, qs, m_sc, l_sc, acc_sc = refs
    else:
        q_ref, k_ref, v_ref, bias_ref, o_ref, qs, m_sc, l_sc, acc_sc = refs
        sel_ref = None
    qi = pl.program_id(2)
    r = pl.program_id(3)
    tq = NSA_TQ
    tk = k_ref.shape[1]
    hg_n = NSA_GROUP

    @pl.when(r == 0)
    def _():
        for hg in range(hg_n):
            qs[hg * tq:(hg + 1) * tq, :] = q_ref[0, :, hg * HEAD_DIM:(hg + 1) * HEAD_DIM].astype(BF16)
        m_sc[...] = jnp.full_like(m_sc, NEG_INF)
        l_sc[...] = jnp.zeros_like(l_sc)
        acc_sc[...] = jnp.zeros_like(acc_sc)

    active = (r <= qi // ratio) if mode == "causal" else (qi // ratio - (nr - 1) + r >= 0)

    @pl.when(active)
    def _():
        s = _dot_nt(qs[...], k_ref[0].astype(BF16)) * SCALE
        s3 = s.reshape(hg_n, tq, tk) + bias_ref[:, 0]
        if has_sel:
            s3 = jnp.where((sel_ref[0, 0] > 0.5)[None], s3, NEG_INF)
        s = s3.reshape(hg_n * tq, tk)
        m_prev = m_sc[...]
        m_new = jnp.maximum(m_prev, jnp.max(s, axis=-1, keepdims=True))
        alpha = jnp.exp(m_prev - m_new)
        p = jnp.exp(s - m_new)
        l_sc[...] = alpha * l_sc[...] + jnp.sum(p, axis=-1, keepdims=True)
        acc_sc[...] = alpha * acc_sc[...] + _dot(p.astype(BF16), v_ref[0].astype(BF16))
        m_sc[...] = m_new

    @pl.when(r == nr - 1)
    def _():
        o = acc_sc[...] / jnp.maximum(l_sc[...], 1e-30)
        for hg in range(hg_n):
            o_ref[0, :, hg * HEAD_DIM:(hg + 1) * HEAD_DIM] = o[hg * tq:(hg + 1) * tq, :]


def nsa_band(proj3, bias_tiles, col_k, col_v, mode, selmask=None):
    b, t, _ = proj3.shape
    tq = NSA_TQ
    tk = bias_tiles.shape[-1]
    assert t % tk == 0 and tk % tq == 0
    ratio = tk // tq
    nq, nk = t // tq, t // tk
    g = NSA_KV_HEADS
    n_tiles = bias_tiles.shape[1]
    nr = nk if mode == "causal" else -(-WINDOW // tk) + 1
    kq = COL_NQ // GROUP_W
    ck, cv = col_k // HEAD_DIM, col_v // HEAD_DIM
    if mode == "causal":
        def ktile(qi, r):
            return jnp.minimum(r, qi // ratio)
    else:
        def ktile(qi, r):
            return jnp.maximum(qi // ratio - (nr - 1) + r, 0)

    def btile(qi, r):
        return jnp.clip(qi - ratio * ktile(qi, r), 0, n_tiles - 1)

    in_specs = [
        pl.BlockSpec((1, tq, GROUP_W), lambda bi, gi, qi, r: (bi, qi, kq + gi)),
        pl.BlockSpec((1, tk, HEAD_DIM), lambda bi, gi, qi, r: (bi, ktile(qi, r), ck + gi)),
        pl.BlockSpec((1, tk, HEAD_DIM), lambda bi, gi, qi, r: (bi, ktile(qi, r), cv + gi)),
        pl.BlockSpec((NSA_GROUP, 1, tq, tk), lambda bi, gi, qi, r: (gi, btile(qi, r), 0, 0)),
    ]
    args = [proj3, proj3, proj3, bias_tiles]
    if selmask is not None:
        in_specs.append(pl.BlockSpec((1, 1, tq, tk), lambda bi, gi, qi, r: (bi, gi, qi, ktile(qi, r))))
        args.append(selmask)
    kern = functools.partial(_nsa_band_kernel, nr=nr, ratio=ratio, mode=mode, has_sel=selmask is not None)
    rows = NSA_GROUP * tq
    return pl.pallas_call(
        kern,
        out_shape=jax.ShapeDtypeStruct((b, t, NSA_W), F32),
        grid=(b, g, nq, nr),
        in_specs=in_specs,
        out_specs=pl.BlockSpec((1, tq, GROUP_W), lambda bi, gi, qi, r: (bi, qi, gi)),
        scratch_shapes=[pltpu.VMEM((rows, HEAD_DIM), BF16), pltpu.VMEM((rows, 1), F32),
                        pltpu.VMEM((rows, 1), F32), pltpu.VMEM((rows, HEAD_DIM), F32)],
        compiler_params=_params("parallel", "parallel", "parallel", "arbitrary"),
        name="nsa_band_" + mode,
    )(*args)


_T5_THRESHOLDS = tuple(int(np.searchsorted(_bucket_np(np.arange(2 * MAX_DISTANCE)), k)) for k in range(1, N_BUCKETS))


def _t5_bias(rel_bias, dist):
    d = dist[..., None]
    out = jnp.broadcast_to(rel_bias[0], dist.shape + (rel_bias.shape[1],))
    for k, thr in enumerate(_T5_THRESHOLDS, start=1):
        out = jnp.where(d >= thr, rel_bias[k], out)
    return out


def _band_tiles(rel_bias, n_tiles, tk, window=None):
    o = jnp.arange(n_tiles, dtype=jnp.int32)[:, None, None]
    i = jnp.arange(NSA_TQ, dtype=jnp.int32)[None, :, None]
    j = jnp.arange(tk, dtype=jnp.int32)[None, None, :]
    d = o * NSA_TQ + i - j
    ok = d >= 0
    if window is not None:
        ok = ok & (d < window)
    tiles = jnp.where(ok[..., None], _t5_bias(rel_bias, d), NEG_INF)
    return jnp.moveaxis(tiles, -1, 0)


def _combine_kernel(small_ref, oc_ref, os_ref, ow_ref, o_ref):
    gates = _sigmoid(small_ref[...])
    for h in range(NSA_HEADS):
        sl = slice(h * HEAD_DIM, (h + 1) * HEAD_DIM)
        base = FOX_HEADS + h
        g0 = gates[:, base:base + 1]
        g1 = gates[:, base + NSA_HEADS:base + NSA_HEADS + 1]
        g2 = gates[:, base + 2 * NSA_HEADS:base + 2 * NSA_HEADS + 1]
        o = g0 * oc_ref[:, sl] + g1 * os_ref[:, sl] + g2 * ow_ref[:, sl]
        o_ref[:, sl] = o.astype(o_ref.dtype)


def nsa_combine(proj, o_cmp, o_slc, o_win):
    m = proj.shape[0]
    tm = min(m, 256)
    wide = pl.BlockSpec((tm, NSA_W), lambda i: (i, 0))
    return pl.pallas_call(
        _combine_kernel,
        out_shape=jax.ShapeDtypeStruct((m, NSA_W), BF16),
        grid=(m // tm,),
        in_specs=[pl.BlockSpec((tm, 128), lambda i: (i, COL_SMALL // 128)), wide, wide, wide],
        out_specs=wide,
        compiler_params=_params("parallel"),
        name="nsa_combine",
    )(proj, o_cmp, o_slc, o_win)


def _lru_kernel(xr_ref, gate_ref, cw_ref, cb_ref, wa_ref, wx_ref, ba_ref, bx_ref, lam_ref, buf_ref, h0_ref,
                y_ref, hlast_ref, nbuf_ref, xbuf, hcar, *, nt, tt, last_row):
    ti = pl.program_id(2)
    kw = LRU_CONV

    @pl.when(ti == 0)
    def _():
        xbuf[8 - (kw - 1):8, :] = buf_ref[0]
        hcar[...] = h0_ref[0]

    x = xr_ref[0]
    xbuf[8:8 + tt, :] = x
    xc = x * cw_ref[kw - 1:kw, :] + cb_ref[...]
    for t in range(kw - 1):
        off = 8 - (kw - 1) + t
        xc = xc + xbuf[off:off + tt, :] * cw_ref[t:t + 1, :]
    xcb = xc.astype(BF16)
    r = _sigmoid(_dot(xcb, wa_ref[0]) + ba_ref[...])
    ig = _sigmoid(_dot(xcb, wx_ref[0]) + bx_ref[...])
    neg_lam = -lam_ref[...]
    softplus = jnp.maximum(neg_lam, 0.0) + jnp.log1p(jnp.exp(-jnp.abs(neg_lam)))
    log_a = -LRU_C * r * softplus
    a = jnp.exp(log_a)
    th = jnp.tanh(log_a)
    u = jnp.sqrt(-2.0 * th / (1.0 - th)) * (ig * xc)
    row = lax.broadcasted_iota(jnp.int32, a.shape, 0)
    step = 1
    while step < tt:
        keep = row >= step
        a_sh = jnp.where(keep, pltpu.roll(a, step, 0), 1.0)
        u_sh = jnp.where(keep, pltpu.roll(u, step, 0), 0.0)
        u = a * u_sh + u
        a = a * a_sh
        step *= 2
    hs = a * hcar[...] + u
    y_ref[0] = (_gelu(gate_ref[0]) * hs).astype(y_ref.dtype)
    hcar[...] = hs[tt - 1:tt, :]

    @pl.when(ti == nt - 1)
    def _():
        hlast_ref[0] = hs[last_row:last_row + 1, :]
        nbuf_ref[0] = xbuf[8 + last_row - (kw - 2):8 + last_row + 1, :]

    xbuf[8 - (kw - 1):8, :] = xbuf[8 + tt - (kw - 1):8 + tt, :]


def lru_block(proj1, conv_w, conv_b, w_a, b_a, w_x, b_x, lam, conv_buf, h0, t_valid):
    b, tp, w2 = proj1.shape
    w = w2 // 2
    nh = LRU_HEADS
    bw = w // nh
    tt = min(tp, 256)
    assert tp % tt == 0 and tt % 8 == 0
    nt = tp // tt
    last_row = (t_valid - 1) - (nt - 1) * tt
    assert 0 <= last_row < tt and (nt > 1 or last_row >= 0)
    kw = LRU_CONV
    vec = lambda a: a.reshape(1, w)
    chan = pl.BlockSpec((1, bw), lambda bi, hi, ti: (0, hi))
    kern = functools.partial(_lru_kernel, nt=nt, tt=tt, last_row=last_row)
    return pl.pallas_call(
        kern,
        out_shape=(jax.ShapeDtypeStruct((b, tp, w), BF16), jax.ShapeDtypeStruct((b, 1, w), F32),
                   jax.ShapeDtypeStruct((b, kw - 1, w), F32)),
        grid=(b, nh, nt),
        in_specs=[
            pl.BlockSpec((1, tt, bw), lambda bi, hi, ti: (bi, ti, nh + hi)),
            pl.BlockSpec((1, tt, bw), lambda bi, hi, ti: (bi, ti, hi)),
            pl.BlockSpec((kw, bw), lambda bi, hi, ti: (0, hi)),
            chan,
            pl.BlockSpec((1, bw, bw), lambda bi, hi, ti: (hi, 0, 0)),
            pl.BlockSpec((1, bw, bw), lambda bi, hi, ti: (hi, 0, 0)),
            chan, chan, chan,
            pl.BlockSpec((1, kw - 1, bw), lambda bi, hi, ti: (bi, 0, hi)),
            pl.BlockSpec((1, 1, bw), lambda bi, hi, ti: (bi, 0, hi)),
        ],
        out_specs=(
            pl.BlockSpec((1, tt, bw), lambda bi, hi, ti: (bi, ti, hi)),
            pl.BlockSpec((1, 1, bw), lambda bi, hi, ti: (bi, 0, hi)),
            pl.BlockSpec((1, kw - 1, bw), lambda bi, hi, ti: (bi, 0, hi)),
        ),
        scratch_shapes=[pltpu.VMEM((8 + tt, bw), F32), pltpu.VMEM((1, bw), F32)],
        compiler_params=_params("parallel", "parallel", "arbitrary"),
        name="rglru",
    )(proj1, proj1, conv_w, vec(conv_b), w_a.astype(BF16), w_x.astype(BF16), vec(b_a), vec(b_x), vec(lam),
      conv_buf, h0.reshape(b, 1, w))


GATHER_PAGES = 8


def _gather_kernel(pt_ref, *refs, n_pools, pg):
    ins, outs = refs[:n_pools * pg], refs[n_pools * pg:]
    for c in range(n_pools):
        for i in range(pg):
            src = ins[c * pg + i]
            rows = slice(i * PAGE_SIZE, (i + 1) * PAGE_SIZE)
            if len(src.shape) == 4:
                for g in range(src.shape[2]):
                    outs[c][0, g, rows, :] = src[0, :, g, :]
            else:
                outs[c][0, rows, :] = src[0]


def gather_pools(pools, page_table, extra_rows):
    b, n_pages = page_table.shape
    pg = math.gcd(GATHER_PAGES, n_pages)
    p = n_pages * PAGE_SIZE
    in_specs, args, out_shape, out_specs = [], [], [], []
    for pool in pools:
        tail = pool.shape[2:]
        zeros = (0,) * (1 + len(tail))
        for i in range(pg):
            in_specs.append(pl.BlockSpec((1, PAGE_SIZE) + tail, lambda bi, j, pt, i=i, z=zeros: (pt[bi, j * pg + i],) + z))
            args.append(pool)
        if len(tail) == 2:
            out_shape.append(jax.ShapeDtypeStruct((b, tail[0], p + extra_rows, tail[1]), pool.dtype))
            out_specs.append(pl.BlockSpec((1, tail[0], pg * PAGE_SIZE, tail[1]), lambda bi, j, pt: (bi, 0, j, 0)))
        else:
            out_shape.append(jax.ShapeDtypeStruct((b, p + extra_rows, tail[0]), pool.dtype))
            out_specs.append(pl.BlockSpec((1, pg * PAGE_SIZE, tail[0]), lambda bi, j, pt: (bi, j, 0)))
    out_shape, out_specs = tuple(out_shape), tuple(out_specs)
    return pl.pallas_call(
        functools.partial(_gather_kernel, n_pools=len(pools), pg=pg),
        out_shape=out_shape,
        grid_spec=pltpu.PrefetchScalarGridSpec(
            num_scalar_prefetch=1, grid=(b, n_pages // pg), in_specs=in_specs, out_specs=out_specs),
        compiler_params=_params("parallel", "arbitrary"),
        name="gather_pages",
    )(page_table, *args)


FOX_DECODE_PAGES = 4


def _fox_decode_kernel(pt_ref, q_ref, cq_ref, ck_ref, cknew_ref, *refs, n_steps, pf):
    kp, vp = refs[:pf], refs[pf:2 * pf]
    knew_ref, vnew_ref, o_ref, m_sc, l_sc, acc_sc = refs[2 * pf:]
    j = pl.program_id(1)
    rows = q_ref.shape[1]
    nh = FOX_HEADS

    @pl.when(j == 0)
    def _():
        m_sc[...] = jnp.full_like(m_sc, NEG_INF)
        l_sc[...] = jnp.zeros_like(l_sc)
        acc_sc[...] = jnp.zeros_like(acc_sc)

    def update(k3, v3, ck_row, causal):
        n = k3.shape[0] * nh
        k2 = k3.reshape(n, HEAD_DIM).astype(BF16)
        v2 = v3.reshape(n, HEAD_DIM).astype(BF16)
        s = _dot_nt(q_ref[0], k2) * SCALE + (cq_ref[0] - ck_row)
        col = lax.broadcasted_iota(jnp.int32, (rows, n), 1)
        row = lax.broadcasted_iota(jnp.int32, (rows, n), 0)
        ok = (col % nh) == (row % nh)
        if causal:
            ok = ok & (col // nh <= row // nh)
        s = jnp.where(ok, s, NEG_INF)
        m_prev = m_sc[...]
        m_new = jnp.maximum(m_prev, jnp.max(s, axis=-1, keepdims=True))
        alpha = jnp.exp(m_prev - m_new)
        p = jnp.exp(s - m_new)
        l_sc[...] = alpha * l_sc[...] + jnp.sum(p, axis=-1, keepdims=True)
        acc_sc[...] = alpha * acc_sc[...] + _dot(p.astype(BF16), v2)
        m_sc[...] = m_new

    @pl.when(j < n_steps)
    def _():
        w = PAGE_SIZE * nh
        for i in range(pf):
            update(kp[i][0], vp[i][0], ck_ref[0, :, i * w:(i + 1) * w], False)

    @pl.when(j == n_steps)
    def _():
        update(knew_ref[0], vnew_ref[0], cknew_ref[0], True)
        o_ref[0] = acc_sc[...] / jnp.maximum(l_sc[...], 1e-30)


def fox_decode(q_rows, cq_rows, ck_past, ck_new, k_pool, v_pool, k_new, v_new, page_table):
    b, n_pages = page_table.shape
    pf = math.gcd(FOX_DECODE_PAGES, n_pages)
    n_steps = n_pages // pf
    rows = q_rows.shape[1]
    tp = k_new.shape[1]
    nh = FOX_HEADS

    def page(i):
        return lambda bi, j, pt: (pt[bi, jnp.minimum(j * pf + i, n_pages - 1)], 0, 0, 0)

    pool_spec = [pl.BlockSpec((1, PAGE_SIZE, nh, HEAD_DIM), page(i)) for i in range(pf)]
    per_seq = lambda shape: pl.BlockSpec(shape, lambda bi, j, pt: (bi,) + (0,) * (len(shape) - 1))
    kern = functools.partial(_fox_decode_kernel, n_steps=n_steps, pf=pf)
    return pl.pallas_call(
        kern,
        out_shape=jax.ShapeDtypeStruct((b, rows, HEAD_DIM), F32),
        grid_spec=pltpu.PrefetchScalarGridSpec(
            num_scalar_prefetch=1,
            grid=(b, n_steps + 1),
            in_specs=[per_seq((1, rows, HEAD_DIM)), per_seq((1, rows, 1)),
                      pl.BlockSpec((1, 1, pf * PAGE_SIZE * nh), lambda bi, j, pt: (bi, 0, jnp.minimum(j, n_steps - 1))),
                      per_seq((1, 1, tp * nh))]
                     + pool_spec + pool_spec
                     + [per_seq((1, tp, nh, HEAD_DIM)), per_seq((1, tp, nh, HEAD_DIM))],
            out_specs=per_seq((1, rows, HEAD_DIM)),
            scratch_shapes=[pltpu.VMEM((rows, 1), F32), pltpu.VMEM((rows, 1), F32), pltpu.VMEM((rows, HEAD_DIM), F32)],
        ),
        compiler_params=_params("parallel", "arbitrary"),
        name="fox_decode",
    )(page_table, q_rows, cq_rows, ck_past, ck_new, *([k_pool] * pf), *([v_pool] * pf), k_new, v_new)


def _cmp_small_kernel(q_ref, k_ref, v_ref, bias_ref, o_ref, imp_ref, *, t_new):
    s = _dot_nt(q_ref[0, 0], k_ref[0, 0].astype(BF16)) * SCALE + bias_ref[0]
    mask = s > VISIBLE_MIN
    m = jnp.max(s, axis=-1, keepdims=True)
    e = jnp.where(mask, jnp.exp(s - m), 0.0)
    p = e / jnp.maximum(jnp.sum(e, axis=-1, keepdims=True), 1e-30)
    o_ref[0, 0] = _dot(p.astype(BF16), v_ref[0, 0].astype(BF16))
    imp_ref[0, 0] = jnp.sum(p.reshape(t_new, NSA_GROUP, p.shape[-1]), axis=1)


def cmp_small(q_rows, ck, cv, bias, t_new):
    b, g, rows, d = q_rows.shape
    s = ck.shape[2]
    return pl.pallas_call(
        functools.partial(_cmp_small_kernel, t_new=t_new),
        out_shape=(jax.ShapeDtypeStruct((b, g, rows, d), F32), jax.ShapeDtypeStruct((b, g, t_new, s), F32)),
        grid=(b, g),
        in_specs=[
            pl.BlockSpec((1, 1, rows, d), lambda bi, gi: (bi, gi, 0, 0)),
            pl.BlockSpec((1, 1, s, d), lambda bi, gi: (bi, gi, 0, 0)),
            pl.BlockSpec((1, 1, s, d), lambda bi, gi: (bi, gi, 0, 0)),
            pl.BlockSpec((1, rows, s), lambda bi, gi: (gi, 0, 0)),
        ],
        out_specs=(pl.BlockSpec((1, 1, rows, d), lambda bi, gi: (bi, gi, 0, 0)),
                   pl.BlockSpec((1, 1, t_new, s), lambda bi, gi: (bi, gi, 0, 0))),
        compiler_params=_params("parallel", "parallel"),
        name="nsa_cmp_sample",
    )(q_rows, ck, cv, bias)


def _flash_small_kernel(q_ref, k_ref, v_ref, bias_ref, o_ref, m_sc, l_sc, acc_sc, *, nk):
    j = pl.program_id(2)

    @pl.when(j == 0)
    def _():
        m_sc[...] = jnp.full_like(m_sc, NEG_INF)
        l_sc[...] = jnp.zeros_like(l_sc)
        acc_sc[...] = jnp.zeros_like(acc_sc)

    s = _dot_nt(q_ref[0, 0], k_ref[0, 0].astype(BF16)) * SCALE + bias_ref[0, 0]
    vis = s > VISIBLE_MIN
    m_prev = m_sc[...]
    m_new = jnp.maximum(m_prev, jnp.max(s, axis=-1, keepdims=True))
    alpha = jnp.exp(m_prev - m_new)
    p = jnp.where(vis, jnp.exp(s - m_new), 0.0)
    l_sc[...] = alpha * l_sc[...] + jnp.sum(p, axis=-1, keepdims=True)
    acc_sc[...] = alpha * acc_sc[...] + _dot(p.astype(BF16), v_ref[0, 0].astype(BF16))
    m_sc[...] = m_new

    @pl.when(j == nk - 1)
    def _():
        o_ref[0, 0] = acc_sc[...] / jnp.maximum(l_sc[...], 1e-30)


def flash_small(q_rows, k, v, bias, tk):
    b, g, rows, d = q_rows.shape
    s = k.shape[2]
    assert s % tk == 0
    nk = s // tk
    return pl.pallas_call(
        functools.partial(_flash_small_kernel, nk=nk),
        out_shape=jax.ShapeDtypeStruct((b, g, rows, d), F32),
        grid=(b, g, nk),
        in_specs=[
            pl.BlockSpec((1, 1, rows, d), lambda bi, gi, j: (bi, gi, 0, 0)),
            pl.BlockSpec((1, 1, tk, d), lambda bi, gi, j: (bi, gi, j, 0)),
            pl.BlockSpec((1, 1, tk, d), lambda bi, gi, j: (bi, gi, j, 0)),
            pl.BlockSpec((1, 1, rows, tk), lambda bi, gi, j: (bi, gi, 0, j)),
        ],
        out_specs=pl.BlockSpec((1, 1, rows, d), lambda bi, gi, j: (bi, gi, 0, 0)),
        scratch_shapes=[pltpu.VMEM((rows, 1), F32), pltpu.VMEM((rows, 1), F32), pltpu.VMEM((rows, d), F32)],
        compiler_params=_params("parallel", "parallel", "arbitrary"),
        name="nsa_flash_sample",
    )(q_rows, k, v, bias)


def _topk_kernel(col_ref, row_ref, o_ref, *, n_top):
    a = col_ref[...]
    bb = row_ref[...]
    shape = (a.shape[0], a.shape[1], bb.shape[2])
    j_idx = lax.broadcasted_iota(jnp.int32, shape, 1)
    i_idx = lax.broadcasted_iota(jnp.int32, shape, 2)
    ahead = (a > bb) | ((a == bb) & (j_idx < i_idx))
    rank = jnp.sum(jnp.where(ahead, 1.0, 0.0), axis=1, keepdims=True)
    o_ref[...] = jnp.where(rank < n_top, 1.0, 0.0)


def topk_mask(score, n_top):
    r, ns = score.shape
    tr = 8 if r % 8 == 0 else r
    out = pl.pallas_call(
        functools.partial(_topk_kernel, n_top=n_top),
        out_shape=jax.ShapeDtypeStruct((r, 1, ns), F32),
        grid=(r // tr,),
        in_specs=[pl.BlockSpec((tr, ns, 1), lambda i: (i, 0, 0)), pl.BlockSpec((tr, 1, ns), lambda i: (i, 0, 0))],
        out_specs=pl.BlockSpec((tr, 1, ns), lambda i: (i, 0, 0)),
        compiler_params=_params("parallel"),
        name="nsa_topk",
    )(score.reshape(r, ns, 1), score.reshape(r, 1, ns))
    return out.reshape(r, ns)


def _pack_w_in0(w_in0):
    d = w_in0.shape[0]
    off_fz = 3 * FOX_W
    off_nq = off_fz + FOX_HEADS
    off_kv = off_nq + NSA_W
    off_ng = off_kv + 6 * NSA_KV_W
    parts = [w_in0[:, :off_fz], w_in0[:, off_nq:off_ng], w_in0[:, off_fz:off_nq], w_in0[:, off_ng:]]
    used = sum(p.shape[1] for p in parts)
    parts.append(jnp.zeros((d, L0_PACKED - used), w_in0.dtype))
    return jnp.concatenate(parts, axis=1).astype(BF16)


def _forget_terms(proj3, b_forget):
    fz = proj3[:, :, COL_SMALL:COL_SMALL + FOX_HEADS]
    lf = jax.nn.log_sigmoid(fz + b_forget.astype(F32))
    return lf, jnp.cumsum(lf, axis=1)


def _kv_cols(proj3, idx):
    off = COL_KV + idx * NSA_KV_W
    return proj3[:, :, off:off + NSA_KV_W]


def attn_prompt(h_bf16, b, t, wts):
    m = b * t
    g = NSA_KV_HEADS
    proj = matmul(h_bf16, wts["w_in0"])
    proj3 = proj.reshape(b, t, L0_PACKED)
    lf, c_new = _forget_terms(proj3, wts["b_forget"])
    o_fox = fox_prompt(proj3, c_new)

    kc, vc, ks, vs, kw, vw = [_kv_cols(proj3, i) for i in range(6)]
    nc, ns = t // CMP_BLOCK, t // SEL_BLOCK
    tk = min(NSA_TK, t)
    assert t % (2 * CMP_BLOCK) == 0 and t % tk == 0
    ck = compress(proj, COL_KV // HEAD_DIM, g, wts["cmp_k"]).reshape(g, b, nc, HEAD_DIM)
    cv = compress(proj, (COL_KV + NSA_KV_W) // HEAD_DIM, g, wts["cmp_v"]).reshape(g, b, nc, HEAD_DIM)
    even_odd = lambda a: jnp.concatenate([a[:, :, 0::2], a[:, :, 1::2]], axis=2)
    rel = wts["rel_bias"].astype(F32)
    blk_c = jnp.concatenate([jnp.arange(0, nc, 2, dtype=jnp.int32), jnp.arange(1, nc, 2, dtype=jnp.int32)])
    c_dist = jnp.arange(t, dtype=jnp.int32)[:, None] - ((blk_c + 1) * CMP_BLOCK - 1)[None, :]
    bias_cmp = jnp.moveaxis(_t5_bias(rel, c_dist), -1, 0)
    o_cmp, selmask = cmp_prompt(proj3, even_odd(ck), even_odd(cv), bias_cmp)

    n_near = -(-(_T5_THRESHOLDS[-1] + tk - 1) // NSA_TQ)
    tiles_slc = _band_tiles(rel, n_near + 1, tk)
    o_slc = nsa_band(proj3, tiles_slc, COL_KV + 2 * NSA_KV_W, COL_KV + 3 * NSA_KV_W, "causal", selmask)
    n_win = (tk // NSA_TQ) * (-(-WINDOW // tk) + 1)
    tiles_win = _band_tiles(rel, n_win, tk, window=WINDOW)
    o_win = nsa_band(proj3, tiles_win, COL_KV + 4 * NSA_KV_W, COL_KV + 5 * NSA_KV_W, "window")

    o_nsa = nsa_combine(proj, o_cmp.reshape(m, NSA_W), o_slc.reshape(m, NSA_W), o_win.reshape(m, NSA_W))
    mix = (o_fox.reshape(m, FOX_W), o_nsa)
    win_len = wts["win_len"]
    zeros = jnp.zeros((b, WINDOW, NSA_KV_W), F32)
    kw_a = jnp.concatenate([zeros, kw], axis=1)
    vw_a = jnp.concatenate([zeros, vw], axis=1)
    fk = proj3[:, :, COL_FK:COL_FK + FOX_W]
    fv = proj3[:, :, COL_FV:COL_FV + FOX_W]
    new = (fk, fv, lf, kc, vc, ks, vs, kw_a[:, kw_a.shape[1] - win_len:], vw_a[:, vw_a.shape[1] - win_len:])
    return mix, new


def attn_sample(h_bf16, b, t, wts, caches, page_table, state_win_k, state_win_v):
    m = b * t
    g, hg = NSA_KV_HEADS, NSA_GROUP
    n_pages = page_table.shape[1]
    p = n_pages * PAGE_SIZE
    assert t <= PAGE_SIZE and p % SEL_BLOCK == 0
    proj = matmul(h_bf16, wts["w_in0"])
    proj3 = proj.reshape(b, t, L0_PACKED)
    lf, c_new = _forget_terms(proj3, wts["b_forget"])
    kc, vc, ks, vs, kw, vw = [_kv_cols(proj3, i) for i in range(6)]
    fq = proj3[:, :, COL_FQ:COL_FQ + FOX_W]
    fk = proj3[:, :, COL_FK:COL_FK + FOX_W]
    fv = proj3[:, :, COL_FV:COL_FV + FOX_W]
    q_pos = p + np.arange(t)

    cache_fox_k, cache_fox_v, cache_fox_lf, cache_cmp_k, cache_cmp_v, cache_slc_k, cache_slc_v = caches
    tk_slc = min(2048, p)
    pcmp_k, pcmp_v, plf = gather_pools([cache_cmp_k, cache_cmp_v, cache_fox_lf], page_table, 0)
    pslc_k, pslc_v = gather_pools([cache_slc_k, cache_slc_v], page_table, tk_slc)

    c_past = plf - lax.cumsum(plf, axis=1, reverse=True)
    tp = -(-t // 8) * 8
    pad_new = lambda a: jnp.concatenate([a, jnp.zeros((b, tp - t) + a.shape[2:], F32)], axis=1)
    o_fox = fox_decode(fq.reshape(b, t * FOX_HEADS, HEAD_DIM).astype(BF16), c_new.reshape(b, t * FOX_HEADS, 1),
                       c_past.reshape(b, 1, p * FOX_HEADS), pad_new(c_new).reshape(b, 1, tp * FOX_HEADS),
                       cache_fox_k, cache_fox_v,
                       pad_new(fk.reshape(b, t, FOX_HEADS, HEAD_DIM)), pad_new(fv.reshape(b, t, FOX_HEADS, HEAD_DIM)),
                       page_table).reshape(m, FOX_W)

    rel = wts["rel_bias"].astype(F32)
    nq = proj3[:, :, COL_NQ:COL_NQ + NSA_W].reshape(b, t, g, hg, HEAD_DIM)
    q_rows = nq.transpose(0, 2, 1, 3, 4).reshape(b, g, t * hg, HEAD_DIM).astype(BF16)

    def head_rows(x):
        tt, ss = x.shape[0], x.shape[1]
        return x.reshape(tt, ss, g, hg).transpose(2, 0, 3, 1).reshape(g, tt * hg, ss)

    l_tot = p + t
    l_pad = -(-l_tot // SEL_BLOCK) * SEL_BLOCK
    nc, ns = l_pad // CMP_BLOCK, l_pad // SEL_BLOCK
    nc_past = p // CMP_BLOCK
    assert nc_past % 2 == 0 and (nc_past + 1) * CMP_BLOCK - 1 > q_pos[-1]
    ck = compress(pcmp_k.reshape(b * g * p, HEAD_DIM), 0, 1, wts["cmp_k"]).reshape(b, g, nc_past, HEAD_DIM)
    cv = compress(pcmp_v.reshape(b * g * p, HEAD_DIM), 0, 1, wts["cmp_v"]).reshape(b, g, nc_past, HEAD_DIM)
    even_odd = lambda a: jnp.concatenate([a[:, :, 0::2], a[:, :, 1::2]], axis=2)
    blk_c = jnp.concatenate([jnp.arange(0, nc_past, 2, dtype=jnp.int32), jnp.arange(1, nc_past, 2, dtype=jnp.int32)])
    c_dist = jnp.asarray(q_pos, jnp.int32)[:, None] - ((blk_c + 1) * CMP_BLOCK - 1)[None, :]
    bias_cmp = jnp.where((c_dist >= 0)[..., None], _t5_bias(rel, c_dist), NEG_INF)
    o_cmp, imp = cmp_small(q_rows, even_odd(ck), even_odd(cv), head_rows(bias_cmp), t)
    ns_past = nc_past // 2
    imp_s = imp[..., :ns_past] + imp[..., ns_past:]
    imp_s = jnp.concatenate([imp_s, jnp.zeros((b, g, t, ns - ns_past), F32)], axis=-1)
    blk = np.arange(ns)[None, :]
    cur = (q_pos // SEL_BLOCK)[:, None]
    valid = blk * SEL_BLOCK <= q_pos[:, None]
    forced = (blk == 0) | (blk == cur) | (blk == cur - 1)
    score = jnp.where(jnp.asarray(forced), SEL_FORCE, jnp.where(jnp.asarray(valid), imp_s, -1.0))
    sel = topk_mask(score.reshape(b * g * t, ns), min(N_SEL, ns)).reshape(b, g, t, ns)

    s_slc = p + tk_slc
    qp = jnp.asarray(q_pos, jnp.int32)[:, None]
    key_pos = jnp.arange(s_slc, dtype=jnp.int32)[None, :]
    d_slc = qp - key_pos
    ok = (d_slc >= 0) & (key_pos < l_tot)
    bias_slc = head_rows(jnp.where(ok[..., None], _t5_bias(rel, d_slc), NEG_INF))
    n_full = p // SEL_BLOCK
    sel_keys = jnp.concatenate([jnp.repeat(sel[..., :n_full], SEL_BLOCK, axis=-1),
                                jnp.broadcast_to(sel[..., ns - 1:ns], (b, g, t, tk_slc))], axis=-1)
    assert ns - 1 == n_full
    sel_rows = jnp.broadcast_to(sel_keys[:, :, :, None, :], (b, g, t, hg, s_slc)).reshape(b, g, t * hg, s_slc)
    bias_slc = jnp.where(sel_rows > 0.5, bias_slc[None], NEG_INF)

    def head_major(a, rows):
        a = a.reshape(b, t, g, HEAD_DIM).transpose(0, 2, 1, 3)
        return jnp.concatenate([a, jnp.zeros((b, g, rows - t, HEAD_DIM), F32)], axis=2)

    pslc_k = lax.dynamic_update_slice(pslc_k, head_major(ks, tk_slc), (0, 0, p, 0))
    pslc_v = lax.dynamic_update_slice(pslc_v, head_major(vs, tk_slc), (0, 0, p, 0))
    o_slc = flash_small(q_rows, pslc_k, pslc_v, bias_slc, tk_slc)

    p_w = state_win_k.shape[1]
    s_win = -(-(p_w + t) // 128) * 128
    kw_a = jnp.concatenate([state_win_k.reshape(b, p_w, NSA_KV_W), kw], axis=1)
    vw_a = jnp.concatenate([state_win_v.reshape(b, p_w, NSA_KV_W), vw], axis=1)
    win_major = lambda st, new: jnp.concatenate([st.transpose(0, 2, 1, 3), head_major(new, s_win - p_w)], axis=2)
    w_idx = jnp.arange(s_win, dtype=jnp.int32)[None, :]
    wpos = (p - p_w) + w_idx
    wd = qp - wpos
    wok = (wd >= 0) & (wd < WINDOW) & (wpos >= 0) & (w_idx < p_w + t)
    bias_win = head_rows(jnp.where(wok[..., None], _t5_bias(rel, wd), NEG_INF))
    bias_win = jnp.broadcast_to(bias_win[None], (b,) + bias_win.shape)
    o_win = flash_small(q_rows, win_major(state_win_k, kw), win_major(state_win_v, vw), bias_win, s_win)

    def to_tokens(o):
        return o.reshape(b, g, t, hg, HEAD_DIM).transpose(0, 2, 1, 3, 4).reshape(m, NSA_W)

    o_nsa = nsa_combine(proj, to_tokens(o_cmp), to_tokens(o_slc), to_tokens(o_win))
    mix = (o_fox.astype(BF16), o_nsa)
    win_len = wts["win_len"]
    new = (fk, fv, lf, kc, vc, ks, vs, kw_a[:, kw_a.shape[1] - win_len:], vw_a[:, vw_a.shape[1] - win_len:])
    return mix, new


def trunk(x, p_emb, wts, lru_conv0, lru_h0, ffn_conv0, past=None):
    b, t, d = x.shape
    m = b * t
    x2 = x.reshape(m, d)
    ffn_bufs = []
    for i in range(2):
        h = rmsnorm(x2, wts["norm_mix"][i], BF16)
        if i == 0:
            if past is None:
                mix, attn_new = attn_prompt(h, b, t, wts)
            else:
                mix, attn_new = attn_sample(h, b, t, wts, *past)
            x2 = matmul_pair(mix[0], mix[1], wts["w_out0"], x2)
        else:
            proj1 = matmul(h, wts["w_in1"]).reshape(b, t, -1)
            tp = -(-t // 8) * 8
            if tp != t:
                proj1 = jnp.concatenate([proj1, jnp.zeros((b, tp - t, proj1.shape[-1]), F32)], axis=1)
            gated, h_last, lru_buf = lru_block(proj1, wts["lru_conv_w"], wts["lru_conv_b"], wts["lru_w_a"],
                                               wts["lru_b_a"], wts["lru_w_x"], wts["lru_b_x"], wts["lru_lambda"],
                                               lru_conv0, lru_h0, t)
            x2 = matmul(gated[:, :t].reshape(m, -1), wts["w_out1"], res=x2)
        hf = rmsnorm(x2, wts["norm_ffn"][i], BF16)
        dff = wts["ffn_w_up"][i].shape[1] // 2
        if t % 8 == 0:
            act, buf = ffn_up_fused(hf, wts["ffn_w_up"][i], wts["ffn_conv_w"][i], wts["ffn_conv_b"][i],
                                    ffn_conv0[i], t)
        else:
            gu = matmul(hf, wts["ffn_w_up"][i])
            xp = jnp.concatenate([ffn_conv0[i], gu[:, :dff].reshape(b, t, dff)], axis=1)
            taps = [xp[:, FFN_CONV - 1 - s:FFN_CONV - 1 - s + t].reshape(m, dff) for s in range(FFN_CONV)]
            act = convgate(taps[0], taps[1], taps[2], gu[:, dff:], wts["ffn_conv_w"][i], wts["ffn_conv_b"][i])
            buf = xp[:, xp.shape[1] - (FFN_CONV - 1):]
        ffn_bufs.append(buf)
        x2 = matmul(act, wts["ffn_w_down"][i], res=x2)
        hp = rmsnorm(x2, wts["ple_norm"][i], BF16)
        emb = matmul(p_emb[i].reshape(m, -1).astype(BF16), wts["ple_w_proj"][i])
        x2 = matmul(hp, wts["ple_w_gate"][i], res=x2, aux=emb)
    y = rmsnorm(x2, wts["final_norm"], F32).reshape(b, t, d)
    return y, attn_new, lru_buf, h_last.reshape(b, -1), jnp.stack(ffn_bufs)


def kernel(x_prompt, x_sample, cache_fox_k, cache_fox_v, cache_fox_lf, cache_cmp_k, cache_cmp_v, cache_slc_k, cache_slc_v, state_win_k, state_win_v, state_lru_conv, state_lru_h, state_ffn_conv, page_table, p_prompt, p_sample, norm_mix, norm_ffn, final_norm, w_in0, b_forget, cmp_w1_k, cmp_pe_k, cmp_w2_k, cmp_w1_v, cmp_pe_v, cmp_w2_v, rel_bias, w_out0, w_in1, lru_conv_w, lru_conv_b, lru_w_a, lru_b_a, lru_w_x, lru_b_x, lru_lambda, w_out1, ffn_w_up, ffn_conv_w, ffn_conv_b, ffn_w_down, ple_w_proj, ple_w_gate, ple_norm):
    depth = norm_mix.shape[0]
    assert depth == 2
    d = x_prompt.shape[-1]
    dff = ffn_w_down.shape[1]
    wts = {
        "norm_mix": norm_mix, "norm_ffn": norm_ffn, "ple_norm": ple_norm, "final_norm": final_norm,
        "w_in0": _pack_w_in0(w_in0), "b_forget": b_forget, "rel_bias": rel_bias,
        "win_len": state_win_k.shape[1],
        "w_out0": w_out0.astype(BF16), "w_in1": w_in1.astype(BF16), "w_out1": w_out1.astype(BF16),
        "lru_conv_w": lru_conv_w, "lru_conv_b": lru_conv_b, "lru_w_a": lru_w_a, "lru_b_a": lru_b_a,
        "lru_w_x": lru_w_x, "lru_b_x": lru_b_x, "lru_lambda": lru_lambda,
        "ffn_w_up": ffn_w_up.astype(BF16), "ffn_conv_w": ffn_conv_w, "ffn_conv_b": ffn_conv_b,
        "ffn_w_down": ffn_w_down.astype(BF16), "ple_w_proj": ple_w_proj.astype(BF16),
        "ple_w_gate": ple_w_gate.astype(BF16),
    }
    wts["cmp_k"] = (cmp_w1_k, cmp_pe_k, cmp_w2_k)
    wts["cmp_v"] = (cmp_w1_v, cmp_pe_v, cmp_w2_v)

    bp = x_prompt.shape[0]
    y_p, attn_p, lru_conv_p, lru_h_p, ffn_conv_p = trunk(
        x_prompt, p_prompt, wts,
        jnp.zeros((bp, LRU_CONV - 1, d), F32), jnp.zeros((bp, d), F32),
        jnp.zeros((depth, bp, FFN_CONV - 1, dff), F32))
    caches = (cache_fox_k, cache_fox_v, cache_fox_lf, cache_cmp_k, cache_cmp_v, cache_slc_k, cache_slc_v)
    y_s, attn_s, lru_conv_s, lru_h_s, ffn_conv_s = trunk(
        x_sample, p_sample, wts, state_lru_conv, state_lru_h, state_ffn_conv,
        past=(caches, page_table, state_win_k, state_win_v))

    def shape_attn(new, b, t):
        fk, fv, lf, kc, vc, ks, vs, wk, wv = new
        h4 = lambda a: a.reshape(b, a.shape[1], FOX_HEADS, HEAD_DIM)
        g4 = lambda a: a.reshape(b, a.shape[1], NSA_KV_HEADS, HEAD_DIM)
        return h4(fk), h4(fv), lf, g4(kc), g4(vc), g4(ks), g4(vs), g4(wk), g4(wv)

    ap = shape_attn(attn_p, bp, x_prompt.shape[1])
    asmp = shape_attn(attn_s, x_sample.shape[0], x_sample.shape[1])
    out = [y_p, y_s]
    for a, s in zip(ap, asmp):
        out += [a, s]
    out += [lru_conv_p, lru_conv_s, lru_h_p, lru_h_s, ffn_conv_p, ffn_conv_s]
    return tuple(out)
```

```python
import functools
import math

import numpy as np
import jax
import jax.numpy as jnp
from jax import lax
from jax.experimental import pallas as pl
from jax.experimental.pallas import tpu as pltpu

PAGE_SIZE = 128
HEAD_DIM = 128
FOX_HEADS = 16
NSA_HEADS = 16
NSA_KV_HEADS = 2
NSA_GROUP = NSA_HEADS // NSA_KV_HEADS
CMP_BLOCK = 32
SEL_BLOCK = 64
N_SEL = 16
WINDOW = 512
SEL_FORCE = 1000.0
N_BUCKETS = 32
MAX_DISTANCE = 1024
LRU_HEADS = 16
LRU_C = 8.0
LRU_CONV = 4
FFN_CONV = 3
EPS = 1e-6
NEG_INF = -1e30
VISIBLE_MIN = -5e29

FOX_W = FOX_HEADS * HEAD_DIM
NSA_W = NSA_HEADS * HEAD_DIM
NSA_KV_W = NSA_KV_HEADS * HEAD_DIM
GROUP_W = NSA_GROUP * HEAD_DIM
SCALE = HEAD_DIM ** -0.5

COL_FQ, COL_FK, COL_FV = 0, FOX_W, 2 * FOX_W
COL_NQ = 3 * FOX_W
COL_KV = COL_NQ + NSA_W
COL_SMALL = COL_KV + 6 * NSA_KV_W
L0_PACKED = -(-(COL_SMALL + 128) // 1024) * 1024

F32 = jnp.float32
BF16 = jnp.bfloat16

VMEM_LIMIT_BYTES = 56 * 1024 * 1024


def _params(*sem):
    return pltpu.CompilerParams(dimension_semantics=sem, vmem_limit_bytes=VMEM_LIMIT_BYTES)


def _dot(a, b):
    return jnp.dot(a, b, preferred_element_type=F32)


def _dot_nt(a, b):
    return lax.dot_general(a, b, (((1,), (1,)), ((), ())), preferred_element_type=F32)


def _gelu(x):
    c = math.sqrt(2.0 / math.pi)
    return 0.5 * x * (1.0 + jnp.tanh(c * (x + 0.044715 * (x * x * x))))


def _sigmoid(x):
    return 1.0 / (1.0 + jnp.exp(-x))


def _online_softmax_step(s, v, m_sc, l_sc, acc_sc):
    reps = s.shape[1] // 128
    m_prev = m_sc[...]
    m_new = jnp.maximum(m_prev, jnp.max(s, axis=-1, keepdims=True))
    alpha = jnp.exp(m_prev - m_new)
    m_wide = m_new if reps == 1 else jnp.concatenate([m_new] * reps, axis=1)
    p = jnp.exp(s - m_wide)
    l_sc[...] = alpha * l_sc[...] + jnp.sum(p, axis=-1, keepdims=True)
    acc_sc[...] = alpha * acc_sc[...] + _dot(p.astype(BF16), v)
    m_sc[...] = m_new


def _bucket_np(dist):
    n = np.maximum(np.asarray(dist, np.int64), 0)
    exact = N_BUCKETS // 2
    nf = np.maximum(n, 1).astype(np.float64)
    large = exact + (np.log(nf / exact) / math.log(MAX_DISTANCE / exact) * (N_BUCKETS - exact)).astype(np.int64)
    return np.where(n < exact, n, np.minimum(large, N_BUCKETS - 1)).astype(np.int32)


def _rmsnorm_kernel(x_ref, g_ref, o_ref):
    x = x_ref[...]
    y = x * lax.rsqrt(jnp.mean(x * x, axis=-1, keepdims=True) + EPS)
    o_ref[...] = (y * g_ref[...]).astype(o_ref.dtype)


def rmsnorm(x, g, out_dtype):
    m, d = x.shape
    tm = min(m, 256)
    return pl.pallas_call(
        _rmsnorm_kernel,
        out_shape=jax.ShapeDtypeStruct((m, d), out_dtype),
        grid=(m // tm,),
        in_specs=[pl.BlockSpec((tm, d), lambda i: (i, 0)), pl.BlockSpec((1, d), lambda i: (0, 0))],
        out_specs=pl.BlockSpec((tm, d), lambda i: (i, 0)),
        compiler_params=_params("parallel"),
        name="rmsnorm",
    )(x, g.reshape(1, d))


def _mm_kernel(*refs, nk, epilogue):
    a_ref, w_ref = refs[0], refs[1]
    o_ref = refs[-1]
    k = pl.program_id(2)

    @pl.when(k == 0)
    def _():
        o_ref[...] = jnp.zeros_like(o_ref)

    o_ref[...] += _dot(a_ref[...], w_ref[...])

    if epilogue is not None:
        @pl.when(k == nk - 1)
        def _():
            acc = o_ref[...]
            if epilogue == "res":
                o_ref[...] = refs[2][...] + acc
            else:
                o_ref[...] = refs[2][...] + _sigmoid(acc) * refs[3][...]


def matmul(a, w, *, res=None, aux=None, tm=1024, tn=1024, tk=2048):
    m, kdim = a.shape
    n = w.shape[1]
    tm, tn, tk = min(tm, m), min(tn, n), min(tk, kdim)
    assert m % tm == 0 and n % tn == 0 and kdim % tk == 0
    nk = kdim // tk
    epilogue = None if res is None else ("res" if aux is None else "ple")
    in_specs = [pl.BlockSpec((tm, tk), lambda i, j, k: (i, k)), pl.BlockSpec((tk, tn), lambda i, j, k: (k, j))]
    args = [a, w]
    for extra in (res, aux):
        if extra is not None:
            in_specs.append(pl.BlockSpec((tm, tn), lambda i, j, k: (i, j)))
            args.append(extra)
    return pl.pallas_call(
        functools.partial(_mm_kernel, nk=nk, epilogue=epilogue),
        out_shape=jax.ShapeDtypeStruct((m, n), F32),
        grid=(m // tm, n // tn, nk),
        in_specs=in_specs,
        out_specs=pl.BlockSpec((tm, tn), lambda i, j, k: (i, j)),
        compiler_params=_params("parallel", "parallel", "arbitrary"),
        name="matmul_" + (epilogue or "plain"),
    )(*args)


def _mm_pair_kernel(a0_ref, a1_ref, w_ref, res_ref, o_ref):
    k = pl.program_id(2)

    @pl.when(k == 0)
    def _():
        o_ref[...] = res_ref[...] + _dot(a0_ref[...], w_ref[...])

    @pl.when(k == 1)
    def _():
        o_ref[...] += _dot(a1_ref[...], w_ref[...])


def matmul_pair(a0, a1, w, res, *, tm=1024, tn=1024):
    m, k0 = a0.shape
    assert a1.shape == (m, k0) and w.shape[0] == 2 * k0
    n = w.shape[1]
    tm, tn = min(tm, m), min(tn, n)
    assert m % tm == 0 and n % tn == 0
    piece = pl.BlockSpec((tm, k0), lambda i, j, k: (i, 0))
    tile = pl.BlockSpec((tm, tn), lambda i, j, k: (i, j))
    return pl.pallas_call(
        _mm_pair_kernel,
        out_shape=jax.ShapeDtypeStruct((m, n), F32),
        grid=(m // tm, n // tn, 2),
        in_specs=[piece, piece, pl.BlockSpec((k0, tn), lambda i, j, k: (k, j)), tile],
        out_specs=tile,
        compiler_params=_params("parallel", "parallel", "arbitrary"),
        name="matmul_pair",
    )(a0, a1, w, res)


def _ffn_up_kernel(a_ref, wg_ref, wu_ref, cw_ref, cb_ref, buf_ref, act_ref, tail_ref,
                   accg, accu, gbuf, *, nk, tiles_per_seq, tm):
    i = pl.program_id(1)
    k = pl.program_id(2)
    kw = FFN_CONV

    @pl.when(k == 0)
    def _():
        accg[...] = jnp.zeros_like(accg)
        accu[...] = jnp.zeros_like(accu)

    a = a_ref[...]
    accg[...] += _dot(a, wg_ref[...])
    accu[...] += _dot(a, wu_ref[...])

    @pl.when(k == nk - 1)
    def _():
        @pl.when(i % tiles_per_seq == 0)
        def _():
            gbuf[8 - (kw - 1):8, :] = buf_ref[0]

        g = accg[...]
        gbuf[8:8 + tm, :] = g
        y = g * cw_ref[kw - 1:kw, :] + cb_ref[...]
        for t in range(kw - 1):
            off = 8 - (kw - 1) + t
            y = y + gbuf[off:off + tm, :] * cw_ref[t:t + 1, :]
        act_ref[...] = (_gelu(y) * accu[...]).astype(act_ref.dtype)
        tail = gbuf[8 + tm - (kw - 1):8 + tm, :]
        gbuf[8 - (kw - 1):8, :] = tail
        tail_ref[0] = tail


def ffn_up_fused(a, w_up, conv_w, conv_b, conv_buf, seq_len, *, tm=1024, tn=1024, tk=2048):
    m, kdim = a.shape
    dff = w_up.shape[1] // 2
    tm, tn, tk = min(tm, seq_len), min(tn, dff), min(tk, kdim)
    assert seq_len % tm == 0 and dff % tn == 0 and kdim % tk == 0 and tm % 8 == 0
    nb = m // seq_len
    tps = seq_len // tm
    nj, nk = dff // tn, kdim // tk
    kw = FFN_CONV
    kern = functools.partial(_ffn_up_kernel, nk=nk, tiles_per_seq=tps, tm=tm)
    return pl.pallas_call(
        kern,
        out_shape=(jax.ShapeDtypeStruct((m, dff), BF16), jax.ShapeDtypeStruct((nb, kw - 1, dff), F32)),
        grid=(nj, m // tm, nk),
        in_specs=[
            pl.BlockSpec((tm, tk), lambda j, i, k: (i, k)),
            pl.BlockSpec((tk, tn), lambda j, i, k: (k, j)),
            pl.BlockSpec((tk, tn), lambda j, i, k: (k, j + nj)),
            pl.BlockSpec((kw, tn), lambda j, i, k: (0, j)),
            pl.BlockSpec((1, tn), lambda j, i, k: (0, j)),
            pl.BlockSpec((1, kw - 1, tn), lambda j, i, k: (i // tps, 0, j)),
        ],
        out_specs=(
            pl.BlockSpec((tm, tn), lambda j, i, k: (i, j)),
            pl.BlockSpec((1, kw - 1, tn), lambda j, i, k: (i // tps, 0, j)),
        ),
        scratch_shapes=[pltpu.VMEM((tm, tn), F32), pltpu.VMEM((tm, tn), F32), pltpu.VMEM((8 + tm, tn), F32)],
        compiler_params=_params("parallel", "arbitrary", "arbitrary"),
        name="ffn_up_fused",
    )(a, w_up, w_up, conv_w, conv_b.reshape(1, dff), conv_buf)


def _convgate_kernel(s0_ref, s1_ref, s2_ref, u_ref, cw_ref, cb_ref, o_ref):
    y = s0_ref[...] * cw_ref[2:3, :] + cb_ref[...]
    y = y + s2_ref[...] * cw_ref[0:1, :]
    y = y + s1_ref[...] * cw_ref[1:2, :]
    o_ref[...] = (_gelu(y) * u_ref[...]).astype(o_ref.dtype)


def convgate(s0, s1, s2, u, conv_w, conv_b, *, tn=2048):
    m, dff = s0.shape
    tn = min(tn, dff)
    row = pl.BlockSpec((m, tn), lambda j: (0, j))
    return pl.pallas_call(
        _convgate_kernel,
        out_shape=jax.ShapeDtypeStruct((m, dff), BF16),
        grid=(dff // tn,),
        in_specs=[row, row, row, row, pl.BlockSpec((FFN_CONV, tn), lambda j: (0, j)),
                  pl.BlockSpec((1, tn), lambda j: (0, j))],
        out_specs=row,
        compiler_params=_params("parallel"),
        name="convgate",
    )(s0, s1, s2, u, conv_w, conv_b.reshape(1, dff))


def _fox_prompt_kernel(q_ref, k_ref, v_ref, cq_ref, ck_ref, o_ref, m_sc, l_sc, acc_sc, cq_sc, *, nk, tq, tk):
    qi = pl.program_id(2)
    kj = pl.program_id(3)

    @pl.when(kj == 0)
    def _():
        m_sc[...] = jnp.full_like(m_sc, NEG_INF)
        l_sc[...] = jnp.zeros_like(l_sc)
        acc_sc[...] = jnp.zeros_like(acc_sc)
        cq_sc[...] = jnp.transpose(jnp.broadcast_to(cq_ref[0, 0], (128, tq)))

    @pl.when(kj * tk <= qi * tq + tq - 1)
    def _():
        q = q_ref[0].astype(BF16)
        k = k_ref[0].astype(BF16)
        cq = jnp.concatenate([cq_sc[...]] * (tk // 128), axis=1)
        s = _dot_nt(q, k) * SCALE + (cq - ck_ref[0, 0])
        row = qi * tq + lax.broadcasted_iota(jnp.int32, (tq, tk), 0)
        col = kj * tk + lax.broadcasted_iota(jnp.int32, (tq, tk), 1)
        s = jnp.where(row >= col, s, NEG_INF)
        _online_softmax_step(s, v_ref[0].astype(BF16), m_sc, l_sc, acc_sc)

    @pl.when(kj == nk - 1)
    def _():
        o_ref[0] = (acc_sc[...] / jnp.maximum(l_sc[...], 1e-30)).astype(o_ref.dtype)


def fox_prompt(proj3, c_new):
    b, t, _ = proj3.shape
    tq = min(512, t)
    tk = min(1024, t)
    assert tq % 128 == 0 and tk % 128 == 0
    nq, nk = t // tq, t // tk
    h = FOX_HEADS
    c_row = c_new.transpose(0, 2, 1).reshape(b, h, 1, t)
    kq, kk, kv = COL_FQ // HEAD_DIM, COL_FK // HEAD_DIM, COL_FV // HEAD_DIM

    def kidx(qi, kj):
        return jnp.minimum(kj, (qi * tq + tq - 1) // tk)

    kern = functools.partial(_fox_prompt_kernel, nk=nk, tq=tq, tk=tk)
    return pl.pallas_call(
        kern,
        out_shape=jax.ShapeDtypeStruct((b, t, FOX_W), BF16),
        grid=(b, h, nq, nk),
        in_specs=[
            pl.BlockSpec((1, tq, HEAD_DIM), lambda bi, hi, qi, kj: (bi, qi, kq + hi)),
            pl.BlockSpec((1, tk, HEAD_DIM), lambda bi, hi, qi, kj: (bi, kidx(qi, kj), kk + hi)),
            pl.BlockSpec((1, tk, HEAD_DIM), lambda bi, hi, qi, kj: (bi, kidx(qi, kj), kv + hi)),
            pl.BlockSpec((1, 1, 1, tq), lambda bi, hi, qi, kj: (bi, hi, 0, qi)),
            pl.BlockSpec((1, 1, 1, tk), lambda bi, hi, qi, kj: (bi, hi, 0, kidx(qi, kj))),
        ],
        out_specs=pl.BlockSpec((1, tq, HEAD_DIM), lambda bi, hi, qi, kj: (bi, qi, hi)),
        scratch_shapes=[pltpu.VMEM((tq, 128), F32), pltpu.VMEM((tq, 128), F32), pltpu.VMEM((tq, HEAD_DIM), F32),
                        pltpu.VMEM((tq, 128), F32)],
        compiler_params=_params("parallel", "parallel", "parallel", "arbitrary"),
        name="fox_prompt",
    )(proj3, proj3, proj3, c_row, c_row)


def _compress_kernel(x_ref, pe_ref, w1_ref, w2_ref, o_ref, *, tr):
    acc = jnp.zeros((tr, HEAD_DIM), F32)
    for l in range(CMP_BLOCK):
        xl = x_ref[pl.ds(l, tr, stride=CMP_BLOCK), :] + pe_ref[l:l + 1, :]
        acc = acc + _dot(xl.astype(BF16), w1_ref[l])
    o_ref[0] = _dot(_gelu(acc).astype(BF16), w2_ref[...])


def compress(x2d, col_block, n_groups, cmp_w):
    w1, pe, w2 = cmp_w
    rows = x2d.shape[0]
    r = rows // CMP_BLOCK
    tr = min(r, 256)
    assert r % tr == 0 and rows % CMP_BLOCK == 0
    return pl.pallas_call(
        functools.partial(_compress_kernel, tr=tr),
        out_shape=jax.ShapeDtypeStruct((n_groups, r, HEAD_DIM), F32),
        grid=(n_groups, r // tr),
        in_specs=[pl.BlockSpec((tr * CMP_BLOCK, HEAD_DIM), lambda g, i: (i, col_block + g)),
                  pl.BlockSpec((CMP_BLOCK, HEAD_DIM), lambda g, i: (0, 0)),
                  pl.BlockSpec((CMP_BLOCK, HEAD_DIM, HEAD_DIM), lambda g, i: (0, 0, 0)),
                  pl.BlockSpec((HEAD_DIM, HEAD_DIM), lambda g, i: (0, 0))],
        out_specs=pl.BlockSpec((1, tr, HEAD_DIM), lambda g, i: (g, i, 0)),
        compiler_params=_params("parallel", "parallel"),
        name="nsa_compress",
    )(x2d, pe, w1.astype(BF16), w2.astype(BF16))


def _cmp_prompt_kernel(q_ref, ck_ref, cv_ref, bias_ref, exp_ref, o_ref, sel_ref, *, tq, nc, n_top):
    qi = pl.program_id(2)
    ns = nc // 2
    t_col = qi * tq + lax.broadcasted_iota(jnp.int32, (tq, nc), 0)
    lane = lax.broadcasted_iota(jnp.int32, (tq, nc), 1)
    blk_c = jnp.where(lane < ns, 2 * lane, 2 * (lane - ns) + 1)
    mask = t_col >= (blk_c + 1) * CMP_BLOCK - 1
    ck = ck_ref[0, 0].astype(BF16)
    cv = cv_ref[0, 0].astype(BF16)
    imp = jnp.zeros((tq, nc), F32)
    for hg in range(NSA_GROUP):
        sl = slice(hg * HEAD_DIM, (hg + 1) * HEAD_DIM)
        s = _dot_nt(q_ref[0, :, sl].astype(BF16), ck) * SCALE + bias_ref[hg]
        s = jnp.where(mask, s, NEG_INF)
        m = jnp.max(s, axis=-1, keepdims=True)
        e = jnp.where(mask, jnp.exp(s - m), 0.0)
        p = e / jnp.maximum(jnp.sum(e, axis=-1, keepdims=True), 1e-30)
        imp = imp + p
        o_ref[0, :, sl] = _dot(p.astype(BF16), cv)
    imp_s = imp[:, :ns] + imp[:, ns:]
    t_s = qi * tq + lax.broadcasted_iota(jnp.int32, (tq, ns), 0)
    blk = lax.broadcasted_iota(jnp.int32, (tq, ns), 1)
    cur = t_s // SEL_BLOCK
    valid = blk * SEL_BLOCK <= t_s
    forced = (blk == 0) | (blk == cur) | (blk == cur - 1)
    score = jnp.where(forced, SEL_FORCE, jnp.where(valid, imp_s, -1.0))
    rank = jnp.zeros((tq, ns), F32)
    for j in range(ns):
        cj = score[:, j:j + 1]
        ahead = (cj > score) | ((cj == score) & (blk > j))
        rank = rank + jnp.where(ahead, 1.0, 0.0)
    sel = jnp.where(rank < n_top, 1.0, 0.0).astype(BF16)
    sel_ref[0, 0] = (1.0 - _dot(sel, exp_ref[...])) * NEG_INF


def cmp_prompt(proj3, ck_perm, cv_perm, bias_cmp):
    b, t, _ = proj3.shape
    nc = t // CMP_BLOCK
    ns = t // SEL_BLOCK
    tq = min(256, t)
    g = NSA_KV_HEADS
    n_top = min(N_SEL, ns)
    expand = jnp.asarray(np.repeat(np.eye(ns, dtype=np.float32), SEL_BLOCK, axis=1), BF16)
    kq = COL_NQ // GROUP_W
    kern = functools.partial(_cmp_prompt_kernel, tq=tq, nc=nc, n_top=n_top)
    return pl.pallas_call(
        kern,
        out_shape=(jax.ShapeDtypeStruct((b, t, NSA_W), F32), jax.ShapeDtypeStruct((b, g, t, t), F32)),
        grid=(b, g, t // tq),
        in_specs=[
            pl.BlockSpec((1, tq, GROUP_W), lambda bi, gi, qi: (bi, qi, kq + gi)),
            pl.BlockSpec((1, 1, nc, HEAD_DIM), lambda bi, gi, qi: (gi, bi, 0, 0)),
            pl.BlockSpec((1, 1, nc, HEAD_DIM), lambda bi, gi, qi: (gi, bi, 0, 0)),
            pl.BlockSpec((NSA_GROUP, tq, nc), lambda bi, gi, qi: (gi, qi, 0)),
            pl.BlockSpec((ns, t), lambda bi, gi, qi: (0, 0)),
        ],
        out_specs=(
            pl.BlockSpec((1, tq, GROUP_W), lambda bi, gi, qi: (bi, qi, gi)),
            pl.BlockSpec((1, 1, tq, t), lambda bi, gi, qi: (bi, gi, qi, 0)),
        ),
        compiler_params=_params("parallel", "parallel", "parallel"),
        name="nsa_cmp_prompt",
    )(proj3, ck_perm, cv_perm, bias_cmp, expand)


NSA_TQ = 128
NSA_TK = 512


def _nsa_band_kernel(*refs, nr, ratio, mode, has_sel):
    if has_sel:
        q_ref, k_ref, v_ref, bias_ref, sel_ref, o_ref, qs, m_sc, l_sc, acc_sc = refs
    else:
        q_ref, k_ref, v_ref, bias_ref, o_ref, qs, m_sc, l_sc, acc_sc = refs
        sel_ref = None
    qi = pl.program_id(2)
    r = pl.program_id(3)
    tq = NSA_TQ
    tk = k_ref.shape[1]
    hg_n = NSA_GROUP

    @pl.when(r == 0)
    def _():
        for hg in range(hg_n):
            qs[hg * tq:(hg + 1) * tq, :] = q_ref[0, :, hg * HEAD_DIM:(hg + 1) * HEAD_DIM].astype(BF16)
        m_sc[...] = jnp.full_like(m_sc, NEG_INF)
        l_sc[...] = jnp.zeros_like(l_sc)
        acc_sc[...] = jnp.zeros_like(acc_sc)

    active = (r <= qi // ratio) if mode == "causal" else (qi // ratio - (nr - 1) + r >= 0)

    @pl.when(active)
    def _():
        s = _dot_nt(qs[...], k_ref[0].astype(BF16)) * SCALE
        s3 = s.reshape(hg_n, tq, tk) + bias_ref[:, 0]
        if has_sel:
            s3 = s3 + sel_ref[0, 0][None]
        _online_softmax_step(s3.reshape(hg_n * tq, tk), v_ref[0].astype(BF16), m_sc, l_sc, acc_sc)

    @pl.when(r == nr - 1)
    def _():
        o = acc_sc[...] / jnp.maximum(l_sc[...], 1e-30)
        for hg in range(hg_n):
            o_ref[0, :, hg * HEAD_DIM:(hg + 1) * HEAD_DIM] = o[hg * tq:(hg + 1) * tq, :]


def nsa_band(proj3, bias_tiles, col_k, col_v, mode, selmask=None):
    b, t, _ = proj3.shape
    tq = NSA_TQ
    tk = bias_tiles.shape[-1]
    assert t % tk == 0 and tk % tq == 0
    ratio = tk // tq
    nq, nk = t // tq, t // tk
    g = NSA_KV_HEADS
    n_tiles = bias_tiles.shape[1]
    nr = nk if mode == "causal" else -(-WINDOW // tk) + 1
    kq = COL_NQ // GROUP_W
    ck, cv = col_k // HEAD_DIM, col_v // HEAD_DIM
    if mode == "causal":
        def ktile(qi, r):
            return jnp.minimum(r, qi // ratio)
    else:
        def ktile(qi, r):
            return jnp.maximum(qi // ratio - (nr - 1) + r, 0)

    def btile(qi, r):
        return jnp.clip(qi - ratio * ktile(qi, r), 0, n_tiles - 1)

    in_specs = [
        pl.BlockSpec((1, tq, GROUP_W), lambda bi, gi, qi, r: (bi, qi, kq + gi)),
        pl.BlockSpec((1, tk, HEAD_DIM), lambda bi, gi, qi, r: (bi, ktile(qi, r), ck + gi)),
        pl.BlockSpec((1, tk, HEAD_DIM), lambda bi, gi, qi, r: (bi, ktile(qi, r), cv + gi)),
        pl.BlockSpec((NSA_GROUP, 1, tq, tk), lambda bi, gi, qi, r: (gi, btile(qi, r), 0, 0)),
    ]
    args = [proj3, proj3, proj3, bias_tiles]
    if selmask is not None:
        in_specs.append(pl.BlockSpec((1, 1, tq, tk), lambda bi, gi, qi, r: (bi, gi, qi, ktile(qi, r))))
        args.append(selmask)
    kern = functools.partial(_nsa_band_kernel, nr=nr, ratio=ratio, mode=mode, has_sel=selmask is not None)
    rows = NSA_GROUP * tq
    return pl.pallas_call(
        kern,
        out_shape=jax.ShapeDtypeStruct((b, t, NSA_W), F32),
        grid=(b, g, nq, nr),
        in_specs=in_specs,
        out_specs=pl.BlockSpec((1, tq, GROUP_W), lambda bi, gi, qi, r: (bi, qi, gi)),
        scratch_shapes=[pltpu.VMEM((rows, HEAD_DIM), BF16), pltpu.VMEM((rows, 128), F32),
                        pltpu.VMEM((rows, 128), F32), pltpu.VMEM((rows, HEAD_DIM), F32)],
        compiler_params=_params("parallel", "parallel", "parallel", "arbitrary"),
        name="nsa_band_" + mode,
    )(*args)


_T5_THRESHOLDS = tuple(int(np.searchsorted(_bucket_np(np.arange(2 * MAX_DISTANCE)), k)) for k in range(1, N_BUCKETS))


def _t5_bias(rel_bias, dist):
    per_head = lambda k: rel_bias[k].reshape((-1,) + (1,) * dist.ndim)
    d = dist[None]
    out = jnp.broadcast_to(per_head(0), (rel_bias.shape[1],) + dist.shape)
    for k, thr in enumerate(_T5_THRESHOLDS, start=1):
        out = jnp.where(d >= thr, per_head(k), out)
    return out


def _band_tiles(rel_bias, n_tiles, tk, window=None):
    o = jnp.arange(n_tiles, dtype=jnp.int32)[:, None, None]
    i = jnp.arange(NSA_TQ, dtype=jnp.int32)[None, :, None]
    j = jnp.arange(tk, dtype=jnp.int32)[None, None, :]
    d = o * NSA_TQ + i - j
    ok = d >= 0
    if window is not None:
        ok = ok & (d < window)
    return jnp.where(ok[None], _t5_bias(rel_bias, d), NEG_INF)


def _combine_kernel(small_ref, oc_ref, os_ref, ow_ref, o_ref):
    gates = _sigmoid(small_ref[...])
    for h in range(NSA_HEADS):
        sl = slice(h * HEAD_DIM, (h + 1) * HEAD_DIM)
        base = FOX_HEADS + h
        g0 = gates[:, base:base + 1]
        g1 = gates[:, base + NSA_HEADS:base + NSA_HEADS + 1]
        g2 = gates[:, base + 2 * NSA_HEADS:base + 2 * NSA_HEADS + 1]
        o = g0 * oc_ref[:, sl] + g1 * os_ref[:, sl] + g2 * ow_ref[:, sl]
        o_ref[:, sl] = o.astype(o_ref.dtype)


def nsa_combine(proj, o_cmp, o_slc, o_win):
    m = proj.shape[0]
    tm = min(m, 256)
    wide = pl.BlockSpec((tm, NSA_W), lambda i: (i, 0))
    return pl.pallas_call(
        _combine_kernel,
        out_shape=jax.ShapeDtypeStruct((m, NSA_W), BF16),
        grid=(m // tm,),
        in_specs=[pl.BlockSpec((tm, 128), lambda i: (i, COL_SMALL // 128)), wide, wide, wide],
        out_specs=wide,
        compiler_params=_params("parallel"),
        name="nsa_combine",
    )(proj, o_cmp, o_slc, o_win)


def _lru_kernel(xr_ref, gate_ref, cw_ref, cb_ref, wa_ref, wx_ref, ba_ref, bx_ref, lam_ref, buf_ref, h0_ref,
                y_ref, hlast_ref, nbuf_ref, xbuf, hcar, *, nt, tt, last_row):
    ti = pl.program_id(2)
    kw = LRU_CONV

    @pl.when(ti == 0)
    def _():
        xbuf[8 - (kw - 1):8, :] = buf_ref[0]
        hcar[...] = h0_ref[0]

    x = xr_ref[0]
    xbuf[8:8 + tt, :] = x
    xc = x * cw_ref[kw - 1:kw, :] + cb_ref[...]
    for t in range(kw - 1):
        off = 8 - (kw - 1) + t
        xc = xc + xbuf[off:off + tt, :] * cw_ref[t:t + 1, :]
    xcb = xc.astype(BF16)
    r = _sigmoid(_dot(xcb, wa_ref[0]) + ba_ref[...])
    ig = _sigmoid(_dot(xcb, wx_ref[0]) + bx_ref[...])
    neg_lam = -lam_ref[...]
    softplus = jnp.maximum(neg_lam, 0.0) + jnp.log1p(jnp.exp(-jnp.abs(neg_lam)))
    log_a = -LRU_C * r * softplus
    a = jnp.exp(log_a)
    th = jnp.tanh(log_a)
    u = jnp.sqrt(-2.0 * th / (1.0 - th)) * (ig * xc)
    row = lax.broadcasted_iota(jnp.int32, a.shape, 0)
    step = 1
    while step < tt:
        keep = row >= step
        a_sh = jnp.where(keep, pltpu.roll(a, step, 0), 1.0)
        u_sh = jnp.where(keep, pltpu.roll(u, step, 0), 0.0)
        u = a * u_sh + u
        a = a * a_sh
        step *= 2
    hs = a * hcar[...] + u
    y_ref[0] = (_gelu(gate_ref[0]) * hs).astype(y_ref.dtype)
    hcar[...] = hs[tt - 1:tt, :]

    @pl.when(ti == nt - 1)
    def _():
        hlast_ref[0] = hs[last_row:last_row + 1, :]
        nbuf_ref[0] = xbuf[8 + last_row - (kw - 2):8 + last_row + 1, :]

    xbuf[8 - (kw - 1):8, :] = xbuf[8 + tt - (kw - 1):8 + tt, :]


def lru_block(proj1, conv_w, conv_b, w_a, b_a, w_x, b_x, lam, conv_buf, h0, t_valid):
    b, tp, w2 = proj1.shape
    w = w2 // 2
    nh = LRU_HEADS
    bw = w // nh
    tt = min(tp, 256)
    assert tp % tt == 0 and tt % 8 == 0
    nt = tp // tt
    last_row = (t_valid - 1) - (nt - 1) * tt
    assert 0 <= last_row < tt and (nt > 1 or last_row >= 0)
    kw = LRU_CONV
    vec = lambda a: a.reshape(1, w)
    chan = pl.BlockSpec((1, bw), lambda bi, hi, ti: (0, hi))
    kern = functools.partial(_lru_kernel, nt=nt, tt=tt, last_row=last_row)
    return pl.pallas_call(
        kern,
        out_shape=(jax.ShapeDtypeStruct((b, tp, w), BF16), jax.ShapeDtypeStruct((b, 1, w), F32),
                   jax.ShapeDtypeStruct((b, kw - 1, w), F32)),
        grid=(b, nh, nt),
        in_specs=[
            pl.BlockSpec((1, tt, bw), lambda bi, hi, ti: (bi, ti, nh + hi)),
            pl.BlockSpec((1, tt, bw), lambda bi, hi, ti: (bi, ti, hi)),
            pl.BlockSpec((kw, bw), lambda bi, hi, ti: (0, hi)),
            chan,
            pl.BlockSpec((1, bw, bw), lambda bi, hi, ti: (hi, 0, 0)),
            pl.BlockSpec((1, bw, bw), lambda bi, hi, ti: (hi, 0, 0)),
            chan, chan, chan,
            pl.BlockSpec((1, kw - 1, bw), lambda bi, hi, ti: (bi, 0, hi)),
            pl.BlockSpec((1, 1, bw), lambda bi, hi, ti: (bi, 0, hi)),
        ],
        out_specs=(
            pl.BlockSpec((1, tt, bw), lambda bi, hi, ti: (bi, ti, hi)),
            pl.BlockSpec((1, 1, bw), lambda bi, hi, ti: (bi, 0, hi)),
            pl.BlockSpec((1, kw - 1, bw), lambda bi, hi, ti: (bi, 0, hi)),
        ),
        scratch_shapes=[pltpu.VMEM((8 + tt, bw), F32), pltpu.VMEM((1, bw), F32)],
        compiler_params=_params("parallel", "parallel", "arbitrary"),
        name="rglru",
    )(proj1, proj1, conv_w, vec(conv_b), w_a.astype(BF16), w_x.astype(BF16), vec(b_a), vec(b_x), vec(lam),
      conv_buf, h0.reshape(b, 1, w))


GATHER_PAGES = 8


def _gather_kernel(pt_ref, *refs, n_pools, pg):
    ins, outs = refs[:n_pools * pg], refs[n_pools * pg:]
    for c in range(n_pools):
        for i in range(pg):
            src = ins[c * pg + i]
            rows = slice(i * PAGE_SIZE, (i + 1) * PAGE_SIZE)
            if len(src.shape) == 4:
                for g in range(src.shape[2]):
                    outs[c][0, g, rows, :] = src[0, :, g, :]
            else:
                outs[c][0, rows, :] = src[0]


def gather_pools(pools, page_table, extra_rows):
    b, n_pages = page_table.shape
    pg = math.gcd(GATHER_PAGES, n_pages)
    p = n_pages * PAGE_SIZE
    in_specs, args, out_shape, out_specs = [], [], [], []
    for pool in pools:
        tail = pool.shape[2:]
        zeros = (0,) * (1 + len(tail))
        for i in range(pg):
            in_specs.append(pl.BlockSpec((1, PAGE_SIZE) + tail, lambda bi, j, pt, i=i, z=zeros: (pt[bi, j * pg + i],) + z))
            args.append(pool)
        if len(tail) == 2:
            out_shape.append(jax.ShapeDtypeStruct((b, tail[0], p + extra_rows, tail[1]), pool.dtype))
            out_specs.append(pl.BlockSpec((1, tail[0], pg * PAGE_SIZE, tail[1]), lambda bi, j, pt: (bi, 0, j, 0)))
        else:
            out_shape.append(jax.ShapeDtypeStruct((b, p + extra_rows, tail[0]), pool.dtype))
            out_specs.append(pl.BlockSpec((1, pg * PAGE_SIZE, tail[0]), lambda bi, j, pt: (bi, j, 0)))
    out_shape, out_specs = tuple(out_shape), tuple(out_specs)
    return pl.pallas_call(
        functools.partial(_gather_kernel, n_pools=len(pools), pg=pg),
        out_shape=out_shape,
        grid_spec=pltpu.PrefetchScalarGridSpec(
            num_scalar_prefetch=1, grid=(b, n_pages // pg), in_specs=in_specs, out_specs=out_specs),
        compiler_params=_params("parallel", "arbitrary"),
        name="gather_pages",
    )(page_table, *args)


FOX_DECODE_PAGES = 4


def _fox_decode_kernel(pt_ref, q_ref, cq_ref, ck_ref, cknew_ref, *refs, n_steps, pf):
    kp, vp = refs[:pf], refs[pf:2 * pf]
    knew_ref, vnew_ref, o_ref, m_sc, l_sc, acc_sc, cq_sc = refs[2 * pf:]
    j = pl.program_id(1)
    rows = q_ref.shape[1]
    nh = FOX_HEADS
    w = PAGE_SIZE * nh

    @pl.when(j == 0)
    def _():
        m_sc[...] = jnp.full_like(m_sc, NEG_INF)
        l_sc[...] = jnp.zeros_like(l_sc)
        acc_sc[...] = jnp.zeros_like(acc_sc)
        col = lax.broadcasted_iota(jnp.int32, (rows, w), 1)
        row = lax.broadcasted_iota(jnp.int32, (rows, w), 0)
        cq_sc[...] = jnp.where((col % nh) == (row % nh), cq_ref[0], NEG_INF)

    def update(k3, v3, ck_row, causal):
        n = k3.shape[0] * nh
        k2 = k3.reshape(n, HEAD_DIM).astype(BF16)
        v2 = v3.reshape(n, HEAD_DIM).astype(BF16)
        s = _dot_nt(q_ref[0], k2) * SCALE + (cq_sc[:, :n] - ck_row)
        if causal:
            col = lax.broadcasted_iota(jnp.int32, (rows, n), 1)
            row = lax.broadcasted_iota(jnp.int32, (rows, n), 0)
            s = jnp.where(col // nh <= row // nh, s, NEG_INF)
        _online_softmax_step(s, v2, m_sc, l_sc, acc_sc)

    @pl.when(j < n_steps)
    def _():
        for i in range(pf):
            update(kp[i][0], vp[i][0], ck_ref[0, :, i * w:(i + 1) * w], False)

    @pl.when(j == n_steps)
    def _():
        update(knew_ref[0], vnew_ref[0], cknew_ref[0], True)
        o_ref[0] = acc_sc[...] / jnp.maximum(l_sc[...], 1e-30)


def fox_decode(q_rows, cq_rows, ck_past, ck_new, k_pool, v_pool, k_new, v_new, page_table):
    b, n_pages = page_table.shape
    pf = math.gcd(FOX_DECODE_PAGES, n_pages)
    n_steps = n_pages // pf
    rows = q_rows.shape[1]
    tp = k_new.shape[1]
    nh = FOX_HEADS

    def page(i):
        return lambda bi, j, pt: (pt[bi, jnp.minimum(j * pf + i, n_pages - 1)], 0, 0, 0)

    pool_spec = [pl.BlockSpec((1, PAGE_SIZE, nh, HEAD_DIM), page(i)) for i in range(pf)]
    per_seq = lambda shape: pl.BlockSpec(shape, lambda bi, j, pt: (bi,) + (0,) * (len(shape) - 1))
    kern = functools.partial(_fox_decode_kernel, n_steps=n_steps, pf=pf)
    return pl.pallas_call(
        kern,
        out_shape=jax.ShapeDtypeStruct((b, rows, HEAD_DIM), F32),
        grid_spec=pltpu.PrefetchScalarGridSpec(
            num_scalar_prefetch=1,
            grid=(b, n_steps + 1),
            in_specs=[per_seq((1, rows, HEAD_DIM)), per_seq((1, rows, 1)),
                      pl.BlockSpec((1, 1, pf * PAGE_SIZE * nh), lambda bi, j, pt: (bi, 0, jnp.minimum(j, n_steps - 1))),
                      per_seq((1, 1, tp * nh))]
                     + pool_spec + pool_spec
                     + [per_seq((1, tp, nh, HEAD_DIM)), per_seq((1, tp, nh, HEAD_DIM))],
            out_specs=per_seq((1, rows, HEAD_DIM)),
            scratch_shapes=[pltpu.VMEM((rows, 128), F32), pltpu.VMEM((rows, 128), F32),
                            pltpu.VMEM((rows, HEAD_DIM), F32), pltpu.VMEM((rows, PAGE_SIZE * nh), F32)],
        ),
        compiler_params=_params("parallel", "arbitrary"),
        name="fox_decode",
    )(page_table, q_rows, cq_rows, ck_past, ck_new, *([k_pool] * pf), *([v_pool] * pf), k_new, v_new)


def _cmp_small_kernel(q_ref, k_ref, v_ref, bias_ref, o_ref, imp_ref, *, t_new):
    s = _dot_nt(q_ref[0, 0], k_ref[0, 0].astype(BF16)) * SCALE + bias_ref[0]
    mask = s > VISIBLE_MIN
    m = jnp.max(s, axis=-1, keepdims=True)
    e = jnp.where(mask, jnp.exp(s - m), 0.0)
    p = e / jnp.maximum(jnp.sum(e, axis=-1, keepdims=True), 1e-30)
    o_ref[0, 0] = _dot(p.astype(BF16), v_ref[0, 0].astype(BF16))
    imp_ref[0, 0] = jnp.sum(p.reshape(t_new, NSA_GROUP, p.shape[-1]), axis=1)


def cmp_small(q_rows, ck, cv, bias, t_new):
    b, g, rows, d = q_rows.shape
    s = ck.shape[2]
    return pl.pallas_call(
        functools.partial(_cmp_small_kernel, t_new=t_new),
        out_shape=(jax.ShapeDtypeStruct((b, g, rows, d), F32), jax.ShapeDtypeStruct((b, g, t_new, s), F32)),
        grid=(b, g),
        in_specs=[
            pl.BlockSpec((1, 1, rows, d), lambda bi, gi: (bi, gi, 0, 0)),
            pl.BlockSpec((1, 1, s, d), lambda bi, gi: (bi, gi, 0, 0)),
            pl.BlockSpec((1, 1, s, d), lambda bi, gi: (bi, gi, 0, 0)),
            pl.BlockSpec((1, rows, s), lambda bi, gi: (gi, 0, 0)),
        ],
        out_specs=(pl.BlockSpec((1, 1, rows, d), lambda bi, gi: (bi, gi, 0, 0)),
                   pl.BlockSpec((1, 1, t_new, s), lambda bi, gi: (bi, gi, 0, 0))),
        compiler_params=_params("parallel", "parallel"),
        name="nsa_cmp_sample",
    )(q_rows, ck, cv, bias)


def _flash_small_kernel(q_ref, k_ref, v_ref, bias_ref, o_ref, m_sc, l_sc, acc_sc, *, nk):
    j = pl.program_id(2)

    @pl.when(j == 0)
    def _():
        m_sc[...] = jnp.full_like(m_sc, NEG_INF)
        l_sc[...] = jnp.zeros_like(l_sc)
        acc_sc[...] = jnp.zeros_like(acc_sc)

    s = _dot_nt(q_ref[0, 0], k_ref[0, 0].astype(BF16)) * SCALE + bias_ref[0, 0]
    vis = s > VISIBLE_MIN
    m_prev = m_sc[...]
    m_new = jnp.maximum(m_prev, jnp.max(s, axis=-1, keepdims=True))
    alpha = jnp.exp(m_prev - m_new)
    p = jnp.where(vis, jnp.exp(s - m_new), 0.0)
    l_sc[...] = alpha * l_sc[...] + jnp.sum(p, axis=-1, keepdims=True)
    acc_sc[...] = alpha * acc_sc[...] + _dot(p.astype(BF16), v_ref[0, 0].astype(BF16))
    m_sc[...] = m_new

    @pl.when(j == nk - 1)
    def _():
        o_ref[0, 0] = acc_sc[...] / jnp.maximum(l_sc[...], 1e-30)


def flash_small(q_rows, k, v, bias, tk):
    b, g, rows, d = q_rows.shape
    s = k.shape[2]
    assert s % tk == 0
    nk = s // tk
    return pl.pallas_call(
        functools.partial(_flash_small_kernel, nk=nk),
        out_shape=jax.ShapeDtypeStruct((b, g, rows, d), F32),
        grid=(b, g, nk),
        in_specs=[
            pl.BlockSpec((1, 1, rows, d), lambda bi, gi, j: (bi, gi, 0, 0)),
            pl.BlockSpec((1, 1, tk, d), lambda bi, gi, j: (bi, gi, j, 0)),
            pl.BlockSpec((1, 1, tk, d), lambda bi, gi, j: (bi, gi, j, 0)),
            pl.BlockSpec((1, 1, rows, tk), lambda bi, gi, j: (bi, gi, 0, j)),
        ],
        out_specs=pl.BlockSpec((1, 1, rows, d), lambda bi, gi, j: (bi, gi, 0, 0)),
        scratch_shapes=[pltpu.VMEM((rows, 1), F32), pltpu.VMEM((rows, 1), F32), pltpu.VMEM((rows, d), F32)],
        compiler_params=_params("parallel", "parallel", "arbitrary"),
        name="nsa_flash_sample",
    )(q_rows, k, v, bias)


def _topk_kernel(col_ref, row_ref, o_ref, *, n_top):
    a = col_ref[...]
    bb = row_ref[...]
    shape = (a.shape[0], a.shape[1], bb.shape[2])
    j_idx = lax.broadcasted_iota(jnp.int32, shape, 1)
    i_idx = lax.broadcasted_iota(jnp.int32, shape, 2)
    ahead = (a > bb) | ((a == bb) & (j_idx < i_idx))
    rank = jnp.sum(jnp.where(ahead, 1.0, 0.0), axis=1, keepdims=True)
    o_ref[...] = jnp.where(rank < n_top, 1.0, 0.0)


def topk_mask(score, n_top):
    r, ns = score.shape
    tr = 8 if r % 8 == 0 else r
    out = pl.pallas_call(
        functools.partial(_topk_kernel, n_top=n_top),
        out_shape=jax.ShapeDtypeStruct((r, 1, ns), F32),
        grid=(r // tr,),
        in_specs=[pl.BlockSpec((tr, ns, 1), lambda i: (i, 0, 0)), pl.BlockSpec((tr, 1, ns), lambda i: (i, 0, 0))],
        out_specs=pl.BlockSpec((tr, 1, ns), lambda i: (i, 0, 0)),
        compiler_params=_params("parallel"),
        name="nsa_topk",
    )(score.reshape(r, ns, 1), score.reshape(r, 1, ns))
    return out.reshape(r, ns)


def _pack_w_in0(w_in0):
    d = w_in0.shape[0]
    off_fz = 3 * FOX_W
    off_nq = off_fz + FOX_HEADS
    off_kv = off_nq + NSA_W
    off_ng = off_kv + 6 * NSA_KV_W
    parts = [w_in0[:, :off_fz], w_in0[:, off_nq:off_ng], w_in0[:, off_fz:off_nq], w_in0[:, off_ng:]]
    used = sum(p.shape[1] for p in parts)
    parts.append(jnp.zeros((d, L0_PACKED - used), w_in0.dtype))
    return jnp.concatenate(parts, axis=1).astype(BF16)


def _forget_terms(proj3, b_forget):
    fz = proj3[:, :, COL_SMALL:COL_SMALL + FOX_HEADS]
    lf = jax.nn.log_sigmoid(fz + b_forget.astype(F32))
    return lf, jnp.cumsum(lf, axis=1)


def _kv_cols(proj3, idx):
    off = COL_KV + idx * NSA_KV_W
    return proj3[:, :, off:off + NSA_KV_W]


def attn_prompt(h_bf16, b, t, wts):
    m = b * t
    g = NSA_KV_HEADS
    proj = matmul(h_bf16, wts["w_in0"])
    proj3 = proj.reshape(b, t, L0_PACKED)
    lf, c_new = _forget_terms(proj3, wts["b_forget"])
    o_fox = fox_prompt(proj3, c_new)

    kc, vc, ks, vs, kw, vw = [_kv_cols(proj3, i) for i in range(6)]
    nc, ns = t // CMP_BLOCK, t // SEL_BLOCK
    tk = min(NSA_TK, t)
    assert t % (2 * CMP_BLOCK) == 0 and t % tk == 0
    ck = compress(proj, COL_KV // HEAD_DIM, g, wts["cmp_k"]).reshape(g, b, nc, HEAD_DIM)
    cv = compress(proj, (COL_KV + NSA_KV_W) // HEAD_DIM, g, wts["cmp_v"]).reshape(g, b, nc, HEAD_DIM)
    even_odd = lambda a: jnp.concatenate([a[:, :, 0::2], a[:, :, 1::2]], axis=2)
    rel = wts["rel_bias"].astype(F32)
    blk_c = jnp.concatenate([jnp.arange(0, nc, 2, dtype=jnp.int32), jnp.arange(1, nc, 2, dtype=jnp.int32)])
    c_dist = jnp.arange(t, dtype=jnp.int32)[:, None] - ((blk_c + 1) * CMP_BLOCK - 1)[None, :]
    bias_cmp = _t5_bias(rel, c_dist)
    o_cmp, selmask = cmp_prompt(proj3, even_odd(ck), even_odd(cv), bias_cmp)

    n_near = -(-(_T5_THRESHOLDS[-1] + tk - 1) // NSA_TQ)
    tiles_slc = _band_tiles(rel, n_near + 1, tk)
    o_slc = nsa_band(proj3, tiles_slc, COL_KV + 2 * NSA_KV_W, COL_KV + 3 * NSA_KV_W, "causal", selmask)
    n_win = (tk // NSA_TQ) * (-(-WINDOW // tk) + 1)
    tiles_win = _band_tiles(rel, n_win, tk, window=WINDOW)
    o_win = nsa_band(proj3, tiles_win, COL_KV + 4 * NSA_KV_W, COL_KV + 5 * NSA_KV_W, "window")

    o_nsa = nsa_combine(proj, o_cmp.reshape(m, NSA_W), o_slc.reshape(m, NSA_W), o_win.reshape(m, NSA_W))
    mix = (o_fox.reshape(m, FOX_W), o_nsa)
    win_len = wts["win_len"]
    zeros = jnp.zeros((b, WINDOW, NSA_KV_W), F32)
    kw_a = jnp.concatenate([zeros, kw], axis=1)
    vw_a = jnp.concatenate([zeros, vw], axis=1)
    fk = proj3[:, :, COL_FK:COL_FK + FOX_W]
    fv = proj3[:, :, COL_FV:COL_FV + FOX_W]
    new = (fk, fv, lf, kc, vc, ks, vs, kw_a[:, kw_a.shape[1] - win_len:], vw_a[:, vw_a.shape[1] - win_len:])
    return mix, new


def attn_sample(h_bf16, b, t, wts, caches, page_table, state_win_k, state_win_v):
    m = b * t
    g, hg = NSA_KV_HEADS, NSA_GROUP
    n_pages = page_table.shape[1]
    p = n_pages * PAGE_SIZE
    assert t <= PAGE_SIZE and p % SEL_BLOCK == 0
    proj = matmul(h_bf16, wts["w_in0"])
    proj3 = proj.reshape(b, t, L0_PACKED)
    lf, c_new = _forget_terms(proj3, wts["b_forget"])
    kc, vc, ks, vs, kw, vw = [_kv_cols(proj3, i) for i in range(6)]
    fq = proj3[:, :, COL_FQ:COL_FQ + FOX_W]
    fk = proj3[:, :, COL_FK:COL_FK + FOX_W]
    fv = proj3[:, :, COL_FV:COL_FV + FOX_W]
    q_pos = p + np.arange(t)

    cache_fox_k, cache_fox_v, cache_fox_lf, cache_cmp_k, cache_cmp_v, cache_slc_k, cache_slc_v = caches
    tk_slc = min(2048, p)
    pcmp_k, pcmp_v, plf = gather_pools([cache_cmp_k, cache_cmp_v, cache_fox_lf], page_table, 0)
    pslc_k, pslc_v = gather_pools([cache_slc_k, cache_slc_v], page_table, tk_slc)

    c_past = plf - lax.cumsum(plf, axis=1, reverse=True)
    tp = -(-t // 8) * 8
    pad_new = lambda a: jnp.concatenate([a, jnp.zeros((b, tp - t) + a.shape[2:], F32)], axis=1)
    o_fox = fox_decode(fq.reshape(b, t * FOX_HEADS, HEAD_DIM).astype(BF16), c_new.reshape(b, t * FOX_HEADS, 1),
                       c_past.reshape(b, 1, p * FOX_HEADS), pad_new(c_new).reshape(b, 1, tp * FOX_HEADS),
                       cache_fox_k, cache_fox_v,
                       pad_new(fk.reshape(b, t, FOX_HEADS, HEAD_DIM)), pad_new(fv.reshape(b, t, FOX_HEADS, HEAD_DIM)),
                       page_table).reshape(m, FOX_W)

    rel = wts["rel_bias"].astype(F32)
    nq = proj3[:, :, COL_NQ:COL_NQ + NSA_W].reshape(b, t, g, hg, HEAD_DIM)
    q_rows = nq.transpose(0, 2, 1, 3, 4).reshape(b, g, t * hg, HEAD_DIM).astype(BF16)

    def head_rows(x):
        tt, ss = x.shape[1], x.shape[2]
        return x.reshape(g, hg, tt, ss).transpose(0, 2, 1, 3).reshape(g, tt * hg, ss)

    l_tot = p + t
    l_pad = -(-l_tot // SEL_BLOCK) * SEL_BLOCK
    nc, ns = l_pad // CMP_BLOCK, l_pad // SEL_BLOCK
    nc_past = p // CMP_BLOCK
    assert nc_past % 2 == 0 and (nc_past + 1) * CMP_BLOCK - 1 > q_pos[-1]
    ck = compress(pcmp_k.reshape(b * g * p, HEAD_DIM), 0, 1, wts["cmp_k"]).reshape(b, g, nc_past, HEAD_DIM)
    cv = compress(pcmp_v.reshape(b * g * p, HEAD_DIM), 0, 1, wts["cmp_v"]).reshape(b, g, nc_past, HEAD_DIM)
    even_odd = lambda a: jnp.concatenate([a[:, :, 0::2], a[:, :, 1::2]], axis=2)
    blk_c = jnp.concatenate([jnp.arange(0, nc_past, 2, dtype=jnp.int32), jnp.arange(1, nc_past, 2, dtype=jnp.int32)])
    c_dist = jnp.asarray(q_pos, jnp.int32)[:, None] - ((blk_c + 1) * CMP_BLOCK - 1)[None, :]
    bias_cmp = jnp.where((c_dist >= 0)[None], _t5_bias(rel, c_dist), NEG_INF)
    o_cmp, imp = cmp_small(q_rows, even_odd(ck), even_odd(cv), head_rows(bias_cmp), t)
    ns_past = nc_past // 2
    imp_s = imp[..., :ns_past] + imp[..., ns_past:]
    imp_s = jnp.concatenate([imp_s, jnp.zeros((b, g, t, ns - ns_past), F32)], axis=-1)
    blk = np.arange(ns)[None, :]
    cur = (q_pos // SEL_BLOCK)[:, None]
    valid = blk * SEL_BLOCK <= q_pos[:, None]
    forced = (blk == 0) | (blk == cur) | (blk == cur - 1)
    score = jnp.where(jnp.asarray(forced), SEL_FORCE, jnp.where(jnp.asarray(valid), imp_s, -1.0))
    sel = topk_mask(score.reshape(b * g * t, ns), min(N_SEL, ns)).reshape(b, g, t, ns)

    s_slc = p + tk_slc
    qp = jnp.asarray(q_pos, jnp.int32)[:, None]
    key_pos = jnp.arange(s_slc, dtype=jnp.int32)[None, :]
    d_slc = qp - key_pos
    ok = (d_slc >= 0) & (key_pos < l_tot)
    bias_slc = head_rows(jnp.where(ok[None], _t5_bias(rel, d_slc), NEG_INF))
    n_full = p // SEL_BLOCK
    sel_keys = jnp.concatenate([jnp.repeat(sel[..., :n_full], SEL_BLOCK, axis=-1),
                                jnp.broadcast_to(sel[..., ns - 1:ns], (b, g, t, tk_slc))], axis=-1)
    assert ns - 1 == n_full
    sel_rows = jnp.broadcast_to(sel_keys[:, :, :, None, :], (b, g, t, hg, s_slc)).reshape(b, g, t * hg, s_slc)
    bias_slc = jnp.where(sel_rows > 0.5, bias_slc[None], NEG_INF)

    def head_major(a, rows):
        a = a.reshape(b, t, g, HEAD_DIM).transpose(0, 2, 1, 3)
        return jnp.concatenate([a, jnp.zeros((b, g, rows - t, HEAD_DIM), F32)], axis=2)

    pslc_k = lax.dynamic_update_slice(pslc_k, head_major(ks, tk_slc), (0, 0, p, 0))
    pslc_v = lax.dynamic_update_slice(pslc_v, head_major(vs, tk_slc), (0, 0, p, 0))
    o_slc = flash_small(q_rows, pslc_k, pslc_v, bias_slc, tk_slc)

    p_w = state_win_k.shape[1]
    s_win = -(-(p_w + t) // 128) * 128
    kw_a = jnp.concatenate([state_win_k.reshape(b, p_w, NSA_KV_W), kw], axis=1)
    vw_a = jnp.concatenate([state_win_v.reshape(b, p_w, NSA_KV_W), vw], axis=1)
    win_major = lambda st, new: jnp.concatenate([st.transpose(0, 2, 1, 3), head_major(new, s_win - p_w)], axis=2)
    w_idx = jnp.arange(s_win, dtype=jnp.int32)[None, :]
    wpos = (p - p_w) + w_idx
    wd = qp - wpos
    wok = (wd >= 0) & (wd < WINDOW) & (wpos >= 0) & (w_idx < p_w + t)
    bias_win = head_rows(jnp.where(wok[None], _t5_bias(rel, wd), NEG_INF))
    bias_win = jnp.broadcast_to(bias_win[None], (b,) + bias_win.shape)
    o_win = flash_small(q_rows, win_major(state_win_k, kw), win_major(state_win_v, vw), bias_win, s_win)

    def to_tokens(o):
        return o.reshape(b, g, t, hg, HEAD_DIM).transpose(0, 2, 1, 3, 4).reshape(m, NSA_W)

    o_nsa = nsa_combine(proj, to_tokens(o_cmp), to_tokens(o_slc), to_tokens(o_win))
    mix = (o_fox.astype(BF16), o_nsa)
    win_len = wts["win_len"]
    new = (fk, fv, lf, kc, vc, ks, vs, kw_a[:, kw_a.shape[1] - win_len:], vw_a[:, vw_a.shape[1] - win_len:])
    return mix, new


def trunk(x, p_emb, wts, lru_conv0, lru_h0, ffn_conv0, past=None):
    b, t, d = x.shape
    m = b * t
    x2 = x.reshape(m, d)
    ffn_bufs = []
    for i in range(2):
        h = rmsnorm(x2, wts["norm_mix"][i], BF16)
        if i == 0:
            if past is None:
                mix, attn_new = attn_prompt(h, b, t, wts)
            else:
                mix, attn_new = attn_sample(h, b, t, wts, *past)
            x2 = matmul_pair(mix[0], mix[1], wts["w_out0"], x2)
        else:
            proj1 = matmul(h, wts["w_in1"]).reshape(b, t, -1)
            tp = -(-t // 8) * 8
            if tp != t:
                proj1 = jnp.concatenate([proj1, jnp.zeros((b, tp - t, proj1.shape[-1]), F32)], axis=1)
            gated, h_last, lru_buf = lru_block(proj1, wts["lru_conv_w"], wts["lru_conv_b"], wts["lru_w_a"],
                                               wts["lru_b_a"], wts["lru_w_x"], wts["lru_b_x"], wts["lru_lambda"],
                                               lru_conv0, lru_h0, t)
            x2 = matmul(gated[:, :t].reshape(m, -1), wts["w_out1"], res=x2)
        hf = rmsnorm(x2, wts["norm_ffn"][i], BF16)
        dff = wts["ffn_w_up"][i].shape[1] // 2
        if t % 8 == 0:
            act, buf = ffn_up_fused(hf, wts["ffn_w_up"][i], wts["ffn_conv_w"][i], wts["ffn_conv_b"][i],
                                    ffn_conv0[i], t)
        else:
            gu = matmul(hf, wts["ffn_w_up"][i])
            xp = jnp.concatenate([ffn_conv0[i], gu[:, :dff].reshape(b, t, dff)], axis=1)
            taps = [xp[:, FFN_CONV - 1 - s:FFN_CONV - 1 - s + t].reshape(m, dff) for s in range(FFN_CONV)]
            act = convgate(taps[0], taps[1], taps[2], gu[:, dff:], wts["ffn_conv_w"][i], wts["ffn_conv_b"][i])
            buf = xp[:, xp.shape[1] - (FFN_CONV - 1):]
        ffn_bufs.append(buf)
        x2 = matmul(act, wts["ffn_w_down"][i], res=x2)
        hp = rmsnorm(x2, wts["ple_norm"][i], BF16)
        emb = matmul(p_emb[i].reshape(m, -1).astype(BF16), wts["ple_w_proj"][i])
        x2 = matmul(hp, wts["ple_w_gate"][i], res=x2, aux=emb)
    y = rmsnorm(x2, wts["final_norm"], F32).reshape(b, t, d)
    return y, attn_new, lru_buf, h_last.reshape(b, -1), jnp.stack(ffn_bufs)


def kernel(x_prompt, x_sample, cache_fox_k, cache_fox_v, cache_fox_lf, cache_cmp_k, cache_cmp_v, cache_slc_k, cache_slc_v, state_win_k, state_win_v, state_lru_conv, state_lru_h, state_ffn_conv, page_table, p_prompt, p_sample, norm_mix, norm_ffn, final_norm, w_in0, b_forget, cmp_w1_k, cmp_pe_k, cmp_w2_k, cmp_w1_v, cmp_pe_v, cmp_w2_v, rel_bias, w_out0, w_in1, lru_conv_w, lru_conv_b, lru_w_a, lru_b_a, lru_w_x, lru_b_x, lru_lambda, w_out1, ffn_w_up, ffn_conv_w, ffn_conv_b, ffn_w_down, ple_w_proj, ple_w_gate, ple_norm):
    depth = norm_mix.shape[0]
    assert depth == 2
    d = x_prompt.shape[-1]
    dff = ffn_w_down.shape[1]
    wts = {
        "norm_mix": norm_mix, "norm_ffn": norm_ffn, "ple_norm": ple_norm, "final_norm": final_norm,
        "w_in0": _pack_w_in0(w_in0), "b_forget": b_forget, "rel_bias": rel_bias,
        "win_len": state_win_k.shape[1],
        "w_out0": w_out0.astype(BF16), "w_in1": w_in1.astype(BF16), "w_out1": w_out1.astype(BF16),
        "lru_conv_w": lru_conv_w, "lru_conv_b": lru_conv_b, "lru_w_a": lru_w_a, "lru_b_a": lru_b_a,
        "lru_w_x": lru_w_x, "lru_b_x": lru_b_x, "lru_lambda": lru_lambda,
        "ffn_w_up": ffn_w_up.astype(BF16), "ffn_conv_w": ffn_conv_w, "ffn_conv_b": ffn_conv_b,
        "ffn_w_down": ffn_w_down.astype(BF16), "ple_w_proj": ple_w_proj.astype(BF16),
        "ple_w_gate": ple_w_gate.astype(BF16),
    }
    wts["cmp_k"] = (cmp_w1_k, cmp_pe_k, cmp_w2_k)
    wts["cmp_v"] = (cmp_w1_v, cmp_pe_v, cmp_w2_v)

    bp = x_prompt.shape[0]
    y_p, attn_p, lru_conv_p, lru_h_p, ffn_conv_p = trunk(
        x_prompt, p_prompt, wts,
        jnp.zeros((bp, LRU_CONV - 1, d), F32), jnp.zeros((bp, d), F32),
        jnp.zeros((depth, bp, FFN_CONV - 1, dff), F32))
    caches = (cache_fox_k, cache_fox_v, cache_fox_lf, cache_cmp_k, cache_cmp_v, cache_slc_k, cache_slc_v)
    y_s, attn_s, lru_conv_s, lru_h_s, ffn_conv_s = trunk(
        x_sample, p_sample, wts, state_lru_conv, state_lru_h, state_ffn_conv,
        past=(caches, page_table, state_win_k, state_win_v))

    def shape_attn(new, b, t):
        fk, fv, lf, kc, vc, ks, vs, wk, wv = new
        h4 = lambda a: a.reshape(b, a.shape[1], FOX_HEADS, HEAD_DIM)
        g4 = lambda a: a.reshape(b, a.shape[1], NSA_KV_HEADS, HEAD_DIM)
        return h4(fk), h4(fv), lf, g4(kc), g4(vc), g4(ks), g4(vs), g4(wk), g4(wv)

    ap = shape_attn(attn_p, bp, x_prompt.shape[1])
    asmp = shape_attn(attn_s, x_sample.shape[0], x_sample.shape[1])
    out = [y_p, y_s]
    for a, s in zip(ap, asmp):
        out += [a, s]
    out += [lru_conv_p, lru_conv_s, lru_h_p, lru_h_s, ffn_conv_p, ffn_conv_s]
    return tuple(out)
```

```python
import functools
import math

import numpy as np
import jax
import jax.numpy as jnp
from jax import lax
from jax.experimental import pallas as pl
from jax.experimental.pallas import tpu as pltpu

PAGE_SIZE = 128
HEAD_DIM = 128
FOX_HEADS = 16
NSA_HEADS = 16
NSA_KV_HEADS = 2
NSA_GROUP = NSA_HEADS // NSA_KV_HEADS
CMP_BLOCK = 32
SEL_BLOCK = 64
N_SEL = 16
WINDOW = 512
SEL_FORCE = 1000.0
N_BUCKETS = 32
MAX_DISTANCE = 1024
LRU_HEADS = 16
LRU_C = 8.0
LRU_CONV = 4
FFN_CONV = 3
EPS = 1e-6
NEG_INF = -1e30
VISIBLE_MIN = -5e29

FOX_W = FOX_HEADS * HEAD_DIM
NSA_W = NSA_HEADS * HEAD_DIM
NSA_KV_W = NSA_KV_HEADS * HEAD_DIM
GROUP_W = NSA_GROUP * HEAD_DIM
SCALE = HEAD_DIM ** -0.5

COL_FQ, COL_FK, COL_FV = 0, FOX_W, 2 * FOX_W
COL_NQ = 3 * FOX_W
COL_KV = COL_NQ + NSA_W
COL_SMALL = COL_KV + 6 * NSA_KV_W
L0_PACKED = -(-(COL_SMALL + 128) // 1024) * 1024

F32 = jnp.float32
BF16 = jnp.bfloat16

VMEM_LIMIT_BYTES = 56 * 1024 * 1024


def _params(*sem):
    return pltpu.CompilerParams(dimension_semantics=sem, vmem_limit_bytes=VMEM_LIMIT_BYTES)


def _dot(a, b):
    return jnp.dot(a, b, preferred_element_type=F32)


def _dot_nt(a, b):
    return lax.dot_general(a, b, (((1,), (1,)), ((), ())), preferred_element_type=F32)


def _gelu(x):
    c = math.sqrt(2.0 / math.pi)
    return 0.5 * x * (1.0 + jnp.tanh(c * (x + 0.044715 * (x * x * x))))


def _sigmoid(x):
    return 1.0 / (1.0 + jnp.exp(-x))


def _online_softmax_step(s, v, m_sc, l_sc, acc_sc):
    reps = s.shape[1] // 128
    m_prev = m_sc[...]
    m_new = jnp.maximum(m_prev, jnp.max(s, axis=-1, keepdims=True))
    alpha = jnp.exp(m_prev - m_new)
    m_wide = m_new if reps == 1 else jnp.concatenate([m_new] * reps, axis=1)
    p = jnp.exp(s - m_wide)
    l_sc[...] = alpha * l_sc[...] + jnp.sum(p, axis=-1, keepdims=True)
    acc_sc[...] = alpha * acc_sc[...] + _dot(p.astype(BF16), v)
    m_sc[...] = m_new


def _bucket_np(dist):
    n = np.maximum(np.asarray(dist, np.int64), 0)
    exact = N_BUCKETS // 2
    nf = np.maximum(n, 1).astype(np.float64)
    large = exact + (np.log(nf / exact) / math.log(MAX_DISTANCE / exact) * (N_BUCKETS - exact)).astype(np.int64)
    return np.where(n < exact, n, np.minimum(large, N_BUCKETS - 1)).astype(np.int32)


def _rmsnorm_kernel(x_ref, g_ref, o_ref):
    x = x_ref[...]
    y = x * lax.rsqrt(jnp.mean(x * x, axis=-1, keepdims=True) + EPS)
    o_ref[...] = (y * g_ref[...]).astype(o_ref.dtype)


def rmsnorm(x, g, out_dtype):
    m, d = x.shape
    tm = min(m, 256)
    return pl.pallas_call(
        _rmsnorm_kernel,
        out_shape=jax.ShapeDtypeStruct((m, d), out_dtype),
        grid=(m // tm,),
        in_specs=[pl.BlockSpec((tm, d), lambda i: (i, 0)), pl.BlockSpec((1, d), lambda i: (0, 0))],
        out_specs=pl.BlockSpec((tm, d), lambda i: (i, 0)),
        compiler_params=_params("parallel"),
        name="rmsnorm",
    )(x, g.reshape(1, d))


def _mm_kernel(*refs, nk, epilogue, emit):
    a_ref, w_ref = refs[0], refs[1]
    n_in = 2 + (epilogue is not None) + (epilogue == "ple")
    o_ref = refs[n_in]
    k = pl.program_id(2)
    w = w_ref[...].astype(BF16)
    if emit:
        refs[n_in + 1][...] = w

    @pl.when(k == 0)
    def _():
        o_ref[...] = jnp.zeros_like(o_ref)

    o_ref[...] += _dot(a_ref[...], w)

    if epilogue is not None:
        @pl.when(k == nk - 1)
        def _():
            acc = o_ref[...]
            if epilogue == "res":
                o_ref[...] = refs[2][...] + acc
            else:
                o_ref[...] = refs[2][...] + _sigmoid(acc) * refs[3][...]


def matmul(a, w, *, layer=None, res=None, aux=None, tm=1024, tn=1024, tk=2048):
    m, kdim = a.shape
    assert w.shape[-2] == kdim and (layer is None) == (w.ndim == 2)
    n = w.shape[-1]
    tm, tn, tk = min(tm, m), min(tn, n), min(tk, kdim)
    assert m % tm == 0 and n % tn == 0 and kdim % tk == 0
    nk = kdim // tk
    emit = w.dtype != BF16
    assert not emit or m == tm
    epilogue = None if res is None else ("res" if aux is None else "ple")
    if layer is None:
        w_spec = pl.BlockSpec((tk, tn), lambda i, j, k: (k, j))
    else:
        w_spec = pl.BlockSpec((None, tk, tn), lambda i, j, k: (layer, k, j))
    in_specs = [pl.BlockSpec((tm, tk), lambda i, j, k: (i, k)), w_spec]
    args = [a, w]
    for extra in (res, aux):
        if extra is not None:
            in_specs.append(pl.BlockSpec((tm, tn), lambda i, j, k: (i, j)))
            args.append(extra)
    out_shape = jax.ShapeDtypeStruct((m, n), F32)
    out_specs = pl.BlockSpec((tm, tn), lambda i, j, k: (i, j))
    if emit:
        out_shape = (out_shape, jax.ShapeDtypeStruct((kdim, n), BF16))
        out_specs = (out_specs, pl.BlockSpec((tk, tn), lambda i, j, k: (k, j)))
    return pl.pallas_call(
        functools.partial(_mm_kernel, nk=nk, epilogue=epilogue, emit=emit),
        out_shape=out_shape,
        grid=(m // tm, n // tn, nk),
        in_specs=in_specs,
        out_specs=out_specs,
        compiler_params=_params("parallel", "parallel", "arbitrary"),
        name="matmul_" + (epilogue or "plain") + ("_cast" if emit else ""),
    )(*args)


def _mm_pair_kernel(a0_ref, a1_ref, w_ref, res_ref, o_ref, *wb_ref):
    k = pl.program_id(2)
    w = w_ref[...].astype(BF16)
    if wb_ref:
        wb_ref[0][...] = w

    @pl.when(k == 0)
    def _():
        o_ref[...] = res_ref[...] + _dot(a0_ref[...], w)

    @pl.when(k == 1)
    def _():
        o_ref[...] += _dot(a1_ref[...], w)


def matmul_pair(a0, a1, w, res, *, tm=1024, tn=1024):
    m, k0 = a0.shape
    assert a1.shape == (m, k0) and w.shape[0] == 2 * k0
    n = w.shape[1]
    tm, tn = min(tm, m), min(tn, n)
    assert m % tm == 0 and n % tn == 0
    emit = w.dtype != BF16
    assert not emit or m == tm
    piece = pl.BlockSpec((tm, k0), lambda i, j, k: (i, 0))
    tile = pl.BlockSpec((tm, tn), lambda i, j, k: (i, j))
    w_tile = pl.BlockSpec((k0, tn), lambda i, j, k: (k, j))
    out_shape, out_specs = jax.ShapeDtypeStruct((m, n), F32), tile
    if emit:
        out_shape, out_specs = (out_shape, jax.ShapeDtypeStruct(w.shape, BF16)), (tile, w_tile)
    return pl.pallas_call(
        _mm_pair_kernel,
        out_shape=out_shape,
        grid=(m // tm, n // tn, 2),
        in_specs=[piece, piece, w_tile, tile],
        out_specs=out_specs,
        compiler_params=_params("parallel", "parallel", "arbitrary"),
        name="matmul_pair" + ("_cast" if emit else ""),
    )(a0, a1, w, res)


def _ffn_up_kernel(a_ref, wg_ref, wu_ref, cw_ref, cb_ref, buf_ref, act_ref, tail_ref,
                   accg, accu, gbuf, *, nk, tiles_per_seq, tm):
    i = pl.program_id(1)
    k = pl.program_id(2)
    kw = FFN_CONV

    @pl.when(k == 0)
    def _():
        accg[...] = jnp.zeros_like(accg)
        accu[...] = jnp.zeros_like(accu)

    a = a_ref[...]
    accg[...] += _dot(a, wg_ref[...])
    accu[...] += _dot(a, wu_ref[...])

    @pl.when(k == nk - 1)
    def _():
        @pl.when(i % tiles_per_seq == 0)
        def _():
            gbuf[8 - (kw - 1):8, :] = buf_ref[0]

        g = accg[...]
        gbuf[8:8 + tm, :] = g
        y = g * cw_ref[kw - 1:kw, :] + cb_ref[...]
        for t in range(kw - 1):
            off = 8 - (kw - 1) + t
            y = y + gbuf[off:off + tm, :] * cw_ref[t:t + 1, :]
        act_ref[...] = (_gelu(y) * accu[...]).astype(act_ref.dtype)
        tail = gbuf[8 + tm - (kw - 1):8 + tm, :]
        gbuf[8 - (kw - 1):8, :] = tail
        tail_ref[0] = tail


def ffn_up_fused(a, w_up, conv_w, conv_b, conv_buf, seq_len, *, tm=1024, tn=1024, tk=2048):
    m, kdim = a.shape
    dff = w_up.shape[1] // 2
    tm, tn, tk = min(tm, seq_len), min(tn, dff), min(tk, kdim)
    assert seq_len % tm == 0 and dff % tn == 0 and kdim % tk == 0 and tm % 8 == 0
    nb = m // seq_len
    tps = seq_len // tm
    nj, nk = dff // tn, kdim // tk
    kw = FFN_CONV
    kern = functools.partial(_ffn_up_kernel, nk=nk, tiles_per_seq=tps, tm=tm)
    return pl.pallas_call(
        kern,
        out_shape=(jax.ShapeDtypeStruct((m, dff), BF16), jax.ShapeDtypeStruct((nb, kw - 1, dff), F32)),
        grid=(nj, m // tm, nk),
        in_specs=[
            pl.BlockSpec((tm, tk), lambda j, i, k: (i, k)),
            pl.BlockSpec((tk, tn), lambda j, i, k: (k, j)),
            pl.BlockSpec((tk, tn), lambda j, i, k: (k, j + nj)),
            pl.BlockSpec((kw, tn), lambda j, i, k: (0, j)),
            pl.BlockSpec((1, tn), lambda j, i, k: (0, j)),
            pl.BlockSpec((1, kw - 1, tn), lambda j, i, k: (i // tps, 0, j)),
        ],
        out_specs=(
            pl.BlockSpec((tm, tn), lambda j, i, k: (i, j)),
            pl.BlockSpec((1, kw - 1, tn), lambda j, i, k: (i // tps, 0, j)),
        ),
        scratch_shapes=[pltpu.VMEM((tm, tn), F32), pltpu.VMEM((tm, tn), F32), pltpu.VMEM((8 + tm, tn), F32)],
        compiler_params=_params("parallel", "arbitrary", "arbitrary"),
        name="ffn_up_fused",
    )(a, w_up, w_up, conv_w, conv_b.reshape(1, dff), conv_buf)


def _convgate_kernel(s0_ref, s1_ref, s2_ref, u_ref, cw_ref, cb_ref, o_ref):
    y = s0_ref[...] * cw_ref[2:3, :] + cb_ref[...]
    y = y + s2_ref[...] * cw_ref[0:1, :]
    y = y + s1_ref[...] * cw_ref[1:2, :]
    o_ref[...] = (_gelu(y) * u_ref[...]).astype(o_ref.dtype)


def convgate(s0, s1, s2, u, conv_w, conv_b, *, tn=2048):
    m, dff = s0.shape
    tn = min(tn, dff)
    row = pl.BlockSpec((m, tn), lambda j: (0, j))
    return pl.pallas_call(
        _convgate_kernel,
        out_shape=jax.ShapeDtypeStruct((m, dff), BF16),
        grid=(dff // tn,),
        in_specs=[row, row, row, row, pl.BlockSpec((FFN_CONV, tn), lambda j: (0, j)),
                  pl.BlockSpec((1, tn), lambda j: (0, j))],
        out_specs=row,
        compiler_params=_params("parallel"),
        name="convgate",
    )(s0, s1, s2, u, conv_w, conv_b.reshape(1, dff))


def _fox_prompt_kernel(q_ref, k_ref, v_ref, cq_ref, ck_ref, o_ref, m_sc, l_sc, acc_sc, cq_sc, *, nk, tq, tk):
    qi = pl.program_id(2)
    kj = pl.program_id(3)

    @pl.when(kj == 0)
    def _():
        m_sc[...] = jnp.full_like(m_sc, NEG_INF)
        l_sc[...] = jnp.zeros_like(l_sc)
        acc_sc[...] = jnp.zeros_like(acc_sc)
        cq_sc[...] = jnp.transpose(jnp.broadcast_to(cq_ref[0, 0], (128, tq)))

    @pl.when(kj * tk <= qi * tq + tq - 1)
    def _():
        q = q_ref[0].astype(BF16)
        k = k_ref[0].astype(BF16)
        cq = jnp.concatenate([cq_sc[...]] * (tk // 128), axis=1)
        s = _dot_nt(q, k) * SCALE + (cq - ck_ref[0, 0])
        row = qi * tq + lax.broadcasted_iota(jnp.int32, (tq, tk), 0)
        col = kj * tk + lax.broadcasted_iota(jnp.int32, (tq, tk), 1)
        s = jnp.where(row >= col, s, NEG_INF)
        _online_softmax_step(s, v_ref[0].astype(BF16), m_sc, l_sc, acc_sc)

    @pl.when(kj == nk - 1)
    def _():
        o_ref[0] = (acc_sc[...] / jnp.maximum(l_sc[...], 1e-30)).astype(o_ref.dtype)


def fox_prompt(proj3, c_new):
    b, t, _ = proj3.shape
    tq = min(512, t)
    tk = min(1024, t)
    assert tq % 128 == 0 and tk % 128 == 0
    nq, nk = t // tq, t // tk
    h = FOX_HEADS
    c_row = c_new.transpose(0, 2, 1).reshape(b, h, 1, t)
    kq, kk, kv = COL_FQ // HEAD_DIM, COL_FK // HEAD_DIM, COL_FV // HEAD_DIM

    def kidx(qi, kj):
        return jnp.minimum(kj, (qi * tq + tq - 1) // tk)

    kern = functools.partial(_fox_prompt_kernel, nk=nk, tq=tq, tk=tk)
    return pl.pallas_call(
        kern,
        out_shape=jax.ShapeDtypeStruct((b, t, FOX_W), BF16),
        grid=(b, h, nq, nk),
        in_specs=[
            pl.BlockSpec((1, tq, HEAD_DIM), lambda bi, hi, qi, kj: (bi, qi, kq + hi)),
            pl.BlockSpec((1, tk, HEAD_DIM), lambda bi, hi, qi, kj: (bi, kidx(qi, kj), kk + hi)),
            pl.BlockSpec((1, tk, HEAD_DIM), lambda bi, hi, qi, kj: (bi, kidx(qi, kj), kv + hi)),
            pl.BlockSpec((1, 1, 1, tq), lambda bi, hi, qi, kj: (bi, hi, 0, qi)),
            pl.BlockSpec((1, 1, 1, tk), lambda bi, hi, qi, kj: (bi, hi, 0, kidx(qi, kj))),
        ],
        out_specs=pl.BlockSpec((1, tq, HEAD_DIM), lambda bi, hi, qi, kj: (bi, qi, hi)),
        scratch_shapes=[pltpu.VMEM((tq, 128), F32), pltpu.VMEM((tq, 128), F32), pltpu.VMEM((tq, HEAD_DIM), F32),
                        pltpu.VMEM((tq, 128), F32)],
        compiler_params=_params("parallel", "parallel", "parallel", "arbitrary"),
        name="fox_prompt",
    )(proj3, proj3, proj3, c_row, c_row)


def _compress_kernel(x_ref, pe_ref, w1_ref, w2_ref, o_ref, *, tr):
    acc = jnp.zeros((tr, HEAD_DIM), F32)
    for l in range(CMP_BLOCK):
        xl = x_ref[pl.ds(l, tr, stride=CMP_BLOCK), :] + pe_ref[l:l + 1, :]
        acc = acc + _dot(xl.astype(BF16), w1_ref[l])
    o_ref[0] = _dot(_gelu(acc).astype(BF16), w2_ref[...])


def compress(x2d, col_block, n_groups, cmp_w):
    w1, pe, w2 = cmp_w
    rows = x2d.shape[0]
    r = rows // CMP_BLOCK
    tr = min(r, 256)
    assert r % tr == 0 and rows % CMP_BLOCK == 0
    return pl.pallas_call(
        functools.partial(_compress_kernel, tr=tr),
        out_shape=jax.ShapeDtypeStruct((n_groups, r, HEAD_DIM), F32),
        grid=(n_groups, r // tr),
        in_specs=[pl.BlockSpec((tr * CMP_BLOCK, HEAD_DIM), lambda g, i: (i, col_block + g)),
                  pl.BlockSpec((CMP_BLOCK, HEAD_DIM), lambda g, i: (0, 0)),
                  pl.BlockSpec((CMP_BLOCK, HEAD_DIM, HEAD_DIM), lambda g, i: (0, 0, 0)),
                  pl.BlockSpec((HEAD_DIM, HEAD_DIM), lambda g, i: (0, 0))],
        out_specs=pl.BlockSpec((1, tr, HEAD_DIM), lambda g, i: (g, i, 0)),
        compiler_params=_params("parallel", "parallel"),
        name="nsa_compress",
    )(x2d, pe, w1.astype(BF16), w2.astype(BF16))


def _cmp_prompt_kernel(q_ref, ck_ref, cv_ref, bias_ref, exp_ref, o_ref, sel_ref, *, tq, nc, n_top):
    qi = pl.program_id(2)
    ns = nc // 2
    t_col = qi * tq + lax.broadcasted_iota(jnp.int32, (tq, nc), 0)
    lane = lax.broadcasted_iota(jnp.int32, (tq, nc), 1)
    blk_c = jnp.where(lane < ns, 2 * lane, 2 * (lane - ns) + 1)
    mask = t_col >= (blk_c + 1) * CMP_BLOCK - 1
    ck = ck_ref[0, 0].astype(BF16)
    cv = cv_ref[0, 0].astype(BF16)
    imp = jnp.zeros((tq, nc), F32)
    for hg in range(NSA_GROUP):
        sl = slice(hg * HEAD_DIM, (hg + 1) * HEAD_DIM)
        s = _dot_nt(q_ref[0, :, sl].astype(BF16), ck) * SCALE + bias_ref[hg]
        s = jnp.where(mask, s, NEG_INF)
        m = jnp.max(s, axis=-1, keepdims=True)
        e = jnp.where(mask, jnp.exp(s - m), 0.0)
        p = e / jnp.maximum(jnp.sum(e, axis=-1, keepdims=True), 1e-30)
        imp = imp + p
        o_ref[0, :, sl] = _dot(p.astype(BF16), cv)
    imp_s = imp[:, :ns] + imp[:, ns:]
    t_s = qi * tq + lax.broadcasted_iota(jnp.int32, (tq, ns), 0)
    blk = lax.broadcasted_iota(jnp.int32, (tq, ns), 1)
    cur = t_s // SEL_BLOCK
    valid = blk * SEL_BLOCK <= t_s
    forced = (blk == 0) | (blk == cur) | (blk == cur - 1)
    score = jnp.where(forced, SEL_FORCE, jnp.where(valid, imp_s, -1.0))
    rank = jnp.zeros((tq, ns), F32)
    for j in range(ns):
        cj = score[:, j:j + 1]
        ahead = (cj > score) | ((cj == score) & (blk > j))
        rank = rank + jnp.where(ahead, 1.0, 0.0)
    sel = jnp.where(rank < n_top, 1.0, 0.0).astype(BF16)
    sel_ref[0, 0] = (1.0 - _dot(sel, exp_ref[...])) * NEG_INF


def cmp_prompt(proj3, ck_perm, cv_perm, bias_cmp):
    b, t, _ = proj3.shape
    nc = t // CMP_BLOCK
    ns = t // SEL_BLOCK
    tq = min(256, t)
    g = NSA_KV_HEADS
    n_top = min(N_SEL, ns)
    expand = jnp.asarray(np.repeat(np.eye(ns, dtype=np.float32), SEL_BLOCK, axis=1), BF16)
    kq = COL_NQ // GROUP_W
    kern = functools.partial(_cmp_prompt_kernel, tq=tq, nc=nc, n_top=n_top)
    return pl.pallas_call(
        kern,
        out_shape=(jax.ShapeDtypeStruct((b, t, NSA_W), F32), jax.ShapeDtypeStruct((b, g, t, t), F32)),
        grid=(b, g, t // tq),
        in_specs=[
            pl.BlockSpec((1, tq, GROUP_W), lambda bi, gi, qi: (bi, qi, kq + gi)),
            pl.BlockSpec((1, 1, nc, HEAD_DIM), lambda bi, gi, qi: (gi, bi, 0, 0)),
            pl.BlockSpec((1, 1, nc, HEAD_DIM), lambda bi, gi, qi: (gi, bi, 0, 0)),
            pl.BlockSpec((NSA_GROUP, tq, nc), lambda bi, gi, qi: (gi, qi, 0)),
            pl.BlockSpec((ns, t), lambda bi, gi, qi: (0, 0)),
        ],
        out_specs=(
            pl.BlockSpec((1, tq, GROUP_W), lambda bi, gi, qi: (bi, qi, gi)),
            pl.BlockSpec((1, 1, tq, t), lambda bi, gi, qi: (bi, gi, qi, 0)),
        ),
        compiler_params=_params("parallel", "parallel", "parallel"),
        name="nsa_cmp_prompt",
    )(proj3, ck_perm, cv_perm, bias_cmp, expand)


NSA_TQ = 128
NSA_TK = 512


def _nsa_band_kernel(*refs, nr, ratio, o_max, mode, has_sel):
    if has_sel:
        q_ref, k_ref, v_ref, bias_ref, sel_ref, o_ref, qs, m_sc, l_sc, acc_sc = refs
    else:
        q_ref, k_ref, v_ref, bias_ref, o_ref, qs, m_sc, l_sc, acc_sc = refs
        sel_ref = None
    qi = pl.program_id(2)
    r = pl.program_id(3)
    tq = NSA_TQ
    tk = k_ref.shape[1]
    hg_n = NSA_GROUP

    @pl.when(r == 0)
    def _():
        for hg in range(hg_n):
            qs[hg * tq:(hg + 1) * tq, :] = q_ref[0, :, hg * HEAD_DIM:(hg + 1) * HEAD_DIM].astype(BF16)
        m_sc[...] = jnp.full_like(m_sc, NEG_INF)
        l_sc[...] = jnp.zeros_like(l_sc)
        acc_sc[...] = jnp.zeros_like(acc_sc)

    if mode == "causal":
        active = r <= qi // ratio
        kj = jnp.minimum(r, qi // ratio)
    else:
        active = qi // ratio - (nr - 1) + r >= 0
        kj = jnp.maximum(qi // ratio - (nr - 1) + r, 0)
    chunk0 = o_max - jnp.clip(qi - ratio * kj, 0, o_max)

    @pl.when(active)
    def _():
        s = _dot_nt(qs[...], k_ref[0].astype(BF16)) * SCALE
        bias = jnp.concatenate([bias_ref[:, chunk0 + u] for u in range(ratio)], axis=-1)
        s3 = s.reshape(hg_n, tq, tk) + bias
        if has_sel:
            s3 = s3 + sel_ref[0, 0][None]
        _online_softmax_step(s3.reshape(hg_n * tq, tk), v_ref[0].astype(BF16), m_sc, l_sc, acc_sc)

    @pl.when(r == nr - 1)
    def _():
        o = acc_sc[...] / jnp.maximum(l_sc[...], 1e-30)
        for hg in range(hg_n):
            o_ref[0, :, hg * HEAD_DIM:(hg + 1) * HEAD_DIM] = o[hg * tq:(hg + 1) * tq, :]


def nsa_band(proj3, bias_strip, tk, col_k, col_v, mode, selmask=None):
    b, t, _ = proj3.shape
    tq = NSA_TQ
    assert t % tk == 0 and tk % tq == 0
    ratio = tk // tq
    nq, nk = t // tq, t // tk
    g = NSA_KV_HEADS
    n_chunks = bias_strip.shape[1]
    o_max = n_chunks - ratio
    nr = nk if mode == "causal" else -(-WINDOW // tk) + 1
    assert mode == "causal" or o_max == ratio * nr - 1
    kq = COL_NQ // GROUP_W
    ck, cv = col_k // HEAD_DIM, col_v // HEAD_DIM
    if mode == "causal":
        def ktile(qi, r):
            return jnp.minimum(r, qi // ratio)
    else:
        def ktile(qi, r):
            return jnp.maximum(qi // ratio - (nr - 1) + r, 0)

    in_specs = [
        pl.BlockSpec((1, tq, GROUP_W), lambda bi, gi, qi, r: (bi, qi, kq + gi)),
        pl.BlockSpec((1, tk, HEAD_DIM), lambda bi, gi, qi, r: (bi, ktile(qi, r), ck + gi)),
        pl.BlockSpec((1, tk, HEAD_DIM), lambda bi, gi, qi, r: (bi, ktile(qi, r), cv + gi)),
        pl.BlockSpec((NSA_GROUP, n_chunks, tq, 128), lambda bi, gi, qi, r: (gi, 0, 0, 0)),
    ]
    args = [proj3, proj3, proj3, bias_strip]
    if selmask is not None:
        in_specs.append(pl.BlockSpec((1, 1, tq, tk), lambda bi, gi, qi, r: (bi, gi, qi, ktile(qi, r))))
        args.append(selmask)
    kern = functools.partial(_nsa_band_kernel, nr=nr, ratio=ratio, o_max=o_max, mode=mode,
                             has_sel=selmask is not None)
    rows = NSA_GROUP * tq
    return pl.pallas_call(
        kern,
        out_shape=jax.ShapeDtypeStruct((b, t, NSA_W), F32),
        grid=(b, g, nq, nr),
        in_specs=in_specs,
        out_specs=pl.BlockSpec((1, tq, GROUP_W), lambda bi, gi, qi, r: (bi, qi, gi)),
        scratch_shapes=[pltpu.VMEM((rows, HEAD_DIM), BF16), pltpu.VMEM((rows, 128), F32),
                        pltpu.VMEM((rows, 128), F32), pltpu.VMEM((rows, HEAD_DIM), F32)],
        compiler_params=_params("parallel", "parallel", "parallel", "arbitrary"),
        name="nsa_band_" + mode,
    )(*args)


_T5_THRESHOLDS = tuple(int(np.searchsorted(_bucket_np(np.arange(2 * MAX_DISTANCE)), k)) for k in range(1, N_BUCKETS))


def _t5_bias(rel_bias, dist):
    per_head = lambda k: rel_bias[k].reshape((-1,) + (1,) * dist.ndim)
    d = dist[None]
    out = jnp.broadcast_to(per_head(0), (rel_bias.shape[1],) + dist.shape)
    for k, thr in enumerate(_T5_THRESHOLDS, start=1):
        out = jnp.where(d >= thr, per_head(k), out)
    return out


def _band_strip(rel_bias, o_max, ratio, window=None):
    c = jnp.arange(o_max + ratio, dtype=jnp.int32)[:, None, None]
    i = jnp.arange(NSA_TQ, dtype=jnp.int32)[None, :, None]
    jj = jnp.arange(128, dtype=jnp.int32)[None, None, :]
    d = i - (c * 128 + jj) + o_max * NSA_TQ
    ok = d >= 0
    if window is not None:
        ok = ok & (d < window)
    return jnp.where(ok[None], _t5_bias(rel_bias, d), NEG_INF)


def _combine_kernel(small_ref, oc_ref, os_ref, ow_ref, o_ref):
    gates = _sigmoid(small_ref[...])
    for h in range(NSA_HEADS):
        sl = slice(h * HEAD_DIM, (h + 1) * HEAD_DIM)
        base = FOX_HEADS + h
        g0 = gates[:, base:base + 1]
        g1 = gates[:, base + NSA_HEADS:base + NSA_HEADS + 1]
        g2 = gates[:, base + 2 * NSA_HEADS:base + 2 * NSA_HEADS + 1]
        o = g0 * oc_ref[:, sl] + g1 * os_ref[:, sl] + g2 * ow_ref[:, sl]
        o_ref[:, sl] = o.astype(o_ref.dtype)


def nsa_combine(proj, o_cmp, o_slc, o_win):
    m = proj.shape[0]
    tm = min(m, 256)
    wide = pl.BlockSpec((tm, NSA_W), lambda i: (i, 0))
    return pl.pallas_call(
        _combine_kernel,
        out_shape=jax.ShapeDtypeStruct((m, NSA_W), BF16),
        grid=(m // tm,),
        in_specs=[pl.BlockSpec((tm, 128), lambda i: (i, COL_SMALL // 128)), wide, wide, wide],
        out_specs=wide,
        compiler_params=_params("parallel"),
        name="nsa_combine",
    )(proj, o_cmp, o_slc, o_win)


def _lru_kernel(xr_ref, gate_ref, cw_ref, cb_ref, wa_ref, wx_ref, ba_ref, bx_ref, lam_ref, buf_ref, h0_ref,
                y_ref, hlast_ref, nbuf_ref, xbuf, hcar, hsbuf, *, nt, tt, last_row):
    ti = pl.program_id(2)
    kw = LRU_CONV

    @pl.when(ti == 0)
    def _():
        xbuf[8 - (kw - 1):8, :] = buf_ref[0]
        hcar[...] = h0_ref[0]

    x = xr_ref[0]
    xbuf[8:8 + tt, :] = x
    xc = x * cw_ref[kw - 1:kw, :] + cb_ref[...]
    for t in range(kw - 1):
        off = 8 - (kw - 1) + t
        xc = xc + xbuf[off:off + tt, :] * cw_ref[t:t + 1, :]
    xcb = xc.astype(BF16)
    r = _sigmoid(_dot(xcb, wa_ref[0]) + ba_ref[...])
    ig = _sigmoid(_dot(xcb, wx_ref[0]) + bx_ref[...])
    neg_lam = -lam_ref[...]
    softplus = jnp.maximum(neg_lam, 0.0) + jnp.log1p(jnp.exp(-jnp.abs(neg_lam)))
    log_a = -LRU_C * r * softplus
    a = jnp.exp(log_a)
    th = jnp.tanh(log_a)
    u = jnp.sqrt(-2.0 * th / (1.0 - th)) * (ig * xc)
    ng = tt // 8
    a = a.reshape(ng, 8, a.shape[-1])
    u = u.reshape(ng, 8, u.shape[-1])
    sub = lax.broadcasted_iota(jnp.int32, a.shape, 1)
    for step in (1, 2, 4):
        keep = sub >= step
        a_sh = jnp.where(keep, pltpu.roll(a, step, 1), 1.0)
        u_sh = jnp.where(keep, pltpu.roll(u, step, 1), 0.0)
        u = a * u_sh + u
        a = a * a_sh
    h = hcar[...]
    for gi in range(ng):
        hs_g = a[gi] * h + u[gi]
        hsbuf[gi * 8:(gi + 1) * 8, :] = hs_g
        h = hs_g[7:8, :]
    hcar[...] = h
    y_ref[0] = (_gelu(gate_ref[0]) * hsbuf[...]).astype(y_ref.dtype)

    @pl.when(ti == nt - 1)
    def _():
        hlast_ref[0] = hsbuf[last_row:last_row + 1, :]
        nbuf_ref[0] = xbuf[8 + last_row - (kw - 2):8 + last_row + 1, :]

    xbuf[8 - (kw - 1):8, :] = xbuf[8 + tt - (kw - 1):8 + tt, :]


def lru_block(proj1, conv_w, conv_b, w_a, b_a, w_x, b_x, lam, conv_buf, h0, t_valid):
    b, tp, w2 = proj1.shape
    w = w2 // 2
    nh = LRU_HEADS
    bw = w // nh
    tt = min(tp, 256)
    assert tp % tt == 0 and tt % 8 == 0
    nt = tp // tt
    last_row = (t_valid - 1) - (nt - 1) * tt
    assert 0 <= last_row < tt and (nt > 1 or last_row >= 0)
    kw = LRU_CONV
    vec = lambda a: a.reshape(1, w)
    chan = pl.BlockSpec((1, bw), lambda bi, hi, ti: (0, hi))
    kern = functools.partial(_lru_kernel, nt=nt, tt=tt, last_row=last_row)
    return pl.pallas_call(
        kern,
        out_shape=(jax.ShapeDtypeStruct((b, tp, w), BF16), jax.ShapeDtypeStruct((b, 1, w), F32),
                   jax.ShapeDtypeStruct((b, kw - 1, w), F32)),
        grid=(b, nh, nt),
        in_specs=[
            pl.BlockSpec((1, tt, bw), lambda bi, hi, ti: (bi, ti, nh + hi)),
            pl.BlockSpec((1, tt, bw), lambda bi, hi, ti: (bi, ti, hi)),
            pl.BlockSpec((kw, bw), lambda bi, hi, ti: (0, hi)),
            chan,
            pl.BlockSpec((1, bw, bw), lambda bi, hi, ti: (hi, 0, 0)),
            pl.BlockSpec((1, bw, bw), lambda bi, hi, ti: (hi, 0, 0)),
            chan, chan, chan,
            pl.BlockSpec((1, kw - 1, bw), lambda bi, hi, ti: (bi, 0, hi)),
            pl.BlockSpec((1, 1, bw), lambda bi, hi, ti: (bi, 0, hi)),
        ],
        out_specs=(
            pl.BlockSpec((1, tt, bw), lambda bi, hi, ti: (bi, ti, hi)),
            pl.BlockSpec((1, 1, bw), lambda bi, hi, ti: (bi, 0, hi)),
            pl.BlockSpec((1, kw - 1, bw), lambda bi, hi, ti: (bi, 0, hi)),
        ),
        scratch_shapes=[pltpu.VMEM((8 + tt, bw), F32), pltpu.VMEM((1, bw), F32), pltpu.VMEM((tt, bw), F32)],
        compiler_params=_params("parallel", "parallel", "arbitrary"),
        name="rglru",
    )(proj1, proj1, conv_w, vec(conv_b), w_a.astype(BF16), w_x.astype(BF16), vec(b_a), vec(b_x), vec(lam),
      conv_buf, h0.reshape(b, 1, w))


GATHER_PAGES = 8


def _gather_kernel(pt_ref, *refs, n_pools, pg):
    ins, outs = refs[:n_pools * pg], refs[n_pools * pg:]
    for c in range(n_pools):
        for i in range(pg):
            src = ins[c * pg + i]
            rows = slice(i * PAGE_SIZE, (i + 1) * PAGE_SIZE)
            if len(src.shape) == 4:
                for g in range(src.shape[2]):
                    outs[c][0, g, rows, :] = src[0, :, g, :]
            else:
                outs[c][0, rows, :] = src[0]


def gather_pools(pools, page_table, extra_rows):
    b, n_pages = page_table.shape
    pg = math.gcd(GATHER_PAGES, n_pages)
    p = n_pages * PAGE_SIZE
    in_specs, args, out_shape, out_specs = [], [], [], []
    for pool in pools:
        tail = pool.shape[2:]
        zeros = (0,) * (1 + len(tail))
        for i in range(pg):
            in_specs.append(pl.BlockSpec((1, PAGE_SIZE) + tail, lambda bi, j, pt, i=i, z=zeros: (pt[bi, j * pg + i],) + z))
            args.append(pool)
        if len(tail) == 2:
            out_shape.append(jax.ShapeDtypeStruct((b, tail[0], p + extra_rows, tail[1]), pool.dtype))
            out_specs.append(pl.BlockSpec((1, tail[0], pg * PAGE_SIZE, tail[1]), lambda bi, j, pt: (bi, 0, j, 0)))
        else:
            out_shape.append(jax.ShapeDtypeStruct((b, p + extra_rows, tail[0]), pool.dtype))
            out_specs.append(pl.BlockSpec((1, pg * PAGE_SIZE, tail[0]), lambda bi, j, pt: (bi, j, 0)))
    out_shape, out_specs = tuple(out_shape), tuple(out_specs)
    return pl.pallas_call(
        functools.partial(_gather_kernel, n_pools=len(pools), pg=pg),
        out_shape=out_shape,
        grid_spec=pltpu.PrefetchScalarGridSpec(
            num_scalar_prefetch=1, grid=(b, n_pages // pg), in_specs=in_specs, out_specs=out_specs),
        compiler_params=_params("parallel", "arbitrary"),
        name="gather_pages",
    )(page_table, *args)


FOX_DECODE_PAGES = 4


def _fox_decode_kernel(pt_ref, q_ref, cq_ref, ck_ref, cknew_ref, *refs, n_steps, pf):
    kp, vp = refs[:pf], refs[pf:2 * pf]
    knew_ref, vnew_ref, o_ref, m_sc, l_sc, acc_sc, cq_sc = refs[2 * pf:]
    j = pl.program_id(1)
    rows = q_ref.shape[1]
    nh = FOX_HEADS
    w = PAGE_SIZE * nh

    @pl.when(j == 0)
    def _():
        m_sc[...] = jnp.full_like(m_sc, NEG_INF)
        l_sc[...] = jnp.zeros_like(l_sc)
        acc_sc[...] = jnp.zeros_like(acc_sc)
        col = lax.broadcasted_iota(jnp.int32, (rows, w), 1)
        row = lax.broadcasted_iota(jnp.int32, (rows, w), 0)
        cq_sc[...] = jnp.where((col % nh) == (row % nh), cq_ref[0], NEG_INF)

    def update(k3, v3, ck_row, causal):
        n = k3.shape[0] * nh
        k2 = k3.reshape(n, HEAD_DIM).astype(BF16)
        v2 = v3.reshape(n, HEAD_DIM).astype(BF16)
        s = _dot_nt(q_ref[0], k2) * SCALE + (cq_sc[:, :n] - ck_row)
        if causal:
            col = lax.broadcasted_iota(jnp.int32, (rows, n), 1)
            row = lax.broadcasted_iota(jnp.int32, (rows, n), 0)
            s = jnp.where(col // nh <= row // nh, s, NEG_INF)
        _online_softmax_step(s, v2, m_sc, l_sc, acc_sc)

    @pl.when(j < n_steps)
    def _():
        for i in range(pf):
            update(kp[i][0], vp[i][0], ck_ref[0, :, i * w:(i + 1) * w], False)

    @pl.when(j == n_steps)
    def _():
        update(knew_ref[0], vnew_ref[0], cknew_ref[0], True)
        o_ref[0] = acc_sc[...] / jnp.maximum(l_sc[...], 1e-30)


def fox_decode(q_rows, cq_rows, ck_past, ck_new, k_pool, v_pool, k_new, v_new, page_table):
    b, n_pages = page_table.shape
    pf = math.gcd(FOX_DECODE_PAGES, n_pages)
    n_steps = n_pages // pf
    rows = q_rows.shape[1]
    tp = k_new.shape[1]
    nh = FOX_HEADS

    def page(i):
        return lambda bi, j, pt: (pt[bi, jnp.minimum(j * pf + i, n_pages - 1)], 0, 0, 0)

    pool_spec = [pl.BlockSpec((1, PAGE_SIZE, nh, HEAD_DIM), page(i)) for i in range(pf)]
    per_seq = lambda shape: pl.BlockSpec(shape, lambda bi, j, pt: (bi,) + (0,) * (len(shape) - 1))
    kern = functools.partial(_fox_decode_kernel, n_steps=n_steps, pf=pf)
    return pl.pallas_call(
        kern,
        out_shape=jax.ShapeDtypeStruct((b, rows, HEAD_DIM), F32),
        grid_spec=pltpu.PrefetchScalarGridSpec(
            num_scalar_prefetch=1,
            grid=(b, n_steps + 1),
            in_specs=[per_seq((1, rows, HEAD_DIM)), per_seq((1, rows, 1)),
                      pl.BlockSpec((1, 1, pf * PAGE_SIZE * nh), lambda bi, j, pt: (bi, 0, jnp.minimum(j, n_steps - 1))),
                      per_seq((1, 1, tp * nh))]
                     + pool_spec + pool_spec
                     + [per_seq((1, tp, nh, HEAD_DIM)), per_seq((1, tp, nh, HEAD_DIM))],
            out_specs=per_seq((1, rows, HEAD_DIM)),
            scratch_shapes=[pltpu.VMEM((rows, 128), F32), pltpu.VMEM((rows, 128), F32),
                            pltpu.VMEM((rows, HEAD_DIM), F32), pltpu.VMEM((rows, PAGE_SIZE * nh), F32)],
        ),
        compiler_params=_params("parallel", "arbitrary"),
        name="fox_decode",
    )(page_table, q_rows, cq_rows, ck_past, ck_new, *([k_pool] * pf), *([v_pool] * pf), k_new, v_new)


def _cmp_small_kernel(q_ref, k_ref, v_ref, bias_ref, o_ref, imp_ref, *, t_new):
    s = _dot_nt(q_ref[0, 0], k_ref[0, 0].astype(BF16)) * SCALE + bias_ref[0]
    mask = s > VISIBLE_MIN
    m = jnp.max(s, axis=-1, keepdims=True)
    e = jnp.where(mask, jnp.exp(s - m), 0.0)
    p = e / jnp.maximum(jnp.sum(e, axis=-1, keepdims=True), 1e-30)
    o_ref[0, 0] = _dot(p.astype(BF16), v_ref[0, 0].astype(BF16))
    imp_ref[0, 0] = jnp.sum(p.reshape(t_new, NSA_GROUP, p.shape[-1]), axis=1)


def cmp_small(q_rows, ck, cv, bias, t_new):
    b, g, rows, d = q_rows.shape
    s = ck.shape[2]
    return pl.pallas_call(
        functools.partial(_cmp_small_kernel, t_new=t_new),
        out_shape=(jax.ShapeDtypeStruct((b, g, rows, d), F32), jax.ShapeDtypeStruct((b, g, t_new, s), F32)),
        grid=(b, g),
        in_specs=[
            pl.BlockSpec((1, 1, rows, d), lambda bi, gi: (bi, gi, 0, 0)),
            pl.BlockSpec((1, 1, s, d), lambda bi, gi: (bi, gi, 0, 0)),
            pl.BlockSpec((1, 1, s, d), lambda bi, gi: (bi, gi, 0, 0)),
            pl.BlockSpec((1, rows, s), lambda bi, gi: (gi, 0, 0)),
        ],
        out_specs=(pl.BlockSpec((1, 1, rows, d), lambda bi, gi: (bi, gi, 0, 0)),
                   pl.BlockSpec((1, 1, t_new, s), lambda bi, gi: (bi, gi, 0, 0))),
        compiler_params=_params("parallel", "parallel"),
        name="nsa_cmp_sample",
    )(q_rows, ck, cv, bias)


def _flash_small_kernel(q_ref, k_ref, v_ref, bias_ref, o_ref, m_sc, l_sc, acc_sc, *, nk):
    j = pl.program_id(2)

    @pl.when(j == 0)
    def _():
        m_sc[...] = jnp.full_like(m_sc, NEG_INF)
        l_sc[...] = jnp.zeros_like(l_sc)
        acc_sc[...] = jnp.zeros_like(acc_sc)

    s = _dot_nt(q_ref[0, 0], k_ref[0, 0].astype(BF16)) * SCALE + bias_ref[0, 0]
    vis = s > VISIBLE_MIN
    m_prev = m_sc[...]
    m_new = jnp.maximum(m_prev, jnp.max(s, axis=-1, keepdims=True))
    alpha = jnp.exp(m_prev - m_new)
    p = jnp.where(vis, jnp.exp(s - m_new), 0.0)
    l_sc[...] = alpha * l_sc[...] + jnp.sum(p, axis=-1, keepdims=True)
    acc_sc[...] = alpha * acc_sc[...] + _dot(p.astype(BF16), v_ref[0, 0].astype(BF16))
    m_sc[...] = m_new

    @pl.when(j == nk - 1)
    def _():
        o_ref[0, 0] = acc_sc[...] / jnp.maximum(l_sc[...], 1e-30)


def flash_small(q_rows, k, v, bias, tk):
    b, g, rows, d = q_rows.shape
    s = k.shape[2]
    assert s % tk == 0
    nk = s // tk
    return pl.pallas_call(
        functools.partial(_flash_small_kernel, nk=nk),
        out_shape=jax.ShapeDtypeStruct((b, g, rows, d), F32),
        grid=(b, g, nk),
        in_specs=[
            pl.BlockSpec((1, 1, rows, d), lambda bi, gi, j: (bi, gi, 0, 0)),
            pl.BlockSpec((1, 1, tk, d), lambda bi, gi, j: (bi, gi, j, 0)),
            pl.BlockSpec((1, 1, tk, d), lambda bi, gi, j: (bi, gi, j, 0)),
            pl.BlockSpec((1, 1, rows, tk), lambda bi, gi, j: (bi, gi, 0, j)),
        ],
        out_specs=pl.BlockSpec((1, 1, rows, d), lambda bi, gi, j: (bi, gi, 0, 0)),
        scratch_shapes=[pltpu.VMEM((rows, 1), F32), pltpu.VMEM((rows, 1), F32), pltpu.VMEM((rows, d), F32)],
        compiler_params=_params("parallel", "parallel", "arbitrary"),
        name="nsa_flash_sample",
    )(q_rows, k, v, bias)


def _topk_kernel(col_ref, row_ref, o_ref, *, n_top):
    a = col_ref[...]
    bb = row_ref[...]
    shape = (a.shape[0], a.shape[1], bb.shape[2])
    j_idx = lax.broadcasted_iota(jnp.int32, shape, 1)
    i_idx = lax.broadcasted_iota(jnp.int32, shape, 2)
    ahead = (a > bb) | ((a == bb) & (j_idx < i_idx))
    rank = jnp.sum(jnp.where(ahead, 1.0, 0.0), axis=1, keepdims=True)
    o_ref[...] = jnp.where(rank < n_top, 1.0, 0.0)


def topk_mask(score, n_top):
    r, ns = score.shape
    tr = 8 if r % 8 == 0 else r
    out = pl.pallas_call(
        functools.partial(_topk_kernel, n_top=n_top),
        out_shape=jax.ShapeDtypeStruct((r, 1, ns), F32),
        grid=(r // tr,),
        in_specs=[pl.BlockSpec((tr, ns, 1), lambda i: (i, 0, 0)), pl.BlockSpec((tr, 1, ns), lambda i: (i, 0, 0))],
        out_specs=pl.BlockSpec((tr, 1, ns), lambda i: (i, 0, 0)),
        compiler_params=_params("parallel"),
        name="nsa_topk",
    )(score.reshape(r, ns, 1), score.reshape(r, 1, ns))
    return out.reshape(r, ns)


def _pack_w_in0(w_in0):
    d = w_in0.shape[0]
    off_fz = 3 * FOX_W
    off_nq = off_fz + FOX_HEADS
    off_kv = off_nq + NSA_W
    off_ng = off_kv + 6 * NSA_KV_W
    parts = [w_in0[:, :off_fz], w_in0[:, off_nq:off_ng], w_in0[:, off_fz:off_nq], w_in0[:, off_ng:]]
    used = sum(p.shape[1] for p in parts)
    parts.append(jnp.zeros((d, L0_PACKED - used), w_in0.dtype))
    return jnp.concatenate(parts, axis=1).astype(BF16)


def _forget_terms(proj3, b_forget):
    fz = proj3[:, :, COL_SMALL:COL_SMALL + FOX_HEADS]
    lf = jax.nn.log_sigmoid(fz + b_forget.astype(F32))
    return lf, jnp.cumsum(lf, axis=1)


def _kv_cols(proj3, idx):
    off = COL_KV + idx * NSA_KV_W
    return proj3[:, :, off:off + NSA_KV_W]


def attn_prompt(h_bf16, b, t, wts):
    m = b * t
    g = NSA_KV_HEADS
    proj = matmul(h_bf16, wts["w_in0"])
    proj3 = proj.reshape(b, t, L0_PACKED)
    lf, c_new = _forget_terms(proj3, wts["b_forget"])
    o_fox = fox_prompt(proj3, c_new)

    kc, vc, ks, vs, kw, vw = [_kv_cols(proj3, i) for i in range(6)]
    nc, ns = t // CMP_BLOCK, t // SEL_BLOCK
    tk = min(NSA_TK, t)
    assert t % (2 * CMP_BLOCK) == 0 and t % tk == 0
    ck = compress(proj, COL_KV // HEAD_DIM, g, wts["cmp_k"]).reshape(g, b, nc, HEAD_DIM)
    cv = compress(proj, (COL_KV + NSA_KV_W) // HEAD_DIM, g, wts["cmp_v"]).reshape(g, b, nc, HEAD_DIM)
    even_odd = lambda a: jnp.concatenate([a[:, :, 0::2], a[:, :, 1::2]], axis=2)
    rel = wts["rel_bias"].astype(F32)
    blk_c = jnp.concatenate([jnp.arange(0, nc, 2, dtype=jnp.int32), jnp.arange(1, nc, 2, dtype=jnp.int32)])
    c_dist = jnp.arange(t, dtype=jnp.int32)[:, None] - ((blk_c + 1) * CMP_BLOCK - 1)[None, :]
    bias_cmp = _t5_bias(rel, c_dist)
    o_cmp, selmask = cmp_prompt(proj3, even_odd(ck), even_odd(cv), bias_cmp)

    ratio = tk // NSA_TQ
    o_far = -(-(_T5_THRESHOLDS[-1] + tk - 1) // NSA_TQ)
    o_slc = nsa_band(proj3, _band_strip(rel, o_far, ratio), tk, COL_KV + 2 * NSA_KV_W, COL_KV + 3 * NSA_KV_W,
                     "causal", selmask)
    o_win_max = ratio * (-(-WINDOW // tk) + 1) - 1
    o_win = nsa_band(proj3, _band_strip(rel, o_win_max, ratio, window=WINDOW), tk,
                     COL_KV + 4 * NSA_KV_W, COL_KV + 5 * NSA_KV_W, "window")

    o_nsa = nsa_combine(proj, o_cmp.reshape(m, NSA_W), o_slc.reshape(m, NSA_W), o_win.reshape(m, NSA_W))
    mix = (o_fox.reshape(m, FOX_W), o_nsa)
    win_len = wts["win_len"]
    zeros = jnp.zeros((b, WINDOW, NSA_KV_W), F32)
    kw_a = jnp.concatenate([zeros, kw], axis=1)
    vw_a = jnp.concatenate([zeros, vw], axis=1)
    fk = proj3[:, :, COL_FK:COL_FK + FOX_W]
    fv = proj3[:, :, COL_FV:COL_FV + FOX_W]
    new = (fk, fv, lf, kc, vc, ks, vs, kw_a[:, kw_a.shape[1] - win_len:], vw_a[:, vw_a.shape[1] - win_len:])
    return mix, new


def attn_sample(h_bf16, b, t, wts, caches, page_table, state_win_k, state_win_v):
    m = b * t
    g, hg = NSA_KV_HEADS, NSA_GROUP
    n_pages = page_table.shape[1]
    p = n_pages * PAGE_SIZE
    assert t <= PAGE_SIZE and p % SEL_BLOCK == 0
    proj = matmul(h_bf16, wts["w_in0"])
    proj3 = proj.reshape(b, t, L0_PACKED)
    lf, c_new = _forget_terms(proj3, wts["b_forget"])
    kc, vc, ks, vs, kw, vw = [_kv_cols(proj3, i) for i in range(6)]
    fq = proj3[:, :, COL_FQ:COL_FQ + FOX_W]
    fk = proj3[:, :, COL_FK:COL_FK + FOX_W]
    fv = proj3[:, :, COL_FV:COL_FV + FOX_W]
    q_pos = p + np.arange(t)

    cache_fox_k, cache_fox_v, cache_fox_lf, cache_cmp_k, cache_cmp_v, cache_slc_k, cache_slc_v = caches
    tk_slc = min(2048, p)
    pcmp_k, pcmp_v, plf = gather_pools([cache_cmp_k, cache_cmp_v, cache_fox_lf], page_table, 0)
    pslc_k, pslc_v = gather_pools([cache_slc_k, cache_slc_v], page_table, tk_slc)

    c_past = plf - lax.cumsum(plf, axis=1, reverse=True)
    tp = -(-t // 8) * 8
    pad_new = lambda a: jnp.concatenate([a, jnp.zeros((b, tp - t) + a.shape[2:], F32)], axis=1)
    o_fox = fox_decode(fq.reshape(b, t * FOX_HEADS, HEAD_DIM).astype(BF16), c_new.reshape(b, t * FOX_HEADS, 1),
                       c_past.reshape(b, 1, p * FOX_HEADS), pad_new(c_new).reshape(b, 1, tp * FOX_HEADS),
                       cache_fox_k, cache_fox_v,
                       pad_new(fk.reshape(b, t, FOX_HEADS, HEAD_DIM)), pad_new(fv.reshape(b, t, FOX_HEADS, HEAD_DIM)),
                       page_table).reshape(m, FOX_W)

    rel = wts["rel_bias"].astype(F32)
    nq = proj3[:, :, COL_NQ:COL_NQ + NSA_W].reshape(b, t, g, hg, HEAD_DIM)
    q_rows = nq.transpose(0, 2, 1, 3, 4).reshape(b, g, t * hg, HEAD_DIM).astype(BF16)

    def head_rows(x):
        tt, ss = x.shape[1], x.shape[2]
        return x.reshape(g, hg, tt, ss).transpose(0, 2, 1, 3).reshape(g, tt * hg, ss)

    l_tot = p + t
    l_pad = -(-l_tot // SEL_BLOCK) * SEL_BLOCK
    nc, ns = l_pad // CMP_BLOCK, l_pad // SEL_BLOCK
    nc_past = p // CMP_BLOCK
    assert nc_past % 2 == 0 and (nc_past + 1) * CMP_BLOCK - 1 > q_pos[-1]
    ck = compress(pcmp_k.reshape(b * g * p, HEAD_DIM), 0, 1, wts["cmp_k"]).reshape(b, g, nc_past, HEAD_DIM)
    cv = compress(pcmp_v.reshape(b * g * p, HEAD_DIM), 0, 1, wts["cmp_v"]).reshape(b, g, nc_past, HEAD_DIM)
    even_odd = lambda a: jnp.concatenate([a[:, :, 0::2], a[:, :, 1::2]], axis=2)
    blk_c = jnp.concatenate([jnp.arange(0, nc_past, 2, dtype=jnp.int32), jnp.arange(1, nc_past, 2, dtype=jnp.int32)])
    c_dist = jnp.asarray(q_pos, jnp.int32)[:, None] - ((blk_c + 1) * CMP_BLOCK - 1)[None, :]
    bias_cmp = jnp.where((c_dist >= 0)[None], _t5_bias(rel, c_dist), NEG_INF)
    o_cmp, imp = cmp_small(q_rows, even_odd(ck), even_odd(cv), head_rows(bias_cmp), t)
    ns_past = nc_past // 2
    imp_s = imp[..., :ns_past] + imp[..., ns_past:]
    imp_s = jnp.concatenate([imp_s, jnp.zeros((b, g, t, ns - ns_past), F32)], axis=-1)
    blk = np.arange(ns)[None, :]
    cur = (q_pos // SEL_BLOCK)[:, None]
    valid = blk * SEL_BLOCK <= q_pos[:, None]
    forced = (blk == 0) | (blk == cur) | (blk == cur - 1)
    score = jnp.where(jnp.asarray(forced), SEL_FORCE, jnp.where(jnp.asarray(valid), imp_s, -1.0))
    sel = topk_mask(score.reshape(b * g * t, ns), min(N_SEL, ns)).reshape(b, g, t, ns)

    s_slc = p + tk_slc
    qp = jnp.asarray(q_pos, jnp.int32)[:, None]
    key_pos = jnp.arange(s_slc, dtype=jnp.int32)[None, :]
    d_slc = qp - key_pos
    ok = (d_slc >= 0) & (key_pos < l_tot)
    bias_slc = head_rows(jnp.where(ok[None], _t5_bias(rel, d_slc), NEG_INF))
    n_full = p // SEL_BLOCK
    sel_keys = jnp.concatenate([jnp.repeat(sel[..., :n_full], SEL_BLOCK, axis=-1),
                                jnp.broadcast_to(sel[..., ns - 1:ns], (b, g, t, tk_slc))], axis=-1)
    assert ns - 1 == n_full
    sel_rows = jnp.broadcast_to(sel_keys[:, :, :, None, :], (b, g, t, hg, s_slc)).reshape(b, g, t * hg, s_slc)
    bias_slc = jnp.where(sel_rows > 0.5, bias_slc[None], NEG_INF)

    def head_major(a, rows):
        a = a.reshape(b, t, g, HEAD_DIM).transpose(0, 2, 1, 3)
        return jnp.concatenate([a, jnp.zeros((b, g, rows - t, HEAD_DIM), F32)], axis=2)

    pslc_k = lax.dynamic_update_slice(pslc_k, head_major(ks, tk_slc), (0, 0, p, 0))
    pslc_v = lax.dynamic_update_slice(pslc_v, head_major(vs, tk_slc), (0, 0, p, 0))
    o_slc = flash_small(q_rows, pslc_k, pslc_v, bias_slc, tk_slc)

    p_w = state_win_k.shape[1]
    s_win = -(-(p_w + t) // 128) * 128
    kw_a = jnp.concatenate([state_win_k.reshape(b, p_w, NSA_KV_W), kw], axis=1)
    vw_a = jnp.concatenate([state_win_v.reshape(b, p_w, NSA_KV_W), vw], axis=1)
    win_major = lambda st, new: jnp.concatenate([st.transpose(0, 2, 1, 3), head_major(new, s_win - p_w)], axis=2)
    w_idx = jnp.arange(s_win, dtype=jnp.int32)[None, :]
    wpos = (p - p_w) + w_idx
    wd = qp - wpos
    wok = (wd >= 0) & (wd < WINDOW) & (wpos >= 0) & (w_idx < p_w + t)
    bias_win = head_rows(jnp.where(wok[None], _t5_bias(rel, wd), NEG_INF))
    bias_win = jnp.broadcast_to(bias_win[None], (b,) + bias_win.shape)
    o_win = flash_small(q_rows, win_major(state_win_k, kw), win_major(state_win_v, vw), bias_win, s_win)

    def to_tokens(o):
        return o.reshape(b, g, t, hg, HEAD_DIM).transpose(0, 2, 1, 3, 4).reshape(m, NSA_W)

    o_nsa = nsa_combine(proj, to_tokens(o_cmp), to_tokens(o_slc), to_tokens(o_win))
    mix = (o_fox.astype(BF16), o_nsa)
    win_len = wts["win_len"]
    new = (fk, fv, lf, kc, vc, ks, vs, kw_a[:, kw_a.shape[1] - win_len:], vw_a[:, vw_a.shape[1] - win_len:])
    return mix, new


def _weight_matmul(wts, key, layer, call):
    cache = wts["bf16"]
    if (key, layer) in cache:
        return call(cache[(key, layer)], None)
    result = call(wts[key], layer)
    if isinstance(result, tuple):
        result, cache[(key, layer)] = result
    return result


def trunk(x, p_emb, wts, lru_conv0, lru_h0, ffn_conv0, past=None):
    b, t, d = x.shape
    m = b * t
    x2 = x.reshape(m, d)
    ffn_bufs = []
    for i in range(2):
        h = rmsnorm(x2, wts["norm_mix"][i], BF16)
        if i == 0:
            if past is None:
                mix, attn_new = attn_prompt(h, b, t, wts)
            else:
                mix, attn_new = attn_sample(h, b, t, wts, *past)
            x2 = _weight_matmul(wts, "w_out0", None, lambda w, _, r=x2: matmul_pair(mix[0], mix[1], w, r))
        else:
            proj1 = _weight_matmul(wts, "w_in1", None, lambda w, _: matmul(h, w)).reshape(b, t, -1)
            tp = -(-t // 8) * 8
            if tp != t:
                proj1 = jnp.concatenate([proj1, jnp.zeros((b, tp - t, proj1.shape[-1]), F32)], axis=1)
            gated, h_last, lru_buf = lru_block(proj1, wts["lru_conv_w"], wts["lru_conv_b"], wts["lru_w_a"],
                                               wts["lru_b_a"], wts["lru_w_x"], wts["lru_b_x"], wts["lru_lambda"],
                                               lru_conv0, lru_h0, t)
            g_in = gated[:, :t].reshape(m, -1)
            x2 = _weight_matmul(wts, "w_out1", None, lambda w, _: matmul(g_in, w, res=x2))
        hf = rmsnorm(x2, wts["norm_ffn"][i], BF16)
        dff = wts["ffn_w_up"].shape[2] // 2
        if t % 8 == 0:
            if ("ffn_w_up", i) not in wts["bf16"]:
                wts["bf16"][("ffn_w_up", i)] = wts["ffn_w_up"][i].astype(BF16)
            act, buf = ffn_up_fused(hf, wts["bf16"][("ffn_w_up", i)], wts["ffn_conv_w"][i], wts["ffn_conv_b"][i],
                                    ffn_conv0[i], t)
        else:
            gu = _weight_matmul(wts, "ffn_w_up", i, lambda w, l: matmul(hf, w, layer=l))
            xp = jnp.concatenate([ffn_conv0[i], gu[:, :dff].reshape(b, t, dff)], axis=1)
            taps = [xp[:, FFN_CONV - 1 - s:FFN_CONV - 1 - s + t].reshape(m, dff) for s in range(FFN_CONV)]
            act = convgate(taps[0], taps[1], taps[2], gu[:, dff:], wts["ffn_conv_w"][i], wts["ffn_conv_b"][i])
            buf = xp[:, xp.shape[1] - (FFN_CONV - 1):]
        ffn_bufs.append(buf)
        x2 = _weight_matmul(wts, "ffn_w_down", i, lambda w, l: matmul(act, w, layer=l, res=x2))
        hp = rmsnorm(x2, wts["ple_norm"][i], BF16)
        p_in = p_emb[i].reshape(m, -1).astype(BF16)
        emb = _weight_matmul(wts, "ple_w_proj", i, lambda w, l: matmul(p_in, w, layer=l))
        x2 = _weight_matmul(wts, "ple_w_gate", i, lambda w, l: matmul(hp, w, layer=l, res=x2, aux=emb))
    y = rmsnorm(x2, wts["final_norm"], F32).reshape(b, t, d)
    return y, attn_new, lru_buf, h_last.reshape(b, -1), jnp.stack(ffn_bufs)


def kernel(x_prompt, x_sample, cache_fox_k, cache_fox_v, cache_fox_lf, cache_cmp_k, cache_cmp_v, cache_slc_k, cache_slc_v, state_win_k, state_win_v, state_lru_conv, state_lru_h, state_ffn_conv, page_table, p_prompt, p_sample, norm_mix, norm_ffn, final_norm, w_in0, b_forget, cmp_w1_k, cmp_pe_k, cmp_w2_k, cmp_w1_v, cmp_pe_v, cmp_w2_v, rel_bias, w_out0, w_in1, lru_conv_w, lru_conv_b, lru_w_a, lru_b_a, lru_w_x, lru_b_x, lru_lambda, w_out1, ffn_w_up, ffn_conv_w, ffn_conv_b, ffn_w_down, ple_w_proj, ple_w_gate, ple_norm):
    depth = norm_mix.shape[0]
    assert depth == 2
    d = x_prompt.shape[-1]
    dff = ffn_w_down.shape[1]
    wts = {
        "norm_mix": norm_mix, "norm_ffn": norm_ffn, "ple_norm": ple_norm, "final_norm": final_norm,
        "w_in0": _pack_w_in0(w_in0), "b_forget": b_forget, "rel_bias": rel_bias,
        "win_len": state_win_k.shape[1],
        "w_out0": w_out0, "w_in1": w_in1, "w_out1": w_out1,
        "lru_conv_w": lru_conv_w, "lru_conv_b": lru_conv_b, "lru_w_a": lru_w_a, "lru_b_a": lru_b_a,
        "lru_w_x": lru_w_x, "lru_b_x": lru_b_x, "lru_lambda": lru_lambda,
        "ffn_w_up": ffn_w_up, "ffn_conv_w": ffn_conv_w, "ffn_conv_b": ffn_conv_b,
        "ffn_w_down": ffn_w_down, "ple_w_proj": ple_w_proj, "ple_w_gate": ple_w_gate,
        "bf16": {},
    }
    wts["cmp_k"] = (cmp_w1_k, cmp_pe_k, cmp_w2_k)
    wts["cmp_v"] = (cmp_w1_v, cmp_pe_v, cmp_w2_v)

    caches = (cache_fox_k, cache_fox_v, cache_fox_lf, cache_cmp_k, cache_cmp_v, cache_slc_k, cache_slc_v)
    y_s, attn_s, lru_conv_s, lru_h_s, ffn_conv_s = trunk(
        x_sample, p_sample, wts, state_lru_conv, state_lru_h, state_ffn_conv,
        past=(caches, page_table, state_win_k, state_win_v))
    bp = x_prompt.shape[0]
    y_p, attn_p, lru_conv_p, lru_h_p, ffn_conv_p = trunk(
        x_prompt, p_prompt, wts,
        jnp.zeros((bp, LRU_CONV - 1, d), F32), jnp.zeros((bp, d), F32),
        jnp.zeros((depth, bp, FFN_CONV - 1, dff), F32))

    def shape_attn(new, b, t):
        fk, fv, lf, kc, vc, ks, vs, wk, wv = new
        h4 = lambda a: a.reshape(b, a.shape[1], FOX_HEADS, HEAD_DIM)
        g4 = lambda a: a.reshape(b, a.shape[1], NSA_KV_HEADS, HEAD_DIM)
        return h4(fk), h4(fv), lf, g4(kc), g4(vc), g4(ks), g4(vs), g4(wk), g4(wv)

    ap = shape_attn(attn_p, bp, x_prompt.shape[1])
    asmp = shape_attn(attn_s, x_sample.shape[0], x_sample.shape[1])
    out = [y_p, y_s]
    for a, s in zip(ap, asmp):
        out += [a, s]
    out += [lru_conv_p, lru_conv_s, lru_h_p, lru_h_s, ffn_conv_p, ffn_conv_s]
    return tuple(out)
```

```python
import functools
import math

import numpy as np
import jax
import jax.numpy as jnp
from jax import lax
from jax.experimental import pallas as pl
from jax.experimental.pallas import tpu as pltpu

PAGE_SIZE = 128
HEAD_DIM = 128
FOX_HEADS = 16
NSA_HEADS = 16
NSA_KV_HEADS = 2
NSA_GROUP = NSA_HEADS // NSA_KV_HEADS
CMP_BLOCK = 32
SEL_BLOCK = 64
N_SEL = 16
WINDOW = 512
SEL_FORCE = 1000.0
N_BUCKETS = 32
MAX_DISTANCE = 1024
LRU_HEADS = 16
LRU_C = 8.0
LRU_CONV = 4
FFN_CONV = 3
EPS = 1e-6
NEG_INF = -1e30
VISIBLE_MIN = -5e29

FOX_W = FOX_HEADS * HEAD_DIM
NSA_W = NSA_HEADS * HEAD_DIM
NSA_KV_W = NSA_KV_HEADS * HEAD_DIM
GROUP_W = NSA_GROUP * HEAD_DIM
SCALE = HEAD_DIM ** -0.5

COL_FQ, COL_FK, COL_FV = 0, FOX_W, 2 * FOX_W
COL_NQ = 3 * FOX_W
COL_KV = COL_NQ + NSA_W
COL_SMALL = COL_KV + 6 * NSA_KV_W
L0_PACKED = -(-(COL_SMALL + 128) // 1024) * 1024

F32 = jnp.float32
BF16 = jnp.bfloat16

VMEM_LIMIT_BYTES = 56 * 1024 * 1024


def _params(*sem):
    return pltpu.CompilerParams(dimension_semantics=sem, vmem_limit_bytes=VMEM_LIMIT_BYTES)


def _dot(a, b):
    return jnp.dot(a, b, preferred_element_type=F32)


def _dot_nt(a, b):
    return lax.dot_general(a, b, (((1,), (1,)), ((), ())), preferred_element_type=F32)


def _gelu(x):
    c = math.sqrt(2.0 / math.pi)
    return 0.5 * x * (1.0 + jnp.tanh(c * (x + 0.044715 * (x * x * x))))


def _sigmoid(x):
    return 1.0 / (1.0 + jnp.exp(-x))


def _online_softmax_step(s, v, m_sc, l_sc, acc_sc):
    reps = s.shape[1] // 128
    m_prev = m_sc[...]
    m_new = jnp.maximum(m_prev, jnp.max(s, axis=-1, keepdims=True))
    alpha = jnp.exp(m_prev - m_new)
    m_wide = m_new if reps == 1 else jnp.concatenate([m_new] * reps, axis=1)
    p = jnp.exp(s - m_wide)
    l_sc[...] = alpha * l_sc[...] + jnp.sum(p, axis=-1, keepdims=True)
    acc_sc[...] = alpha * acc_sc[...] + _dot(p.astype(BF16), v)
    m_sc[...] = m_new


def _bucket_np(dist):
    n = np.maximum(np.asarray(dist, np.int64), 0)
    exact = N_BUCKETS // 2
    nf = np.maximum(n, 1).astype(np.float64)
    large = exact + (np.log(nf / exact) / math.log(MAX_DISTANCE / exact) * (N_BUCKETS - exact)).astype(np.int64)
    return np.where(n < exact, n, np.minimum(large, N_BUCKETS - 1)).astype(np.int32)


def _rmsnorm_kernel(x_ref, g_ref, o_ref):
    x = x_ref[...]
    y = x * lax.rsqrt(jnp.mean(x * x, axis=-1, keepdims=True) + EPS)
    o_ref[...] = (y * g_ref[...]).astype(o_ref.dtype)


def rmsnorm(x, g, out_dtype):
    m, d = x.shape
    tm = min(m, 256)
    return pl.pallas_call(
        _rmsnorm_kernel,
        out_shape=jax.ShapeDtypeStruct((m, d), out_dtype),
        grid=(m // tm,),
        in_specs=[pl.BlockSpec((tm, d), lambda i: (i, 0)), pl.BlockSpec((1, d), lambda i: (0, 0))],
        out_specs=pl.BlockSpec((tm, d), lambda i: (i, 0)),
        compiler_params=_params("parallel"),
        name="rmsnorm",
    )(x, g.reshape(1, d))


def _mm_kernel(*refs, nk, epilogue, emit):
    a_ref, w_ref = refs[0], refs[1]
    n_in = 2 + (epilogue is not None) + (epilogue == "ple")
    o_ref = refs[n_in]
    k = pl.program_id(2)

    @pl.when(k == 0)
    def _():
        o_ref[...] = jnp.zeros_like(o_ref)

    if emit:
        refs[n_in + 1][...] = w_ref[...].astype(BF16)
        o_ref[...] += _dot(a_ref[...], refs[n_in + 1][...])
    else:
        o_ref[...] += _dot(a_ref[...], w_ref[...])

    if epilogue is not None:
        @pl.when(k == nk - 1)
        def _():
            acc = o_ref[...]
            if epilogue == "res":
                o_ref[...] = refs[2][...] + acc
            else:
                o_ref[...] = refs[2][...] + _sigmoid(acc) * refs[3][...]


def matmul(a, w, *, layer=None, res=None, aux=None, tm=1024, tn=1024, tk=2048):
    m, kdim = a.shape
    assert w.shape[-2] == kdim and (layer is None) == (w.ndim == 2)
    n = w.shape[-1]
    tm, tn, tk = min(tm, m), min(tn, n), min(tk, kdim)
    assert m % tm == 0 and n % tn == 0 and kdim % tk == 0
    nk = kdim // tk
    emit = w.dtype != BF16
    assert not emit or m == tm
    epilogue = None if res is None else ("res" if aux is None else "ple")
    if layer is None:
        w_spec = pl.BlockSpec((tk, tn), lambda i, j, k: (k, j))
    else:
        w_spec = pl.BlockSpec((None, tk, tn), lambda i, j, k: (layer, k, j))
    in_specs = [pl.BlockSpec((tm, tk), lambda i, j, k: (i, k)), w_spec]
    args = [a, w]
    for extra in (res, aux):
        if extra is not None:
            in_specs.append(pl.BlockSpec((tm, tn), lambda i, j, k: (i, j)))
            args.append(extra)
    out_shape = jax.ShapeDtypeStruct((m, n), F32)
    out_specs = pl.BlockSpec((tm, tn), lambda i, j, k: (i, j))
    if emit:
        out_shape = (out_shape, jax.ShapeDtypeStruct((kdim, n), BF16))
        out_specs = (out_specs, pl.BlockSpec((tk, tn), lambda i, j, k: (k, j)))
    return pl.pallas_call(
        functools.partial(_mm_kernel, nk=nk, epilogue=epilogue, emit=emit),
        out_shape=out_shape,
        grid=(m // tm, n // tn, nk),
        in_specs=in_specs,
        out_specs=out_specs,
        compiler_params=_params("parallel", "parallel", "arbitrary"),
        name="matmul_" + (epilogue or "plain") + ("_cast" if emit else ""),
    )(*args)


def _mm_pair_kernel(a0_ref, a1_ref, w_ref, res_ref, o_ref, *wb_ref):
    k = pl.program_id(2)
    if wb_ref:
        wb_ref[0][...] = w_ref[...].astype(BF16)
        w_ref = wb_ref[0]

    @pl.when(k == 0)
    def _():
        o_ref[...] = res_ref[...] + _dot(a0_ref[...], w_ref[...])

    @pl.when(k == 1)
    def _():
        o_ref[...] += _dot(a1_ref[...], w_ref[...])


def matmul_pair(a0, a1, w, res, *, tm=1024, tn=1024):
    m, k0 = a0.shape
    assert a1.shape == (m, k0) and w.shape[0] == 2 * k0
    n = w.shape[1]
    tm, tn = min(tm, m), min(tn, n)
    assert m % tm == 0 and n % tn == 0
    emit = w.dtype != BF16
    assert not emit or m == tm
    piece = pl.BlockSpec((tm, k0), lambda i, j, k: (i, 0))
    tile = pl.BlockSpec((tm, tn), lambda i, j, k: (i, j))
    w_tile = pl.BlockSpec((k0, tn), lambda i, j, k: (k, j))
    out_shape, out_specs = jax.ShapeDtypeStruct((m, n), F32), tile
    if emit:
        out_shape, out_specs = (out_shape, jax.ShapeDtypeStruct(w.shape, BF16)), (tile, w_tile)
    return pl.pallas_call(
        _mm_pair_kernel,
        out_shape=out_shape,
        grid=(m // tm, n // tn, 2),
        in_specs=[piece, piece, w_tile, tile],
        out_specs=out_specs,
        compiler_params=_params("parallel", "parallel", "arbitrary"),
        name="matmul_pair" + ("_cast" if emit else ""),
    )(a0, a1, w, res)


def _ffn_up_kernel(a_ref, wg_ref, wu_ref, cw_ref, cb_ref, buf_ref, act_ref, tail_ref,
                   accg, accu, gbuf, *, nk, tiles_per_seq, tm):
    i = pl.program_id(1)
    k = pl.program_id(2)
    kw = FFN_CONV

    @pl.when(k == 0)
    def _():
        accg[...] = jnp.zeros_like(accg)
        accu[...] = jnp.zeros_like(accu)

    a = a_ref[...]
    accg[...] += _dot(a, wg_ref[...])
    accu[...] += _dot(a, wu_ref[...])

    @pl.when(k == nk - 1)
    def _():
        @pl.when(i % tiles_per_seq == 0)
        def _():
            gbuf[8 - (kw - 1):8, :] = buf_ref[0]

        g = accg[...]
        gbuf[8:8 + tm, :] = g
        y = g * cw_ref[kw - 1:kw, :] + cb_ref[...]
        for t in range(kw - 1):
            off = 8 - (kw - 1) + t
            y = y + gbuf[off:off + tm, :] * cw_ref[t:t + 1, :]
        act_ref[...] = (_gelu(y) * accu[...]).astype(act_ref.dtype)
        tail = gbuf[8 + tm - (kw - 1):8 + tm, :]
        gbuf[8 - (kw - 1):8, :] = tail
        tail_ref[0] = tail


def ffn_up_fused(a, w_up, conv_w, conv_b, conv_buf, seq_len, *, tm=1024, tn=1024, tk=2048):
    m, kdim = a.shape
    dff = w_up.shape[1] // 2
    tm, tn, tk = min(tm, seq_len), min(tn, dff), min(tk, kdim)
    assert seq_len % tm == 0 and dff % tn == 0 and kdim % tk == 0 and tm % 8 == 0
    nb = m // seq_len
    tps = seq_len // tm
    nj, nk = dff // tn, kdim // tk
    kw = FFN_CONV
    kern = functools.partial(_ffn_up_kernel, nk=nk, tiles_per_seq=tps, tm=tm)
    return pl.pallas_call(
        kern,
        out_shape=(jax.ShapeDtypeStruct((m, dff), BF16), jax.ShapeDtypeStruct((nb, kw - 1, dff), F32)),
        grid=(nj, m // tm, nk),
        in_specs=[
            pl.BlockSpec((tm, tk), lambda j, i, k: (i, k)),
            pl.BlockSpec((tk, tn), lambda j, i, k: (k, j)),
            pl.BlockSpec((tk, tn), lambda j, i, k: (k, j + nj)),
            pl.BlockSpec((kw, tn), lambda j, i, k: (0, j)),
            pl.BlockSpec((1, tn), lambda j, i, k: (0, j)),
            pl.BlockSpec((1, kw - 1, tn), lambda j, i, k: (i // tps, 0, j)),
        ],
        out_specs=(
            pl.BlockSpec((tm, tn), lambda j, i, k: (i, j)),
            pl.BlockSpec((1, kw - 1, tn), lambda j, i, k: (i // tps, 0, j)),
        ),
        scratch_shapes=[pltpu.VMEM((tm, tn), F32), pltpu.VMEM((tm, tn), F32), pltpu.VMEM((8 + tm, tn), F32)],
        compiler_params=_params("parallel", "arbitrary", "arbitrary"),
        name="ffn_up_fused",
    )(a, w_up, w_up, conv_w, conv_b.reshape(1, dff), conv_buf)


def _convgate_kernel(s0_ref, s1_ref, s2_ref, u_ref, cw_ref, cb_ref, o_ref):
    y = s0_ref[...] * cw_ref[2:3, :] + cb_ref[...]
    y = y + s2_ref[...] * cw_ref[0:1, :]
    y = y + s1_ref[...] * cw_ref[1:2, :]
    o_ref[...] = (_gelu(y) * u_ref[...]).astype(o_ref.dtype)


def convgate(s0, s1, s2, u, conv_w, conv_b, *, tn=2048):
    m, dff = s0.shape
    tn = min(tn, dff)
    row = pl.BlockSpec((m, tn), lambda j: (0, j))
    return pl.pallas_call(
        _convgate_kernel,
        out_shape=jax.ShapeDtypeStruct((m, dff), BF16),
        grid=(dff // tn,),
        in_specs=[row, row, row, row, pl.BlockSpec((FFN_CONV, tn), lambda j: (0, j)),
                  pl.BlockSpec((1, tn), lambda j: (0, j))],
        out_specs=row,
        compiler_params=_params("parallel"),
        name="convgate",
    )(s0, s1, s2, u, conv_w, conv_b.reshape(1, dff))


def _fox_prompt_kernel(q_ref, k_ref, v_ref, cq_ref, ck_ref, o_ref, m_sc, l_sc, acc_sc, cq_sc, *, nk, tq, tk):
    qi = pl.program_id(2)
    kj = pl.program_id(3)

    @pl.when(kj == 0)
    def _():
        m_sc[...] = jnp.full_like(m_sc, NEG_INF)
        l_sc[...] = jnp.zeros_like(l_sc)
        acc_sc[...] = jnp.zeros_like(acc_sc)
        cq_sc[...] = jnp.transpose(jnp.broadcast_to(cq_ref[0, 0], (128, tq)))

    @pl.when(kj * tk <= qi * tq + tq - 1)
    def _():
        q = q_ref[0].astype(BF16)
        k = k_ref[0].astype(BF16)
        cq = jnp.concatenate([cq_sc[...]] * (tk // 128), axis=1)
        s = _dot_nt(q, k) * SCALE + (cq - ck_ref[0, 0])
        row = qi * tq + lax.broadcasted_iota(jnp.int32, (tq, tk), 0)
        col = kj * tk + lax.broadcasted_iota(jnp.int32, (tq, tk), 1)
        s = jnp.where(row >= col, s, NEG_INF)
        _online_softmax_step(s, v_ref[0].astype(BF16), m_sc, l_sc, acc_sc)

    @pl.when(kj == nk - 1)
    def _():
        o_ref[0] = (acc_sc[...] / jnp.maximum(l_sc[...], 1e-30)).astype(o_ref.dtype)


def fox_prompt(proj3, c_new):
    b, t, _ = proj3.shape
    tq = min(512, t)
    tk = min(1024, t)
    assert tq % 128 == 0 and tk % 128 == 0
    nq, nk = t // tq, t // tk
    h = FOX_HEADS
    c_row = c_new.transpose(0, 2, 1).reshape(b, h, 1, t)
    kq, kk, kv = COL_FQ // HEAD_DIM, COL_FK // HEAD_DIM, COL_FV // HEAD_DIM

    def kidx(qi, kj):
        return jnp.minimum(kj, (qi * tq + tq - 1) // tk)

    kern = functools.partial(_fox_prompt_kernel, nk=nk, tq=tq, tk=tk)
    return pl.pallas_call(
        kern,
        out_shape=jax.ShapeDtypeStruct((b, t, FOX_W), BF16),
        grid=(b, h, nq, nk),
        in_specs=[
            pl.BlockSpec((1, tq, HEAD_DIM), lambda bi, hi, qi, kj: (bi, qi, kq + hi)),
            pl.BlockSpec((1, tk, HEAD_DIM), lambda bi, hi, qi, kj: (bi, kidx(qi, kj), kk + hi)),
            pl.BlockSpec((1, tk, HEAD_DIM), lambda bi, hi, qi, kj: (bi, kidx(qi, kj), kv + hi)),
            pl.BlockSpec((1, 1, 1, tq), lambda bi, hi, qi, kj: (bi, hi, 0, qi)),
            pl.BlockSpec((1, 1, 1, tk), lambda bi, hi, qi, kj: (bi, hi, 0, kidx(qi, kj))),
        ],
        out_specs=pl.BlockSpec((1, tq, HEAD_DIM), lambda bi, hi, qi, kj: (bi, qi, hi)),
        scratch_shapes=[pltpu.VMEM((tq, 128), F32), pltpu.VMEM((tq, 128), F32), pltpu.VMEM((tq, HEAD_DIM), F32),
                        pltpu.VMEM((tq, 128), F32)],
        compiler_params=_params("parallel", "parallel", "parallel", "arbitrary"),
        name="fox_prompt",
    )(proj3, proj3, proj3, c_row, c_row)


def _compress_kernel(x_ref, pe_ref, w1_ref, w2_ref, o_ref, *, tr):
    acc = jnp.zeros((tr, HEAD_DIM), F32)
    for l in range(CMP_BLOCK):
        xl = x_ref[pl.ds(l, tr, stride=CMP_BLOCK), :] + pe_ref[l:l + 1, :]
        acc = acc + _dot(xl.astype(BF16), w1_ref[l])
    o_ref[0] = _dot(_gelu(acc).astype(BF16), w2_ref[...])


def compress(x2d, col_block, n_groups, cmp_w):
    w1, pe, w2 = cmp_w
    rows = x2d.shape[0]
    r = rows // CMP_BLOCK
    tr = min(r, 256)
    assert r % tr == 0 and rows % CMP_BLOCK == 0
    return pl.pallas_call(
        functools.partial(_compress_kernel, tr=tr),
        out_shape=jax.ShapeDtypeStruct((n_groups, r, HEAD_DIM), F32),
        grid=(n_groups, r // tr),
        in_specs=[pl.BlockSpec((tr * CMP_BLOCK, HEAD_DIM), lambda g, i: (i, col_block + g)),
                  pl.BlockSpec((CMP_BLOCK, HEAD_DIM), lambda g, i: (0, 0)),
                  pl.BlockSpec((CMP_BLOCK, HEAD_DIM, HEAD_DIM), lambda g, i: (0, 0, 0)),
                  pl.BlockSpec((HEAD_DIM, HEAD_DIM), lambda g, i: (0, 0))],
        out_specs=pl.BlockSpec((1, tr, HEAD_DIM), lambda g, i: (g, i, 0)),
        compiler_params=_params("parallel", "parallel"),
        name="nsa_compress",
    )(x2d, pe, w1.astype(BF16), w2.astype(BF16))


def _compress_pool_kernel(x_ref, pe_ref, w1_ref, w2_ref, o_ref, *, tr):
    for g in range(x_ref.shape[1]):
        acc = jnp.zeros((tr, HEAD_DIM), F32)
        for l in range(0, CMP_BLOCK, 2):
            xa = x_ref[pl.ds(l, tr, stride=CMP_BLOCK), g, :] + pe_ref[l:l + 1, :]
            xb = x_ref[pl.ds(l + 1, tr, stride=CMP_BLOCK), g, :] + pe_ref[l + 1:l + 2, :]
            acc = acc + _dot(jnp.concatenate([xa, xb], axis=1).astype(BF16), w1_ref[l // 2])
        o_ref[:, g * HEAD_DIM:(g + 1) * HEAD_DIM] = _dot(_gelu(acc).astype(BF16), w2_ref[...])


def compress_pool(pool, cmp_w):
    w1, pe, w2 = cmp_w
    n_phys, page, g, d = pool.shape
    r = n_phys * page // CMP_BLOCK
    tr = max(c for c in range(8, 257, 8) if r % c == 0)
    return pl.pallas_call(
        functools.partial(_compress_pool_kernel, tr=tr),
        out_shape=jax.ShapeDtypeStruct((r, g * d), F32),
        grid=(r // tr,),
        in_specs=[pl.BlockSpec((tr * CMP_BLOCK, g, d), lambda i: (i, 0, 0)),
                  pl.BlockSpec((CMP_BLOCK, d), lambda i: (0, 0)),
                  pl.BlockSpec((CMP_BLOCK // 2, 2 * d, d), lambda i: (0, 0, 0)),
                  pl.BlockSpec((d, d), lambda i: (0, 0))],
        out_specs=pl.BlockSpec((tr, g * d), lambda i: (i, 0)),
        compiler_params=_params("parallel"),
        name="nsa_compress_pool",
    )(pool.reshape(n_phys * page, g, d), pe, w1.astype(BF16).reshape(CMP_BLOCK // 2, 2 * d, d), w2.astype(BF16))


def _cmp_prompt_kernel(q_ref, ck_ref, cv_ref, bias_ref, exp_ref, o_ref, sel_ref, *, tq, nc, n_top):
    qi = pl.program_id(2)
    ns = nc // 2
    t_col = qi * tq + lax.broadcasted_iota(jnp.int32, (tq, nc), 0)
    lane = lax.broadcasted_iota(jnp.int32, (tq, nc), 1)
    blk_c = jnp.where(lane < ns, 2 * lane, 2 * (lane - ns) + 1)
    mask = t_col >= (blk_c + 1) * CMP_BLOCK - 1
    ck = ck_ref[0, 0].astype(BF16)
    cv = cv_ref[0, 0].astype(BF16)
    imp = jnp.zeros((tq, nc), F32)
    for hg in range(NSA_GROUP):
        sl = slice(hg * HEAD_DIM, (hg + 1) * HEAD_DIM)
        s = _dot_nt(q_ref[0, :, sl].astype(BF16), ck) * SCALE + bias_ref[hg]
        s = jnp.where(mask, s, NEG_INF)
        m = jnp.max(s, axis=-1, keepdims=True)
        e = jnp.where(mask, jnp.exp(s - m), 0.0)
        p = e / jnp.maximum(jnp.sum(e, axis=-1, keepdims=True), 1e-30)
        imp = imp + p
        o_ref[0, :, sl] = _dot(p.astype(BF16), cv)
    imp_s = imp[:, :ns] + imp[:, ns:]
    t_s = qi * tq + lax.broadcasted_iota(jnp.int32, (tq, ns), 0)
    blk = lax.broadcasted_iota(jnp.int32, (tq, ns), 1)
    cur = t_s // SEL_BLOCK
    valid = blk * SEL_BLOCK <= t_s
    forced = (blk == 0) | (blk == cur) | (blk == cur - 1)
    score = jnp.where(forced, SEL_FORCE, jnp.where(valid, imp_s, -1.0))
    rank = jnp.zeros((tq, ns), F32)
    for j in range(ns):
        cj = score[:, j:j + 1]
        ahead = (cj > score) | ((cj == score) & (blk > j))
        rank = rank + jnp.where(ahead, 1.0, 0.0)
    sel = jnp.where(rank < n_top, 1.0, 0.0).astype(BF16)
    sel_ref[0, 0] = (1.0 - _dot(sel, exp_ref[...])) * NEG_INF


def cmp_prompt(proj3, ck_perm, cv_perm, bias_cmp):
    b, t, _ = proj3.shape
    nc = t // CMP_BLOCK
    ns = t // SEL_BLOCK
    tq = min(256, t)
    g = NSA_KV_HEADS
    n_top = min(N_SEL, ns)
    expand = jnp.asarray(np.repeat(np.eye(ns, dtype=np.float32), SEL_BLOCK, axis=1), BF16)
    kq = COL_NQ // GROUP_W
    kern = functools.partial(_cmp_prompt_kernel, tq=tq, nc=nc, n_top=n_top)
    return pl.pallas_call(
        kern,
        out_shape=(jax.ShapeDtypeStruct((b, t, NSA_W), F32), jax.ShapeDtypeStruct((b, g, t, t), F32)),
        grid=(b, g, t // tq),
        in_specs=[
            pl.BlockSpec((1, tq, GROUP_W), lambda bi, gi, qi: (bi, qi, kq + gi)),
            pl.BlockSpec((1, 1, nc, HEAD_DIM), lambda bi, gi, qi: (gi, bi, 0, 0)),
            pl.BlockSpec((1, 1, nc, HEAD_DIM), lambda bi, gi, qi: (gi, bi, 0, 0)),
            pl.BlockSpec((NSA_GROUP, tq, nc), lambda bi, gi, qi: (gi, qi, 0)),
            pl.BlockSpec((ns, t), lambda bi, gi, qi: (0, 0)),
        ],
        out_specs=(
            pl.BlockSpec((1, tq, GROUP_W), lambda bi, gi, qi: (bi, qi, gi)),
            pl.BlockSpec((1, 1, tq, t), lambda bi, gi, qi: (bi, gi, qi, 0)),
        ),
        compiler_params=_params("parallel", "parallel", "parallel"),
        name="nsa_cmp_prompt",
    )(proj3, ck_perm, cv_perm, bias_cmp, expand)


NSA_TQ = 128
NSA_TK = 512


def _nsa_band_kernel(*refs, nr, ratio, o_max, mode, has_sel, combine):
    q_ref, k_ref, v_ref, bias_ref = refs[:4]
    o_ref, qs, m_sc, l_sc, acc_sc = refs[-5:]
    extra = list(refs[4:-5])
    sel_ref = extra.pop(0) if has_sel else None
    small_ref, oc_ref, os_ref = extra if combine else (None, None, None)
    qi = pl.program_id(2)
    r = pl.program_id(3)
    tq = NSA_TQ
    tk = k_ref.shape[1]
    hg_n = NSA_GROUP

    @pl.when(r == 0)
    def _():
        for hg in range(hg_n):
            qs[hg * tq:(hg + 1) * tq, :] = q_ref[0, :, hg * HEAD_DIM:(hg + 1) * HEAD_DIM].astype(BF16)
        m_sc[...] = jnp.full_like(m_sc, NEG_INF)
        l_sc[...] = jnp.zeros_like(l_sc)
        acc_sc[...] = jnp.zeros_like(acc_sc)

    if mode == "causal":
        active = r <= qi // ratio
        kj = jnp.minimum(r, qi // ratio)
    else:
        active = qi // ratio - (nr - 1) + r >= 0
        kj = jnp.maximum(qi // ratio - (nr - 1) + r, 0)
    chunk0 = o_max - jnp.clip(qi - ratio * kj, 0, o_max)

    @pl.when(active)
    def _():
        s = _dot_nt(qs[...], k_ref[0].astype(BF16)) * SCALE
        bias = jnp.concatenate([bias_ref[:, chunk0 + u] for u in range(ratio)], axis=-1)
        s3 = s.reshape(hg_n, tq, tk) + bias
        if has_sel:
            s3 = s3 + sel_ref[0, 0][None]
        _online_softmax_step(s3.reshape(hg_n * tq, tk), v_ref[0].astype(BF16), m_sc, l_sc, acc_sc)

    @pl.when(r == nr - 1)
    def _():
        o = acc_sc[...] / jnp.maximum(l_sc[...], 1e-30)
        if combine:
            gates = _sigmoid(small_ref[0])
            second_group = pl.program_id(1) == 1
        for hg in range(hg_n):
            sl = slice(hg * HEAD_DIM, (hg + 1) * HEAD_DIM)
            o_h = o[hg * tq:(hg + 1) * tq, :]
            if combine:
                def gate(branch):
                    c = FOX_HEADS + branch * NSA_HEADS + hg
                    return jnp.where(second_group, gates[:, c + hg_n:c + hg_n + 1], gates[:, c:c + 1])

                o_h = gate(0) * oc_ref[0, :, sl] + gate(1) * os_ref[0, :, sl] + gate(2) * o_h
            o_ref[0, :, sl] = o_h.astype(o_ref.dtype)


def nsa_band(proj3, bias_strip, tk, col_k, col_v, mode, selmask=None, combine_with=None):
    b, t, _ = proj3.shape
    tq = NSA_TQ
    assert t % tk == 0 and tk % tq == 0
    ratio = tk // tq
    nq, nk = t // tq, t // tk
    g = NSA_KV_HEADS
    n_chunks = bias_strip.shape[1]
    o_max = n_chunks - ratio
    nr = nk if mode == "causal" else -(-WINDOW // tk) + 1
    assert mode == "causal" or o_max == ratio * nr - 1
    kq = COL_NQ // GROUP_W
    ck, cv = col_k // HEAD_DIM, col_v // HEAD_DIM
    if mode == "causal":
        def ktile(qi, r):
            return jnp.minimum(r, qi // ratio)
    else:
        def ktile(qi, r):
            return jnp.maximum(qi // ratio - (nr - 1) + r, 0)

    in_specs = [
        pl.BlockSpec((1, tq, GROUP_W), lambda bi, gi, qi, r: (bi, qi, kq + gi)),
        pl.BlockSpec((1, tk, HEAD_DIM), lambda bi, gi, qi, r: (bi, ktile(qi, r), ck + gi)),
        pl.BlockSpec((1, tk, HEAD_DIM), lambda bi, gi, qi, r: (bi, ktile(qi, r), cv + gi)),
        pl.BlockSpec((NSA_GROUP, n_chunks, tq, 128), lambda bi, gi, qi, r: (gi, 0, 0, 0)),
    ]
    args = [proj3, proj3, proj3, bias_strip]
    if selmask is not None:
        in_specs.append(pl.BlockSpec((1, 1, tq, tk), lambda bi, gi, qi, r: (bi, gi, qi, ktile(qi, r))))
        args.append(selmask)
    head_tile = pl.BlockSpec((1, tq, GROUP_W), lambda bi, gi, qi, r: (bi, qi, gi))
    if combine_with is not None:
        assert g == 2
        in_specs += [pl.BlockSpec((1, tq, 128), lambda bi, gi, qi, r: (bi, qi, COL_SMALL // 128)), head_tile, head_tile]
        args += [proj3, combine_with[0], combine_with[1]]
    kern = functools.partial(_nsa_band_kernel, nr=nr, ratio=ratio, o_max=o_max, mode=mode,
                             has_sel=selmask is not None, combine=combine_with is not None)
    rows = NSA_GROUP * tq
    return pl.pallas_call(
        kern,
        out_shape=jax.ShapeDtypeStruct((b, t, NSA_W), F32 if combine_with is None else BF16),
        grid=(b, g, nq, nr),
        in_specs=in_specs,
        out_specs=pl.BlockSpec((1, tq, GROUP_W), lambda bi, gi, qi, r: (bi, qi, gi)),
        scratch_shapes=[pltpu.VMEM((rows, HEAD_DIM), BF16), pltpu.VMEM((rows, 128), F32),
                        pltpu.VMEM((rows, 128), F32), pltpu.VMEM((rows, HEAD_DIM), F32)],
        compiler_params=_params("parallel", "parallel", "parallel", "arbitrary"),
        name="nsa_band_" + mode,
    )(*args)


_T5_THRESHOLDS = tuple(int(np.searchsorted(_bucket_np(np.arange(2 * MAX_DISTANCE)), k)) for k in range(1, N_BUCKETS))


def _t5_bias(rel_bias, dist):
    per_head = lambda k: rel_bias[k].reshape((-1,) + (1,) * dist.ndim)
    d = dist[None]
    out = jnp.broadcast_to(per_head(0), (rel_bias.shape[1],) + dist.shape)
    for k, thr in enumerate(_T5_THRESHOLDS, start=1):
        out = jnp.where(d >= thr, per_head(k), out)
    return out


def _band_strip(rel_bias, o_max, ratio, window=None):
    c = jnp.arange(o_max + ratio, dtype=jnp.int32)[:, None, None]
    i = jnp.arange(NSA_TQ, dtype=jnp.int32)[None, :, None]
    jj = jnp.arange(128, dtype=jnp.int32)[None, None, :]
    d = i - (c * 128 + jj) + o_max * NSA_TQ
    ok = d >= 0
    if window is not None:
        ok = ok & (d < window)
    return jnp.where(ok[None], _t5_bias(rel_bias, d), NEG_INF)


def _combine_kernel(small_ref, oc_ref, os_ref, ow_ref, o_ref):
    gates = _sigmoid(small_ref[...])
    for h in range(NSA_HEADS):
        sl = slice(h * HEAD_DIM, (h + 1) * HEAD_DIM)
        base = FOX_HEADS + h
        g0 = gates[:, base:base + 1]
        g1 = gates[:, base + NSA_HEADS:base + NSA_HEADS + 1]
        g2 = gates[:, base + 2 * NSA_HEADS:base + 2 * NSA_HEADS + 1]
        o = g0 * oc_ref[:, sl] + g1 * os_ref[:, sl] + g2 * ow_ref[:, sl]
        o_ref[:, sl] = o.astype(o_ref.dtype)


def nsa_combine(proj, o_cmp, o_slc, o_win):
    m = proj.shape[0]
    tm = min(m, 256)
    wide = pl.BlockSpec((tm, NSA_W), lambda i: (i, 0))
    return pl.pallas_call(
        _combine_kernel,
        out_shape=jax.ShapeDtypeStruct((m, NSA_W), BF16),
        grid=(m // tm,),
        in_specs=[pl.BlockSpec((tm, 128), lambda i: (i, COL_SMALL // 128)), wide, wide, wide],
        out_specs=wide,
        compiler_params=_params("parallel"),
        name="nsa_combine",
    )(proj, o_cmp, o_slc, o_win)


def _lru_kernel(xr_ref, gate_ref, cw_ref, cb_ref, wa_ref, wx_ref, ba_ref, bx_ref, lam_ref, buf_ref, h0_ref,
                y_ref, hlast_ref, nbuf_ref, xbuf, hcar, hsbuf, *, nt, tt, last_row):
    ti = pl.program_id(2)
    kw = LRU_CONV

    @pl.when(ti == 0)
    def _():
        xbuf[8 - (kw - 1):8, :] = buf_ref[0]
        hcar[...] = h0_ref[0]

    x = xr_ref[0]
    xbuf[8:8 + tt, :] = x
    xc = x * cw_ref[kw - 1:kw, :] + cb_ref[...]
    for t in range(kw - 1):
        off = 8 - (kw - 1) + t
        xc = xc + xbuf[off:off + tt, :] * cw_ref[t:t + 1, :]
    xcb = xc.astype(BF16)
    r = _sigmoid(_dot(xcb, wa_ref[0]) + ba_ref[...])
    ig = _sigmoid(_dot(xcb, wx_ref[0]) + bx_ref[...])
    neg_lam = -lam_ref[...]
    softplus = jnp.maximum(neg_lam, 0.0) + jnp.log1p(jnp.exp(-jnp.abs(neg_lam)))
    log_a = -LRU_C * r * softplus
    a = jnp.exp(log_a)
    th = jnp.tanh(log_a)
    u = jnp.sqrt(-2.0 * th / (1.0 - th)) * (ig * xc)
    ng = tt // 8
    a = a.reshape(ng, 8, a.shape[-1])
    u = u.reshape(ng, 8, u.shape[-1])
    sub = lax.broadcasted_iota(jnp.int32, a.shape, 1)
    for step in (1, 2, 4):
        keep = sub >= step
        a_sh = jnp.where(keep, pltpu.roll(a, step, 1), 1.0)
        u_sh = jnp.where(keep, pltpu.roll(u, step, 1), 0.0)
        u = a * u_sh + u
        a = a * a_sh
    h = hcar[...]
    for gi in range(ng):
        hs_g = a[gi] * h + u[gi]
        hsbuf[gi * 8:(gi + 1) * 8, :] = hs_g
        h = hs_g[7:8, :]
    hcar[...] = h
    y_ref[0] = (_gelu(gate_ref[0]) * hsbuf[...]).astype(y_ref.dtype)

    @pl.when(ti == nt - 1)
    def _():
        hlast_ref[0] = hsbuf[last_row:last_row + 1, :]
        nbuf_ref[0] = xbuf[8 + last_row - (kw - 2):8 + last_row + 1, :]

    xbuf[8 - (kw - 1):8, :] = xbuf[8 + tt - (kw - 1):8 + tt, :]


def lru_block(proj1, conv_w, conv_b, w_a, b_a, w_x, b_x, lam, conv_buf, h0, t_valid):
    b, tp, w2 = proj1.shape
    w = w2 // 2
    nh = LRU_HEADS
    bw = w // nh
    tt = min(tp, 256)
    assert tp % tt == 0 and tt % 8 == 0
    nt = tp // tt
    last_row = (t_valid - 1) - (nt - 1) * tt
    assert 0 <= last_row < tt and (nt > 1 or last_row >= 0)
    kw = LRU_CONV
    vec = lambda a: a.reshape(1, w)
    chan = pl.BlockSpec((1, bw), lambda bi, hi, ti: (0, hi))
    kern = functools.partial(_lru_kernel, nt=nt, tt=tt, last_row=last_row)
    return pl.pallas_call(
        kern,
        out_shape=(jax.ShapeDtypeStruct((b, tp, w), BF16), jax.ShapeDtypeStruct((b, 1, w), F32),
                   jax.ShapeDtypeStruct((b, kw - 1, w), F32)),
        grid=(b, nh, nt),
        in_specs=[
            pl.BlockSpec((1, tt, bw), lambda bi, hi, ti: (bi, ti, nh + hi)),
            pl.BlockSpec((1, tt, bw), lambda bi, hi, ti: (bi, ti, hi)),
            pl.BlockSpec((kw, bw), lambda bi, hi, ti: (0, hi)),
            chan,
            pl.BlockSpec((1, bw, bw), lambda bi, hi, ti: (hi, 0, 0)),
            pl.BlockSpec((1, bw, bw), lambda bi, hi, ti: (hi, 0, 0)),
            chan, chan, chan,
            pl.BlockSpec((1, kw - 1, bw), lambda bi, hi, ti: (bi, 0, hi)),
            pl.BlockSpec((1, 1, bw), lambda bi, hi, ti: (bi, 0, hi)),
        ],
        out_specs=(
            pl.BlockSpec((1, tt, bw), lambda bi, hi, ti: (bi, ti, hi)),
            pl.BlockSpec((1, 1, bw), lambda bi, hi, ti: (bi, 0, hi)),
            pl.BlockSpec((1, kw - 1, bw), lambda bi, hi, ti: (bi, 0, hi)),
        ),
        scratch_shapes=[pltpu.VMEM((8 + tt, bw), F32), pltpu.VMEM((1, bw), F32), pltpu.VMEM((tt, bw), F32)],
        compiler_params=_params("parallel", "parallel", "arbitrary"),
        name="rglru",
    )(proj1, proj1, conv_w, vec(conv_b), w_a.astype(BF16), w_x.astype(BF16), vec(b_a), vec(b_x), vec(lam),
      conv_buf, h0.reshape(b, 1, w))


GATHER_PAGES = 8


def _gather_kernel(pt_ref, *refs, n_pools, pg):
    ins, outs = refs[:n_pools * pg], refs[n_pools * pg:]
    for c in range(n_pools):
        for i in range(pg):
            src = ins[c * pg + i]
            rows = src.shape[1]
            outs[c][0, i * rows:(i + 1) * rows, :] = src[0]


def gather_pools(pools, page_table):
    b, n_pages = page_table.shape
    pg = math.gcd(GATHER_PAGES, n_pages)
    in_specs, args, out_shape, out_specs = [], [], [], []
    for pool in pools:
        _, rows, c = pool.shape
        assert (pg * rows) % 8 == 0
        for i in range(pg):
            in_specs.append(pl.BlockSpec((1, rows, c), lambda bi, j, pt, i=i: (pt[bi, j * pg + i], 0, 0)))
            args.append(pool)
        out_shape.append(jax.ShapeDtypeStruct((b, n_pages * rows, c), pool.dtype))
        out_specs.append(pl.BlockSpec((1, pg * rows, c), lambda bi, j, pt: (bi, j, 0)))
    out_shape, out_specs = tuple(out_shape), tuple(out_specs)
    return pl.pallas_call(
        functools.partial(_gather_kernel, n_pools=len(pools), pg=pg),
        out_shape=out_shape,
        grid_spec=pltpu.PrefetchScalarGridSpec(
            num_scalar_prefetch=1, grid=(b, n_pages // pg), in_specs=in_specs, out_specs=out_specs),
        compiler_params=_params("parallel", "arbitrary"),
        name="gather_pages",
    )(page_table, *args)


FOX_DECODE_PAGES = 4


def _fox_decode_kernel(pt_ref, q_ref, cq_ref, ck_ref, cknew_ref, *refs, n_steps, pf):
    kp, vp = refs[:pf], refs[pf:2 * pf]
    knew_ref, vnew_ref, o_ref, m_sc, l_sc, acc_sc, cq_sc = refs[2 * pf:]
    j = pl.program_id(1)
    rows = q_ref.shape[1]
    nh = FOX_HEADS
    w = PAGE_SIZE * nh

    @pl.when(j == 0)
    def _():
        m_sc[...] = jnp.full_like(m_sc, NEG_INF)
        l_sc[...] = jnp.zeros_like(l_sc)
        acc_sc[...] = jnp.zeros_like(acc_sc)
        col = lax.broadcasted_iota(jnp.int32, (rows, w), 1)
        row = lax.broadcasted_iota(jnp.int32, (rows, w), 0)
        cq_sc[...] = jnp.where((col % nh) == (row % nh), cq_ref[0], NEG_INF)

    def update(k3, v3, ck_row, causal):
        n = k3.shape[0] * nh
        k2 = k3.reshape(n, HEAD_DIM).astype(BF16)
        v2 = v3.reshape(n, HEAD_DIM).astype(BF16)
        s = _dot_nt(q_ref[0], k2) * SCALE + (cq_sc[:, :n] - ck_row)
        if causal:
            col = lax.broadcasted_iota(jnp.int32, (rows, n), 1)
            row = lax.broadcasted_iota(jnp.int32, (rows, n), 0)
            s = jnp.where(col // nh <= row // nh, s, NEG_INF)
        _online_softmax_step(s, v2, m_sc, l_sc, acc_sc)

    @pl.when(j < n_steps)
    def _():
        for i in range(pf):
            update(kp[i][0], vp[i][0], ck_ref[0, :, i * w:(i + 1) * w], False)

    @pl.when(j == n_steps)
    def _():
        update(knew_ref[0], vnew_ref[0], cknew_ref[0], True)
        o_ref[0] = acc_sc[...] / jnp.maximum(l_sc[...], 1e-30)


def fox_decode(q_rows, cq_rows, ck_past, ck_new, k_pool, v_pool, k_new, v_new, page_table):
    b, n_pages = page_table.shape
    pf = math.gcd(FOX_DECODE_PAGES, n_pages)
    n_steps = n_pages // pf
    rows = q_rows.shape[1]
    tp = k_new.shape[1]
    nh = FOX_HEADS

    def page(i):
        return lambda bi, j, pt: (pt[bi, jnp.minimum(j * pf + i, n_pages - 1)], 0, 0, 0)

    pool_spec = [pl.BlockSpec((1, PAGE_SIZE, nh, HEAD_DIM), page(i)) for i in range(pf)]
    per_seq = lambda shape: pl.BlockSpec(shape, lambda bi, j, pt: (bi,) + (0,) * (len(shape) - 1))
    kern = functools.partial(_fox_decode_kernel, n_steps=n_steps, pf=pf)
    return pl.pallas_call(
        kern,
        out_shape=jax.ShapeDtypeStruct((b, rows, HEAD_DIM), F32),
        grid_spec=pltpu.PrefetchScalarGridSpec(
            num_scalar_prefetch=1,
            grid=(b, n_steps + 1),
            in_specs=[per_seq((1, rows, HEAD_DIM)), per_seq((1, rows, 1)),
                      pl.BlockSpec((1, 1, pf * PAGE_SIZE * nh), lambda bi, j, pt: (bi, 0, jnp.minimum(j, n_steps - 1))),
                      per_seq((1, 1, tp * nh))]
                     + pool_spec + pool_spec
                     + [per_seq((1, tp, nh, HEAD_DIM)), per_seq((1, tp, nh, HEAD_DIM))],
            out_specs=per_seq((1, rows, HEAD_DIM)),
            scratch_shapes=[pltpu.VMEM((rows, 128), F32), pltpu.VMEM((rows, 128), F32),
                            pltpu.VMEM((rows, HEAD_DIM), F32), pltpu.VMEM((rows, PAGE_SIZE * nh), F32)],
        ),
        compiler_params=_params("parallel", "arbitrary"),
        name="fox_decode",
    )(page_table, q_rows, cq_rows, ck_past, ck_new, *([k_pool] * pf), *([v_pool] * pf), k_new, v_new)


SLC_DECODE_PAGES = 16


def _slc_decode_kernel(pt_ref, q_ref, bias_ref, biasnew_ref, *refs, n_steps, pf):
    kp, vp = refs[:pf], refs[pf:2 * pf]
    knew_ref, vnew_ref, o_ref, m_sc, l_sc, acc_sc = refs[2 * pf:]
    j = pl.program_id(1)

    @pl.when(j == 0)
    def _():
        m_sc[...] = jnp.full_like(m_sc, NEG_INF)
        l_sc[...] = jnp.zeros_like(l_sc)
        acc_sc[...] = jnp.zeros_like(acc_sc)

    def update(k3, v3, bias):
        n = k3.shape[0] * k3.shape[1]
        k2 = k3.reshape(n, HEAD_DIM).astype(BF16)
        v2 = v3.reshape(n, HEAD_DIM).astype(BF16)
        s = _dot_nt(q_ref[0], k2) * SCALE + bias
        _online_softmax_step(s, v2, m_sc, l_sc, acc_sc)

    @pl.when(j < n_steps)
    def _():
        w = PAGE_SIZE * kp[0].shape[2]
        for i in range(pf):
            update(kp[i][0], vp[i][0], bias_ref[0, :, i * w:(i + 1) * w])

    @pl.when(j == n_steps)
    def _():
        update(knew_ref[0], vnew_ref[0], biasnew_ref[0])
        o_ref[0] = acc_sc[...] / jnp.maximum(l_sc[...], 1e-30)


def slc_decode(q_rows, bias_past, bias_new, k_pool, v_pool, k_new, v_new, page_table):
    b, n_pages = page_table.shape
    pf = math.gcd(SLC_DECODE_PAGES, n_pages)
    n_steps = n_pages // pf
    rows = q_rows.shape[1]
    tp, g = k_new.shape[1], k_new.shape[2]

    def page(i):
        return lambda bi, j, pt: (pt[bi, jnp.minimum(j * pf + i, n_pages - 1)], 0, 0, 0)

    pool_spec = [pl.BlockSpec((1, PAGE_SIZE, g, HEAD_DIM), page(i)) for i in range(pf)]
    per_seq = lambda shape: pl.BlockSpec(shape, lambda bi, j, pt: (bi,) + (0,) * (len(shape) - 1))
    return pl.pallas_call(
        functools.partial(_slc_decode_kernel, n_steps=n_steps, pf=pf),
        out_shape=jax.ShapeDtypeStruct((b, rows, HEAD_DIM), F32),
        grid_spec=pltpu.PrefetchScalarGridSpec(
            num_scalar_prefetch=1,
            grid=(b, n_steps + 1),
            in_specs=[per_seq((1, rows, HEAD_DIM)),
                      pl.BlockSpec((1, rows, pf * PAGE_SIZE * g), lambda bi, j, pt: (bi, 0, jnp.minimum(j, n_steps - 1))),
                      per_seq((1, rows, tp * g))]
                     + pool_spec + pool_spec
                     + [per_seq((1, tp, g, HEAD_DIM)), per_seq((1, tp, g, HEAD_DIM))],
            out_specs=per_seq((1, rows, HEAD_DIM)),
            scratch_shapes=[pltpu.VMEM((rows, 128), F32), pltpu.VMEM((rows, 128), F32),
                            pltpu.VMEM((rows, HEAD_DIM), F32)],
        ),
        compiler_params=_params("parallel", "arbitrary"),
        name="nsa_slc_decode",
    )(page_table, q_rows, bias_past, bias_new, *([k_pool] * pf), *([v_pool] * pf), k_new, v_new)


def _cmp_small_kernel(q_ref, k_ref, v_ref, bias_ref, o_ref, imp_ref, *, t_new):
    s = _dot_nt(q_ref[0, 0], k_ref[0].astype(BF16)) * SCALE + bias_ref[0]
    mask = s > VISIBLE_MIN
    m = jnp.max(s, axis=-1, keepdims=True)
    e = jnp.where(mask, jnp.exp(s - m), 0.0)
    p = e / jnp.maximum(jnp.sum(e, axis=-1, keepdims=True), 1e-30)
    o_ref[0, 0] = _dot(p.astype(BF16), v_ref[0].astype(BF16))
    imp_ref[0, 0] = jnp.sum(p.reshape(t_new, NSA_GROUP, p.shape[-1]), axis=1)


def cmp_small(q_rows, ck, cv, bias, t_new):
    b, g, rows, d = q_rows.shape
    s = ck.shape[1]
    return pl.pallas_call(
        functools.partial(_cmp_small_kernel, t_new=t_new),
        out_shape=(jax.ShapeDtypeStruct((b, g, rows, d), F32), jax.ShapeDtypeStruct((b, g, t_new, s), F32)),
        grid=(b, g),
        in_specs=[
            pl.BlockSpec((1, 1, rows, d), lambda bi, gi: (bi, gi, 0, 0)),
            pl.BlockSpec((1, s, d), lambda bi, gi: (bi, 0, gi)),
            pl.BlockSpec((1, s, d), lambda bi, gi: (bi, 0, gi)),
            pl.BlockSpec((1, rows, s), lambda bi, gi: (gi, 0, 0)),
        ],
        out_specs=(pl.BlockSpec((1, 1, rows, d), lambda bi, gi: (bi, gi, 0, 0)),
                   pl.BlockSpec((1, 1, t_new, s), lambda bi, gi: (bi, gi, 0, 0))),
        compiler_params=_params("parallel", "parallel"),
        name="nsa_cmp_sample",
    )(q_rows, ck, cv, bias)


def _flash_small_kernel(q_ref, k_ref, v_ref, bias_ref, o_ref, m_sc, l_sc, acc_sc, *, nk):
    j = pl.program_id(2)

    @pl.when(j == 0)
    def _():
        m_sc[...] = jnp.full_like(m_sc, NEG_INF)
        l_sc[...] = jnp.zeros_like(l_sc)
        acc_sc[...] = jnp.zeros_like(acc_sc)

    s = _dot_nt(q_ref[0, 0], k_ref[0, 0].astype(BF16)) * SCALE + bias_ref[0, 0]
    vis = s > VISIBLE_MIN
    m_prev = m_sc[...]
    m_new = jnp.maximum(m_prev, jnp.max(s, axis=-1, keepdims=True))
    alpha = jnp.exp(m_prev - m_new)
    p = jnp.where(vis, jnp.exp(s - m_new), 0.0)
    l_sc[...] = alpha * l_sc[...] + jnp.sum(p, axis=-1, keepdims=True)
    acc_sc[...] = alpha * acc_sc[...] + _dot(p.astype(BF16), v_ref[0, 0].astype(BF16))
    m_sc[...] = m_new

    @pl.when(j == nk - 1)
    def _():
        o_ref[0, 0] = acc_sc[...] / jnp.maximum(l_sc[...], 1e-30)


def flash_small(q_rows, k, v, bias, tk):
    b, g, rows, d = q_rows.shape
    s = k.shape[2]
    assert s % tk == 0
    nk = s // tk
    return pl.pallas_call(
        functools.partial(_flash_small_kernel, nk=nk),
        out_shape=jax.ShapeDtypeStruct((b, g, rows, d), F32),
        grid=(b, g, nk),
        in_specs=[
            pl.BlockSpec((1, 1, rows, d), lambda bi, gi, j: (bi, gi, 0, 0)),
            pl.BlockSpec((1, 1, tk, d), lambda bi, gi, j: (bi, gi, j, 0)),
            pl.BlockSpec((1, 1, tk, d), lambda bi, gi, j: (bi, gi, j, 0)),
            pl.BlockSpec((1, 1, rows, tk), lambda bi, gi, j: (bi, gi, 0, j)),
        ],
        out_specs=pl.BlockSpec((1, 1, rows, d), lambda bi, gi, j: (bi, gi, 0, 0)),
        scratch_shapes=[pltpu.VMEM((rows, 1), F32), pltpu.VMEM((rows, 1), F32), pltpu.VMEM((rows, d), F32)],
        compiler_params=_params("parallel", "parallel", "arbitrary"),
        name="nsa_flash_sample",
    )(q_rows, k, v, bias)


def _topk_kernel(col_ref, row_ref, o_ref, *, n_top):
    a = col_ref[...]
    bb = row_ref[...]
    shape = (a.shape[0], a.shape[1], bb.shape[2])
    j_idx = lax.broadcasted_iota(jnp.int32, shape, 1)
    i_idx = lax.broadcasted_iota(jnp.int32, shape, 2)
    ahead = (a > bb) | ((a == bb) & (j_idx < i_idx))
    rank = jnp.sum(jnp.where(ahead, 1.0, 0.0), axis=1, keepdims=True)
    o_ref[...] = jnp.where(rank < n_top, 1.0, 0.0)


def topk_mask(score, n_top):
    r, ns = score.shape
    tr = 8 if r % 8 == 0 else r
    out = pl.pallas_call(
        functools.partial(_topk_kernel, n_top=n_top),
        out_shape=jax.ShapeDtypeStruct((r, 1, ns), F32),
        grid=(r // tr,),
        in_specs=[pl.BlockSpec((tr, ns, 1), lambda i: (i, 0, 0)), pl.BlockSpec((tr, 1, ns), lambda i: (i, 0, 0))],
        out_specs=pl.BlockSpec((tr, 1, ns), lambda i: (i, 0, 0)),
        compiler_params=_params("parallel"),
        name="nsa_topk",
    )(score.reshape(r, ns, 1), score.reshape(r, 1, ns))
    return out.reshape(r, ns)


def _pack_w_in0(w_in0):
    d = w_in0.shape[0]
    off_fz = 3 * FOX_W
    off_nq = off_fz + FOX_HEADS
    off_kv = off_nq + NSA_W
    off_ng = off_kv + 6 * NSA_KV_W
    parts = [w_in0[:, :off_fz], w_in0[:, off_nq:off_ng], w_in0[:, off_fz:off_nq], w_in0[:, off_ng:]]
    used = sum(p.shape[1] for p in parts)
    parts.append(jnp.zeros((d, L0_PACKED - used), w_in0.dtype))
    return jnp.concatenate(parts, axis=1).astype(BF16)


def _forget_terms(proj3, b_forget):
    fz = proj3[:, :, COL_SMALL:COL_SMALL + FOX_HEADS]
    lf = jax.nn.log_sigmoid(fz + b_forget.astype(F32))
    return lf, jnp.cumsum(lf, axis=1)


def _kv_cols(proj3, idx):
    off = COL_KV + idx * NSA_KV_W
    return proj3[:, :, off:off + NSA_KV_W]


def attn_prompt(h_bf16, b, t, wts):
    m = b * t
    g = NSA_KV_HEADS
    proj = matmul(h_bf16, wts["w_in0"])
    proj3 = proj.reshape(b, t, L0_PACKED)
    lf, c_new = _forget_terms(proj3, wts["b_forget"])
    o_fox = fox_prompt(proj3, c_new)

    kc, vc, ks, vs, kw, vw = [_kv_cols(proj3, i) for i in range(6)]
    nc, ns = t // CMP_BLOCK, t // SEL_BLOCK
    tk = min(NSA_TK, t)
    assert t % (2 * CMP_BLOCK) == 0 and t % tk == 0
    ck = compress(proj, COL_KV // HEAD_DIM, g, wts["cmp_k"]).reshape(g, b, nc, HEAD_DIM)
    cv = compress(proj, (COL_KV + NSA_KV_W) // HEAD_DIM, g, wts["cmp_v"]).reshape(g, b, nc, HEAD_DIM)
    even_odd = lambda a: jnp.concatenate([a[:, :, 0::2], a[:, :, 1::2]], axis=2)
    rel = wts["rel_bias"].astype(F32)
    blk_c = jnp.concatenate([jnp.arange(0, nc, 2, dtype=jnp.int32), jnp.arange(1, nc, 2, dtype=jnp.int32)])
    c_dist = jnp.arange(t, dtype=jnp.int32)[:, None] - ((blk_c + 1) * CMP_BLOCK - 1)[None, :]
    bias_cmp = _t5_bias(rel, c_dist)
    o_cmp, selmask = cmp_prompt(proj3, even_odd(ck), even_odd(cv), bias_cmp)

    ratio = tk // NSA_TQ
    o_far = -(-(_T5_THRESHOLDS[-1] + tk - 1) // NSA_TQ)
    o_slc = nsa_band(proj3, _band_strip(rel, o_far, ratio), tk, COL_KV + 2 * NSA_KV_W, COL_KV + 3 * NSA_KV_W,
                     "causal", selmask)
    o_win_max = ratio * (-(-WINDOW // tk) + 1) - 1
    o_nsa = nsa_band(proj3, _band_strip(rel, o_win_max, ratio, window=WINDOW), tk,
                     COL_KV + 4 * NSA_KV_W, COL_KV + 5 * NSA_KV_W, "window", combine_with=(o_cmp, o_slc))
    mix = (o_fox.reshape(m, FOX_W), o_nsa.reshape(m, NSA_W))
    win_len = wts["win_len"]
    zeros = jnp.zeros((b, WINDOW, NSA_KV_W), F32)
    kw_a = jnp.concatenate([zeros, kw], axis=1)
    vw_a = jnp.concatenate([zeros, vw], axis=1)
    fk = proj3[:, :, COL_FK:COL_FK + FOX_W]
    fv = proj3[:, :, COL_FV:COL_FV + FOX_W]
    new = (fk, fv, lf, kc, vc, ks, vs, kw_a[:, kw_a.shape[1] - win_len:], vw_a[:, vw_a.shape[1] - win_len:])
    return mix, new


def attn_sample(h_bf16, b, t, wts, caches, page_table, state_win_k, state_win_v):
    m = b * t
    g, hg = NSA_KV_HEADS, NSA_GROUP
    n_pages = page_table.shape[1]
    p = n_pages * PAGE_SIZE
    assert t <= PAGE_SIZE and p % SEL_BLOCK == 0
    proj = matmul(h_bf16, wts["w_in0"])
    proj3 = proj.reshape(b, t, L0_PACKED)
    lf, c_new = _forget_terms(proj3, wts["b_forget"])
    kc, vc, ks, vs, kw, vw = [_kv_cols(proj3, i) for i in range(6)]
    fq = proj3[:, :, COL_FQ:COL_FQ + FOX_W]
    fk = proj3[:, :, COL_FK:COL_FK + FOX_W]
    fv = proj3[:, :, COL_FV:COL_FV + FOX_W]
    q_pos = p + np.arange(t)

    cache_fox_k, cache_fox_v, cache_fox_lf, cache_cmp_k, cache_cmp_v, cache_slc_k, cache_slc_v = caches
    assert PAGE_SIZE % CMP_BLOCK == 0
    per_page = PAGE_SIZE // CMP_BLOCK
    n_phys = cache_cmp_k.shape[0]
    plf, ck, cv = gather_pools(
        [cache_fox_lf,
         compress_pool(cache_cmp_k, wts["cmp_k"]).reshape(n_phys, per_page, NSA_KV_W),
         compress_pool(cache_cmp_v, wts["cmp_v"]).reshape(n_phys, per_page, NSA_KV_W)], page_table)

    c_past = plf - lax.cumsum(plf, axis=1, reverse=True)
    tp = -(-t // 8) * 8
    pad_new = lambda a: jnp.concatenate([a, jnp.zeros((b, tp - t) + a.shape[2:], F32)], axis=1)
    o_fox = fox_decode(fq.reshape(b, t * FOX_HEADS, HEAD_DIM).astype(BF16), c_new.reshape(b, t * FOX_HEADS, 1),
                       c_past.reshape(b, 1, p * FOX_HEADS), pad_new(c_new).reshape(b, 1, tp * FOX_HEADS),
                       cache_fox_k, cache_fox_v,
                       pad_new(fk.reshape(b, t, FOX_HEADS, HEAD_DIM)), pad_new(fv.reshape(b, t, FOX_HEADS, HEAD_DIM)),
                       page_table).reshape(m, FOX_W)

    rel = wts["rel_bias"].astype(F32)
    nq = proj3[:, :, COL_NQ:COL_NQ + NSA_W].reshape(b, t, g, hg, HEAD_DIM)
    q_rows = nq.transpose(0, 2, 1, 3, 4).reshape(b, g, t * hg, HEAD_DIM).astype(BF16)

    def head_rows(x):
        tt, ss = x.shape[1], x.shape[2]
        return x.reshape(g, hg, tt, ss).transpose(0, 2, 1, 3).reshape(g, tt * hg, ss)

    l_tot = p + t
    l_pad = -(-l_tot // SEL_BLOCK) * SEL_BLOCK
    nc, ns = l_pad // CMP_BLOCK, l_pad // SEL_BLOCK
    nc_past = p // CMP_BLOCK
    assert nc_past % 2 == 0 and (nc_past + 1) * CMP_BLOCK - 1 > q_pos[-1]
    even_odd = lambda a: jnp.concatenate([a[:, 0::2], a[:, 1::2]], axis=1)
    blk_c = jnp.concatenate([jnp.arange(0, nc_past, 2, dtype=jnp.int32), jnp.arange(1, nc_past, 2, dtype=jnp.int32)])
    c_dist = jnp.asarray(q_pos, jnp.int32)[:, None] - ((blk_c + 1) * CMP_BLOCK - 1)[None, :]
    bias_cmp = jnp.where((c_dist >= 0)[None], _t5_bias(rel, c_dist), NEG_INF)
    o_cmp, imp = cmp_small(q_rows, even_odd(ck), even_odd(cv), head_rows(bias_cmp), t)
    ns_past = nc_past // 2
    imp_s = imp[..., :ns_past] + imp[..., ns_past:]
    imp_s = jnp.concatenate([imp_s, jnp.zeros((b, g, t, ns - ns_past), F32)], axis=-1)
    blk = np.arange(ns)[None, :]
    cur = (q_pos // SEL_BLOCK)[:, None]
    valid = blk * SEL_BLOCK <= q_pos[:, None]
    forced = (blk == 0) | (blk == cur) | (blk == cur - 1)
    score = jnp.where(jnp.asarray(forced), SEL_FORCE, jnp.where(jnp.asarray(valid), imp_s, -1.0))
    sel = topk_mask(score.reshape(b * g * t, ns), min(N_SEL, ns)).reshape(b, g, t, ns)

    qp = jnp.asarray(q_pos, jnp.int32)[:, None]
    n_full = p // SEL_BLOCK
    t_pad = SEL_BLOCK
    assert ns - 1 == n_full and t <= t_pad
    own_head = (jnp.arange(g)[:, None] == jnp.arange(g)[None, :])[None, :, None, None, :]

    def slc_bias(dist, visible, sel_keys):
        s = dist.shape[1]
        t5 = head_rows(jnp.where(visible[None], _t5_bias(rel, dist), NEG_INF))
        chosen = jnp.broadcast_to(sel_keys[:, :, :, None, :], (b, g, t, hg, s)).reshape(b, g, t * hg, s)
        val = jnp.where(chosen > 0.5, t5[None], NEG_INF)
        return jnp.where(own_head, val[..., None], NEG_INF).reshape(b, g * t * hg, s * g)

    d_past = qp - jnp.arange(p, dtype=jnp.int32)[None, :]
    bias_past = slc_bias(d_past, d_past >= 0, jnp.repeat(sel[..., :n_full], SEL_BLOCK, axis=-1))
    d_new = qp - (p + jnp.arange(t_pad, dtype=jnp.int32))[None, :]
    new_ok = (d_new >= 0) & (jnp.arange(t_pad)[None, :] < t)
    bias_new = slc_bias(d_new, new_ok, jnp.broadcast_to(sel[..., ns - 1:ns], (b, g, t, t_pad)))
    pad_rows = lambda a: jnp.concatenate(
        [a.reshape(b, t, g, HEAD_DIM), jnp.zeros((b, t_pad - t, g, HEAD_DIM), F32)], axis=1)
    o_slc = slc_decode(q_rows.reshape(b, g * t * hg, HEAD_DIM), bias_past, bias_new, cache_slc_k, cache_slc_v,
                       pad_rows(ks), pad_rows(vs), page_table).reshape(b, g, t * hg, HEAD_DIM)

    def head_major(a, rows):
        a = a.reshape(b, t, g, HEAD_DIM).transpose(0, 2, 1, 3)
        return jnp.concatenate([a, jnp.zeros((b, g, rows - t, HEAD_DIM), F32)], axis=2)

    p_w = state_win_k.shape[1]
    s_win = -(-(p_w + t) // 128) * 128
    kw_a = jnp.concatenate([state_win_k.reshape(b, p_w, NSA_KV_W), kw], axis=1)
    vw_a = jnp.concatenate([state_win_v.reshape(b, p_w, NSA_KV_W), vw], axis=1)
    win_major = lambda st, new: jnp.concatenate([st.transpose(0, 2, 1, 3), head_major(new, s_win - p_w)], axis=2)
    w_idx = jnp.arange(s_win, dtype=jnp.int32)[None, :]
    wpos = (p - p_w) + w_idx
    wd = qp - wpos
    wok = (wd >= 0) & (wd < WINDOW) & (wpos >= 0) & (w_idx < p_w + t)
    bias_win = head_rows(jnp.where(wok[None], _t5_bias(rel, wd), NEG_INF))
    bias_win = jnp.broadcast_to(bias_win[None], (b,) + bias_win.shape)
    o_win = flash_small(q_rows, win_major(state_win_k, kw), win_major(state_win_v, vw), bias_win, s_win)

    def to_tokens(o):
        return o.reshape(b, g, t, hg, HEAD_DIM).transpose(0, 2, 1, 3, 4).reshape(m, NSA_W)

    o_nsa = nsa_combine(proj, to_tokens(o_cmp), to_tokens(o_slc), to_tokens(o_win))
    mix = (o_fox.astype(BF16), o_nsa)
    win_len = wts["win_len"]
    new = (fk, fv, lf, kc, vc, ks, vs, kw_a[:, kw_a.shape[1] - win_len:], vw_a[:, vw_a.shape[1] - win_len:])
    return mix, new


def _weight_matmul(wts, key, layer, call):
    cache = wts["bf16"]
    if (key, layer) in cache:
        return call(cache[(key, layer)], None)
    result = call(wts[key], layer)
    if isinstance(result, tuple):
        result, cache[(key, layer)] = result
    return result


def trunk(x, p_emb, wts, lru_conv0, lru_h0, ffn_conv0, past=None):
    b, t, d = x.shape
    m = b * t
    x2 = x.reshape(m, d)
    ffn_bufs = []
    for i in range(2):
        h = rmsnorm(x2, wts["norm_mix"][i], BF16)
        if i == 0:
            if past is None:
                mix, attn_new = attn_prompt(h, b, t, wts)
            else:
                mix, attn_new = attn_sample(h, b, t, wts, *past)
            x2 = _weight_matmul(wts, "w_out0", None, lambda w, _, r=x2: matmul_pair(mix[0], mix[1], w, r))
        else:
            proj1 = _weight_matmul(wts, "w_in1", None, lambda w, _: matmul(h, w)).reshape(b, t, -1)
            tp = -(-t // 8) * 8
            if tp != t:
                proj1 = jnp.concatenate([proj1, jnp.zeros((b, tp - t, proj1.shape[-1]), F32)], axis=1)
            gated, h_last, lru_buf = lru_block(proj1, wts["lru_conv_w"], wts["lru_conv_b"], wts["lru_w_a"],
                                               wts["lru_b_a"], wts["lru_w_x"], wts["lru_b_x"], wts["lru_lambda"],
                                               lru_conv0, lru_h0, t)
            g_in = gated[:, :t].reshape(m, -1)
            x2 = _weight_matmul(wts, "w_out1", None, lambda w, _: matmul(g_in, w, res=x2))
        hf = rmsnorm(x2, wts["norm_ffn"][i], BF16)
        dff = wts["ffn_w_up"].shape[2] // 2
        if t % 8 == 0:
            if ("ffn_w_up", i) not in wts["bf16"]:
                wts["bf16"][("ffn_w_up", i)] = wts["ffn_w_up"][i].astype(BF16)
            act, buf = ffn_up_fused(hf, wts["bf16"][("ffn_w_up", i)], wts["ffn_conv_w"][i], wts["ffn_conv_b"][i],
                                    ffn_conv0[i], t)
        else:
            gu = _weight_matmul(wts, "ffn_w_up", i, lambda w, l: matmul(hf, w, layer=l))
            xp = jnp.concatenate([ffn_conv0[i], gu[:, :dff].reshape(b, t, dff)], axis=1)
            taps = [xp[:, FFN_CONV - 1 - s:FFN_CONV - 1 - s + t].reshape(m, dff) for s in range(FFN_CONV)]
            act = convgate(taps[0], taps[1], taps[2], gu[:, dff:], wts["ffn_conv_w"][i], wts["ffn_conv_b"][i])
            buf = xp[:, xp.shape[1] - (FFN_CONV - 1):]
        ffn_bufs.append(buf)
        x2 = _weight_matmul(wts, "ffn_w_down", i, lambda w, l: matmul(act, w, layer=l, res=x2))
        hp = rmsnorm(x2, wts["ple_norm"][i], BF16)
        p_in = p_emb[i].reshape(m, -1).astype(BF16)
        emb = _weight_matmul(wts, "ple_w_proj", i, lambda w, l: matmul(p_in, w, layer=l))
        x2 = _weight_matmul(wts, "ple_w_gate", i, lambda w, l: matmul(hp, w, layer=l, res=x2, aux=emb))
    y = rmsnorm(x2, wts["final_norm"], F32).reshape(b, t, d)
    return y, attn_new, lru_buf, h_last.reshape(b, -1), jnp.stack(ffn_bufs)


def kernel(x_prompt, x_sample, cache_fox_k, cache_fox_v, cache_fox_lf, cache_cmp_k, cache_cmp_v, cache_slc_k, cache_slc_v, state_win_k, state_win_v, state_lru_conv, state_lru_h, state_ffn_conv, page_table, p_prompt, p_sample, norm_mix, norm_ffn, final_norm, w_in0, b_forget, cmp_w1_k, cmp_pe_k, cmp_w2_k, cmp_w1_v, cmp_pe_v, cmp_w2_v, rel_bias, w_out0, w_in1, lru_conv_w, lru_conv_b, lru_w_a, lru_b_a, lru_w_x, lru_b_x, lru_lambda, w_out1, ffn_w_up, ffn_conv_w, ffn_conv_b, ffn_w_down, ple_w_proj, ple_w_gate, ple_norm):
    depth = norm_mix.shape[0]
    assert depth == 2
    d = x_prompt.shape[-1]
    dff = ffn_w_down.shape[1]
    wts = {
        "norm_mix": norm_mix, "norm_ffn": norm_ffn, "ple_norm": ple_norm, "final_norm": final_norm,
        "w_in0": _pack_w_in0(w_in0), "b_forget": b_forget, "rel_bias": rel_bias,
        "win_len": state_win_k.shape[1],
        "w_out0": w_out0, "w_in1": w_in1, "w_out1": w_out1,
        "lru_conv_w": lru_conv_w, "lru_conv_b": lru_conv_b, "lru_w_a": lru_w_a, "lru_b_a": lru_b_a,
        "lru_w_x": lru_w_x, "lru_b_x": lru_b_x, "lru_lambda": lru_lambda,
        "ffn_w_up": ffn_w_up, "ffn_conv_w": ffn_conv_w, "ffn_conv_b": ffn_conv_b,
        "ffn_w_down": ffn_w_down, "ple_w_proj": ple_w_proj, "ple_w_gate": ple_w_gate,
        "bf16": {},
    }
    wts["cmp_k"] = (cmp_w1_k, cmp_pe_k, cmp_w2_k)
    wts["cmp_v"] = (cmp_w1_v, cmp_pe_v, cmp_w2_v)

    caches = (cache_fox_k, cache_fox_v, cache_fox_lf, cache_cmp_k, cache_cmp_v, cache_slc_k, cache_slc_v)
    y_s, attn_s, lru_conv_s, lru_h_s, ffn_conv_s = trunk(
        x_sample, p_sample, wts, state_lru_conv, state_lru_h, state_ffn_conv,
        past=(caches, page_table, state_win_k, state_win_v))
    bp = x_prompt.shape[0]
    y_p, attn_p, lru_conv_p, lru_h_p, ffn_conv_p = trunk(
        x_prompt, p_prompt, wts,
        jnp.zeros((bp, LRU_CONV - 1, d), F32), jnp.zeros((bp, d), F32),
        jnp.zeros((depth, bp, FFN_CONV - 1, dff), F32))

    def shape_attn(new, b, t):
        fk, fv, lf, kc, vc, ks, vs, wk, wv = new
        h4 = lambda a: a.reshape(b, a.shape[1], FOX_HEADS, HEAD_DIM)
        g4 = lambda a: a.reshape(b, a.shape[1], NSA_KV_HEADS, HEAD_DIM)
        return h4(fk), h4(fv), lf, g4(kc), g4(vc), g4(ks), g4(vs), g4(wk), g4(wv)

    ap = shape_attn(attn_p, bp, x_prompt.shape[1])
    asmp = shape_attn(attn_s, x_sample.shape[0], x_sample.shape[1])
    out = [y_p, y_s]
    for a, s in zip(ap, asmp):
        out += [a, s]
    out += [lru_conv_p, lru_conv_s, lru_h_p, lru_h_s, ffn_conv_p, ffn_conv_s]
    return tuple(out)
```

```python
import functools
import math

import numpy as np
import jax
import jax.numpy as jnp
from jax import lax
from jax.experimental import pallas as pl
from jax.experimental.pallas import tpu as pltpu

PAGE_SIZE = 128
HEAD_DIM = 128
FOX_HEADS = 16
NSA_HEADS = 16
NSA_KV_HEADS = 2
NSA_GROUP = NSA_HEADS // NSA_KV_HEADS
CMP_BLOCK = 32
SEL_BLOCK = 64
N_SEL = 16
WINDOW = 512
SEL_FORCE = 1000.0
N_BUCKETS = 32
MAX_DISTANCE = 1024
LRU_HEADS = 16
LRU_C = 8.0
LRU_CONV = 4
FFN_CONV = 3
EPS = 1e-6
NEG_INF = -1e30
VISIBLE_MIN = -5e29

FOX_W = FOX_HEADS * HEAD_DIM
NSA_W = NSA_HEADS * HEAD_DIM
NSA_KV_W = NSA_KV_HEADS * HEAD_DIM
GROUP_W = NSA_GROUP * HEAD_DIM
SCALE = HEAD_DIM ** -0.5

COL_FQ, COL_FK, COL_FV = 0, FOX_W, 2 * FOX_W
COL_NQ = 3 * FOX_W
COL_KV = COL_NQ + NSA_W
COL_SMALL = COL_KV + 6 * NSA_KV_W
L0_PACKED = -(-(COL_SMALL + 128) // 1024) * 1024

F32 = jnp.float32
BF16 = jnp.bfloat16

VMEM_LIMIT_BYTES = 56 * 1024 * 1024


def _params(*sem):
    return pltpu.CompilerParams(dimension_semantics=sem, vmem_limit_bytes=VMEM_LIMIT_BYTES)


def _dot(a, b):
    return jnp.dot(a, b, preferred_element_type=F32)


def _dot_nt(a, b):
    return lax.dot_general(a, b, (((1,), (1,)), ((), ())), preferred_element_type=F32)


def _gelu(x):
    c = math.sqrt(2.0 / math.pi)
    return 0.5 * x * (1.0 + jnp.tanh(c * (x + 0.044715 * (x * x * x))))


def _sigmoid(x):
    return 1.0 / (1.0 + jnp.exp(-x))


def _online_softmax_step(s, v, m_sc, l_sc, acc_sc):
    reps = s.shape[1] // 128
    m_prev = m_sc[...]
    m_new = jnp.maximum(m_prev, jnp.max(s, axis=-1, keepdims=True))
    alpha = jnp.exp(m_prev - m_new)
    m_wide = m_new if reps == 1 else jnp.concatenate([m_new] * reps, axis=1)
    p = jnp.exp(s - m_wide)
    l_sc[...] = alpha * l_sc[...] + jnp.sum(p, axis=-1, keepdims=True)
    acc_sc[...] = alpha * acc_sc[...] + _dot(p.astype(BF16), v)
    m_sc[...] = m_new


def _bucket_np(dist):
    n = np.maximum(np.asarray(dist, np.int64), 0)
    exact = N_BUCKETS // 2
    nf = np.maximum(n, 1).astype(np.float64)
    large = exact + (np.log(nf / exact) / math.log(MAX_DISTANCE / exact) * (N_BUCKETS - exact)).astype(np.int64)
    return np.where(n < exact, n, np.minimum(large, N_BUCKETS - 1)).astype(np.int32)


def _rmsnorm_kernel(x_ref, g_ref, o_ref):
    x = x_ref[...]
    y = x * lax.rsqrt(jnp.mean(x * x, axis=-1, keepdims=True) + EPS)
    o_ref[...] = (y * g_ref[...]).astype(o_ref.dtype)


def rmsnorm(x, g, out_dtype):
    m, d = x.shape
    tm = min(m, 256)
    return pl.pallas_call(
        _rmsnorm_kernel,
        out_shape=jax.ShapeDtypeStruct((m, d), out_dtype),
        grid=(m // tm,),
        in_specs=[pl.BlockSpec((tm, d), lambda i: (i, 0)), pl.BlockSpec((1, d), lambda i: (0, 0))],
        out_specs=pl.BlockSpec((tm, d), lambda i: (i, 0)),
        compiler_params=_params("parallel"),
        name="rmsnorm",
    )(x, g.reshape(1, d))


def _mm_kernel(*refs, nk, epilogue, emit):
    a_ref, w_ref = refs[0], refs[1]
    n_in = 2 + (epilogue is not None) + (epilogue == "ple")
    o_ref = refs[n_in]
    k = pl.program_id(2)

    @pl.when(k == 0)
    def _():
        o_ref[...] = jnp.zeros_like(o_ref)

    if emit:
        refs[n_in + 1][...] = w_ref[...].astype(BF16)
        o_ref[...] += _dot(a_ref[...], refs[n_in + 1][...])
    else:
        o_ref[...] += _dot(a_ref[...], w_ref[...])

    if epilogue is not None:
        @pl.when(k == nk - 1)
        def _():
            acc = o_ref[...]
            if epilogue == "res":
                o_ref[...] = refs[2][...] + acc
            else:
                o_ref[...] = refs[2][...] + _sigmoid(acc) * refs[3][...]


def matmul(a, w, *, layer=None, res=None, aux=None, tm=1024, tn=1024, tk=2048):
    m, kdim = a.shape
    assert w.shape[-2] == kdim and (layer is None) == (w.ndim == 2)
    n = w.shape[-1]
    tm, tn, tk = min(tm, m), min(tn, n), min(tk, kdim)
    assert m % tm == 0 and n % tn == 0 and kdim % tk == 0
    nk = kdim // tk
    emit = w.dtype != BF16
    assert not emit or m == tm
    epilogue = None if res is None else ("res" if aux is None else "ple")
    if layer is None:
        w_spec = pl.BlockSpec((tk, tn), lambda i, j, k: (k, j))
    else:
        w_spec = pl.BlockSpec((None, tk, tn), lambda i, j, k: (layer, k, j))
    in_specs = [pl.BlockSpec((tm, tk), lambda i, j, k: (i, k)), w_spec]
    args = [a, w]
    for extra in (res, aux):
        if extra is not None:
            in_specs.append(pl.BlockSpec((tm, tn), lambda i, j, k: (i, j)))
            args.append(extra)
    out_shape = jax.ShapeDtypeStruct((m, n), F32)
    out_specs = pl.BlockSpec((tm, tn), lambda i, j, k: (i, j))
    if emit:
        out_shape = (out_shape, jax.ShapeDtypeStruct((kdim, n), BF16))
        out_specs = (out_specs, pl.BlockSpec((tk, tn), lambda i, j, k: (k, j)))
    return pl.pallas_call(
        functools.partial(_mm_kernel, nk=nk, epilogue=epilogue, emit=emit),
        out_shape=out_shape,
        grid=(m // tm, n // tn, nk),
        in_specs=in_specs,
        out_specs=out_specs,
        compiler_params=_params("parallel", "parallel", "arbitrary"),
        name="matmul_" + (epilogue or "plain") + ("_cast" if emit else ""),
    )(*args)


def _mm_pair_kernel(a0_ref, a1_ref, w_ref, res_ref, o_ref, *wb_ref):
    k = pl.program_id(2)
    if wb_ref:
        wb_ref[0][...] = w_ref[...].astype(BF16)
        w_ref = wb_ref[0]

    @pl.when(k == 0)
    def _():
        o_ref[...] = res_ref[...] + _dot(a0_ref[...], w_ref[...])

    @pl.when(k == 1)
    def _():
        o_ref[...] += _dot(a1_ref[...], w_ref[...])


def matmul_pair(a0, a1, w, res, *, tm=1024, tn=1024):
    m, k0 = a0.shape
    assert a1.shape == (m, k0) and w.shape[0] == 2 * k0
    n = w.shape[1]
    tm, tn = min(tm, m), min(tn, n)
    assert m % tm == 0 and n % tn == 0
    emit = w.dtype != BF16
    assert not emit or m == tm
    piece = pl.BlockSpec((tm, k0), lambda i, j, k: (i, 0))
    tile = pl.BlockSpec((tm, tn), lambda i, j, k: (i, j))
    w_tile = pl.BlockSpec((k0, tn), lambda i, j, k: (k, j))
    out_shape, out_specs = jax.ShapeDtypeStruct((m, n), F32), tile
    if emit:
        out_shape, out_specs = (out_shape, jax.ShapeDtypeStruct(w.shape, BF16)), (tile, w_tile)
    return pl.pallas_call(
        _mm_pair_kernel,
        out_shape=out_shape,
        grid=(m // tm, n // tn, 2),
        in_specs=[piece, piece, w_tile, tile],
        out_specs=out_specs,
        compiler_params=_params("parallel", "parallel", "arbitrary"),
        name="matmul_pair" + ("_cast" if emit else ""),
    )(a0, a1, w, res)


def _ffn_up_kernel(a_ref, wg_ref, wu_ref, cw_ref, cb_ref, buf_ref, act_ref, tail_ref,
                   accg, accu, gbuf, *, nk, tiles_per_seq, tm):
    i = pl.program_id(1)
    k = pl.program_id(2)
    kw = FFN_CONV

    @pl.when(k == 0)
    def _():
        accg[...] = jnp.zeros_like(accg)
        accu[...] = jnp.zeros_like(accu)

    a = a_ref[...]
    accg[...] += _dot(a, wg_ref[...])
    accu[...] += _dot(a, wu_ref[...])

    @pl.when(k == nk - 1)
    def _():
        @pl.when(i % tiles_per_seq == 0)
        def _():
            gbuf[8 - (kw - 1):8, :] = buf_ref[0]

        g = accg[...]
        gbuf[8:8 + tm, :] = g
        y = g * cw_ref[kw - 1:kw, :] + cb_ref[...]
        for t in range(kw - 1):
            off = 8 - (kw - 1) + t
            y = y + gbuf[off:off + tm, :] * cw_ref[t:t + 1, :]
        act_ref[...] = (_gelu(y) * accu[...]).astype(act_ref.dtype)
        tail = gbuf[8 + tm - (kw - 1):8 + tm, :]
        gbuf[8 - (kw - 1):8, :] = tail
        tail_ref[0] = tail


def ffn_up_fused(a, w_up, conv_w, conv_b, conv_buf, seq_len, *, tm=1024, tn=1024, tk=2048):
    m, kdim = a.shape
    dff = w_up.shape[1] // 2
    tm, tn, tk = min(tm, seq_len), min(tn, dff), min(tk, kdim)
    assert seq_len % tm == 0 and dff % tn == 0 and kdim % tk == 0 and tm % 8 == 0
    nb = m // seq_len
    tps = seq_len // tm
    nj, nk = dff // tn, kdim // tk
    kw = FFN_CONV
    kern = functools.partial(_ffn_up_kernel, nk=nk, tiles_per_seq=tps, tm=tm)
    return pl.pallas_call(
        kern,
        out_shape=(jax.ShapeDtypeStruct((m, dff), BF16), jax.ShapeDtypeStruct((nb, kw - 1, dff), F32)),
        grid=(nj, m // tm, nk),
        in_specs=[
            pl.BlockSpec((tm, tk), lambda j, i, k: (i, k)),
            pl.BlockSpec((tk, tn), lambda j, i, k: (k, j)),
            pl.BlockSpec((tk, tn), lambda j, i, k: (k, j + nj)),
            pl.BlockSpec((kw, tn), lambda j, i, k: (0, j)),
            pl.BlockSpec((1, tn), lambda j, i, k: (0, j)),
            pl.BlockSpec((1, kw - 1, tn), lambda j, i, k: (i // tps, 0, j)),
        ],
        out_specs=(
            pl.BlockSpec((tm, tn), lambda j, i, k: (i, j)),
            pl.BlockSpec((1, kw - 1, tn), lambda j, i, k: (i // tps, 0, j)),
        ),
        scratch_shapes=[pltpu.VMEM((tm, tn), F32), pltpu.VMEM((tm, tn), F32), pltpu.VMEM((8 + tm, tn), F32)],
        compiler_params=_params("parallel", "arbitrary", "arbitrary"),
        name="ffn_up_fused",
    )(a, w_up, w_up, conv_w, conv_b.reshape(1, dff), conv_buf)


def _convgate_kernel(s0_ref, s1_ref, s2_ref, u_ref, cw_ref, cb_ref, o_ref):
    y = s0_ref[...] * cw_ref[2:3, :] + cb_ref[...]
    y = y + s2_ref[...] * cw_ref[0:1, :]
    y = y + s1_ref[...] * cw_ref[1:2, :]
    o_ref[...] = (_gelu(y) * u_ref[...]).astype(o_ref.dtype)


def convgate(s0, s1, s2, u, conv_w, conv_b, *, tn=2048):
    m, dff = s0.shape
    tn = min(tn, dff)
    row = pl.BlockSpec((m, tn), lambda j: (0, j))
    return pl.pallas_call(
        _convgate_kernel,
        out_shape=jax.ShapeDtypeStruct((m, dff), BF16),
        grid=(dff // tn,),
        in_specs=[row, row, row, row, pl.BlockSpec((FFN_CONV, tn), lambda j: (0, j)),
                  pl.BlockSpec((1, tn), lambda j: (0, j))],
        out_specs=row,
        compiler_params=_params("parallel"),
        name="convgate",
    )(s0, s1, s2, u, conv_w, conv_b.reshape(1, dff))


def _fox_prompt_kernel(q_ref, k_ref, v_ref, cq_ref, ck_ref, o_ref, m_sc, l_sc, acc_sc, cq_sc, *, nk, tq, tk):
    qi = pl.program_id(2)
    kj = pl.program_id(3)

    @pl.when(kj == 0)
    def _():
        m_sc[...] = jnp.full_like(m_sc, NEG_INF)
        l_sc[...] = jnp.zeros_like(l_sc)
        acc_sc[...] = jnp.zeros_like(acc_sc)
        cq_sc[...] = jnp.transpose(jnp.broadcast_to(cq_ref[0, 0], (128, tq)))

    @pl.when(kj * tk <= qi * tq + tq - 1)
    def _():
        q = q_ref[0].astype(BF16)
        k = k_ref[0].astype(BF16)
        cq = jnp.concatenate([cq_sc[...]] * (tk // 128), axis=1)
        s = _dot_nt(q, k) * SCALE + (cq - ck_ref[0, 0])
        row = qi * tq + lax.broadcasted_iota(jnp.int32, (tq, tk), 0)
        col = kj * tk + lax.broadcasted_iota(jnp.int32, (tq, tk), 1)
        s = jnp.where(row >= col, s, NEG_INF)
        _online_softmax_step(s, v_ref[0].astype(BF16), m_sc, l_sc, acc_sc)

    @pl.when(kj == nk - 1)
    def _():
        o_ref[0] = (acc_sc[...] / jnp.maximum(l_sc[...], 1e-30)).astype(o_ref.dtype)


def fox_prompt(proj3, c_new):
    b, t, _ = proj3.shape
    tq = min(512, t)
    tk = min(1024, t)
    assert tq % 128 == 0 and tk % 128 == 0
    nq, nk = t // tq, t // tk
    h = FOX_HEADS
    c_row = c_new.transpose(0, 2, 1).reshape(b, h, 1, t)
    kq, kk, kv = COL_FQ // HEAD_DIM, COL_FK // HEAD_DIM, COL_FV // HEAD_DIM

    def kidx(qi, kj):
        return jnp.minimum(kj, (qi * tq + tq - 1) // tk)

    kern = functools.partial(_fox_prompt_kernel, nk=nk, tq=tq, tk=tk)
    return pl.pallas_call(
        kern,
        out_shape=jax.ShapeDtypeStruct((b, t, FOX_W), BF16),
        grid=(b, h, nq, nk),
        in_specs=[
            pl.BlockSpec((1, tq, HEAD_DIM), lambda bi, hi, qi, kj: (bi, qi, kq + hi)),
            pl.BlockSpec((1, tk, HEAD_DIM), lambda bi, hi, qi, kj: (bi, kidx(qi, kj), kk + hi)),
            pl.BlockSpec((1, tk, HEAD_DIM), lambda bi, hi, qi, kj: (bi, kidx(qi, kj), kv + hi)),
            pl.BlockSpec((1, 1, 1, tq), lambda bi, hi, qi, kj: (bi, hi, 0, qi)),
            pl.BlockSpec((1, 1, 1, tk), lambda bi, hi, qi, kj: (bi, hi, 0, kidx(qi, kj))),
        ],
        out_specs=pl.BlockSpec((1, tq, HEAD_DIM), lambda bi, hi, qi, kj: (bi, qi, hi)),
        scratch_shapes=[pltpu.VMEM((tq, 128), F32), pltpu.VMEM((tq, 128), F32), pltpu.VMEM((tq, HEAD_DIM), F32),
                        pltpu.VMEM((tq, 128), F32)],
        compiler_params=_params("parallel", "parallel", "parallel", "arbitrary"),
        name="fox_prompt",
    )(proj3, proj3, proj3, c_row, c_row)


def _compress_kernel(x_ref, pe_ref, w1_ref, w2_ref, o_ref, *, tr):
    acc = jnp.zeros((tr, HEAD_DIM), F32)
    for l in range(CMP_BLOCK):
        xl = x_ref[pl.ds(l, tr, stride=CMP_BLOCK), :] + pe_ref[l:l + 1, :]
        acc = acc + _dot(xl.astype(BF16), w1_ref[l])
    o_ref[0] = _dot(_gelu(acc).astype(BF16), w2_ref[...])


def compress(x2d, col_block, n_groups, cmp_w):
    w1, pe, w2 = cmp_w
    rows = x2d.shape[0]
    r = rows // CMP_BLOCK
    tr = min(r, 256)
    assert r % tr == 0 and rows % CMP_BLOCK == 0
    return pl.pallas_call(
        functools.partial(_compress_kernel, tr=tr),
        out_shape=jax.ShapeDtypeStruct((n_groups, r, HEAD_DIM), F32),
        grid=(n_groups, r // tr),
        in_specs=[pl.BlockSpec((tr * CMP_BLOCK, HEAD_DIM), lambda g, i: (i, col_block + g)),
                  pl.BlockSpec((CMP_BLOCK, HEAD_DIM), lambda g, i: (0, 0)),
                  pl.BlockSpec((CMP_BLOCK, HEAD_DIM, HEAD_DIM), lambda g, i: (0, 0, 0)),
                  pl.BlockSpec((HEAD_DIM, HEAD_DIM), lambda g, i: (0, 0))],
        out_specs=pl.BlockSpec((1, tr, HEAD_DIM), lambda g, i: (g, i, 0)),
        compiler_params=_params("parallel", "parallel"),
        name="nsa_compress",
    )(x2d, pe, w1.astype(BF16), w2.astype(BF16))


def _compress_pool_kernel(x_ref, pe_ref, w1_ref, w2_ref, o_ref, *, tr):
    for g in range(x_ref.shape[1]):
        acc = jnp.zeros((tr, HEAD_DIM), F32)
        for l in range(0, CMP_BLOCK, 2):
            xa = x_ref[pl.ds(l, tr, stride=CMP_BLOCK), g, :] + pe_ref[l:l + 1, :]
            xb = x_ref[pl.ds(l + 1, tr, stride=CMP_BLOCK), g, :] + pe_ref[l + 1:l + 2, :]
            acc = acc + _dot(jnp.concatenate([xa, xb], axis=1).astype(BF16), w1_ref[l // 2])
        o_ref[:, g * HEAD_DIM:(g + 1) * HEAD_DIM] = _dot(_gelu(acc).astype(BF16), w2_ref[...])


def compress_pool(pool, cmp_w):
    w1, pe, w2 = cmp_w
    n_phys, page, g, d = pool.shape
    r = n_phys * page // CMP_BLOCK
    tr = max(c for c in range(8, 257, 8) if r % c == 0)
    return pl.pallas_call(
        functools.partial(_compress_pool_kernel, tr=tr),
        out_shape=jax.ShapeDtypeStruct((r, g * d), F32),
        grid=(r // tr,),
        in_specs=[pl.BlockSpec((tr * CMP_BLOCK, g, d), lambda i: (i, 0, 0)),
                  pl.BlockSpec((CMP_BLOCK, d), lambda i: (0, 0)),
                  pl.BlockSpec((CMP_BLOCK // 2, 2 * d, d), lambda i: (0, 0, 0)),
                  pl.BlockSpec((d, d), lambda i: (0, 0))],
        out_specs=pl.BlockSpec((tr, g * d), lambda i: (i, 0)),
        compiler_params=_params("parallel"),
        name="nsa_compress_pool",
    )(pool.reshape(n_phys * page, g, d), pe, w1.astype(BF16).reshape(CMP_BLOCK // 2, 2 * d, d), w2.astype(BF16))


def _cmp_prompt_kernel(q_ref, ck_ref, cv_ref, bias_ref, exp_ref, o_ref, sel_ref, *, tq, nc, n_top):
    qi = pl.program_id(2)
    ns = nc // 2
    t_col = qi * tq + lax.broadcasted_iota(jnp.int32, (tq, nc), 0)
    lane = lax.broadcasted_iota(jnp.int32, (tq, nc), 1)
    blk_c = jnp.where(lane < ns, 2 * lane, 2 * (lane - ns) + 1)
    mask = t_col >= (blk_c + 1) * CMP_BLOCK - 1
    ck = ck_ref[0, 0].astype(BF16)
    cv = cv_ref[0, 0].astype(BF16)
    imp = jnp.zeros((tq, nc), F32)
    for hg in range(NSA_GROUP):
        sl = slice(hg * HEAD_DIM, (hg + 1) * HEAD_DIM)
        s = _dot_nt(q_ref[0, :, sl].astype(BF16), ck) * SCALE + bias_ref[hg]
        s = jnp.where(mask, s, NEG_INF)
        m = jnp.max(s, axis=-1, keepdims=True)
        e = jnp.where(mask, jnp.exp(s - m), 0.0)
        p = e / jnp.maximum(jnp.sum(e, axis=-1, keepdims=True), 1e-30)
        imp = imp + p
        o_ref[0, :, sl] = _dot(p.astype(BF16), cv)
    imp_s = imp[:, :ns] + imp[:, ns:]
    t_s = qi * tq + lax.broadcasted_iota(jnp.int32, (tq, ns), 0)
    blk = lax.broadcasted_iota(jnp.int32, (tq, ns), 1)
    cur = t_s // SEL_BLOCK
    valid = blk * SEL_BLOCK <= t_s
    forced = (blk == 0) | (blk == cur) | (blk == cur - 1)
    score = jnp.where(forced, SEL_FORCE, jnp.where(valid, imp_s, -1.0))
    rank = jnp.zeros((tq, ns), F32)
    for j in range(ns):
        cj = score[:, j:j + 1]
        ahead = (cj > score) | ((cj == score) & (blk > j))
        rank = rank + jnp.where(ahead, 1.0, 0.0)
    sel = jnp.where(rank < n_top, 1.0, 0.0).astype(BF16)
    sel_ref[0, 0] = (1.0 - _dot(sel, exp_ref[...])) * NEG_INF


def cmp_prompt(proj3, ck_perm, cv_perm, bias_cmp):
    b, t, _ = proj3.shape
    nc = t // CMP_BLOCK
    ns = t // SEL_BLOCK
    tq = min(256, t)
    g = NSA_KV_HEADS
    n_top = min(N_SEL, ns)
    expand = jnp.asarray(np.repeat(np.eye(ns, dtype=np.float32), SEL_BLOCK, axis=1), BF16)
    kq = COL_NQ // GROUP_W
    kern = functools.partial(_cmp_prompt_kernel, tq=tq, nc=nc, n_top=n_top)
    return pl.pallas_call(
        kern,
        out_shape=(jax.ShapeDtypeStruct((b, t, NSA_W), F32), jax.ShapeDtypeStruct((b, g, t, t), F32)),
        grid=(b, g, t // tq),
        in_specs=[
            pl.BlockSpec((1, tq, GROUP_W), lambda bi, gi, qi: (bi, qi, kq + gi)),
            pl.BlockSpec((1, 1, nc, HEAD_DIM), lambda bi, gi, qi: (gi, bi, 0, 0)),
            pl.BlockSpec((1, 1, nc, HEAD_DIM), lambda bi, gi, qi: (gi, bi, 0, 0)),
            pl.BlockSpec((NSA_GROUP, tq, nc), lambda bi, gi, qi: (gi, qi, 0)),
            pl.BlockSpec((ns, t), lambda bi, gi, qi: (0, 0)),
        ],
        out_specs=(
            pl.BlockSpec((1, tq, GROUP_W), lambda bi, gi, qi: (bi, qi, gi)),
            pl.BlockSpec((1, 1, tq, t), lambda bi, gi, qi: (bi, gi, qi, 0)),
        ),
        compiler_params=_params("parallel", "parallel", "parallel"),
        name="nsa_cmp_prompt",
    )(proj3, ck_perm, cv_perm, bias_cmp, expand)


NSA_TQ = 128
NSA_TK = 512


def _nsa_band_kernel(*refs, nr, ratio, o_max, mode, has_sel, combine):
    q_ref, k_ref, v_ref, bias_ref = refs[:4]
    o_ref, qs, m_sc, l_sc, acc_sc = refs[-5:]
    extra = list(refs[4:-5])
    sel_ref = extra.pop(0) if has_sel else None
    small_ref, oc_ref, os_ref = extra if combine else (None, None, None)
    qi = pl.program_id(2)
    r = pl.program_id(3)
    tq = NSA_TQ
    tk = k_ref.shape[1]
    hg_n = NSA_GROUP

    @pl.when(r == 0)
    def _():
        for hg in range(hg_n):
            qs[hg * tq:(hg + 1) * tq, :] = q_ref[0, :, hg * HEAD_DIM:(hg + 1) * HEAD_DIM].astype(BF16)
        m_sc[...] = jnp.full_like(m_sc, NEG_INF)
        l_sc[...] = jnp.zeros_like(l_sc)
        acc_sc[...] = jnp.zeros_like(acc_sc)

    if mode == "causal":
        active = r <= qi // ratio
        kj = jnp.minimum(r, qi // ratio)
    else:
        active = qi // ratio - (nr - 1) + r >= 0
        kj = jnp.maximum(qi // ratio - (nr - 1) + r, 0)
    chunk0 = o_max - jnp.clip(qi - ratio * kj, 0, o_max)

    @pl.when(active)
    def _():
        s = _dot_nt(qs[...], k_ref[0].astype(BF16)) * SCALE
        bias = jnp.concatenate([bias_ref[:, chunk0 + u] for u in range(ratio)], axis=-1)
        s3 = s.reshape(hg_n, tq, tk) + bias
        if has_sel:
            s3 = s3 + sel_ref[0, 0][None]
        _online_softmax_step(s3.reshape(hg_n * tq, tk), v_ref[0].astype(BF16), m_sc, l_sc, acc_sc)

    @pl.when(r == nr - 1)
    def _():
        o = acc_sc[...] / jnp.maximum(l_sc[...], 1e-30)
        if combine:
            gates = _sigmoid(small_ref[0])
            second_group = pl.program_id(1) == 1
        for hg in range(hg_n):
            sl = slice(hg * HEAD_DIM, (hg + 1) * HEAD_DIM)
            o_h = o[hg * tq:(hg + 1) * tq, :]
            if combine:
                def gate(branch):
                    c = FOX_HEADS + branch * NSA_HEADS + hg
                    return jnp.where(second_group, gates[:, c + hg_n:c + hg_n + 1], gates[:, c:c + 1])

                o_h = gate(0) * oc_ref[0, :, sl] + gate(1) * os_ref[0, :, sl] + gate(2) * o_h
            o_ref[0, :, sl] = o_h.astype(o_ref.dtype)


def nsa_band(proj3, bias_strip, tk, col_k, col_v, mode, selmask=None, combine_with=None):
    b, t, _ = proj3.shape
    tq = NSA_TQ
    assert t % tk == 0 and tk % tq == 0
    ratio = tk // tq
    nq, nk = t // tq, t // tk
    g = NSA_KV_HEADS
    n_chunks = bias_strip.shape[1]
    o_max = n_chunks - ratio
    nr = nk if mode == "causal" else -(-WINDOW // tk) + 1
    assert mode == "causal" or o_max == ratio * nr - 1
    kq = COL_NQ // GROUP_W
    ck, cv = col_k // HEAD_DIM, col_v // HEAD_DIM
    if mode == "causal":
        def ktile(qi, r):
            return jnp.minimum(r, qi // ratio)
    else:
        def ktile(qi, r):
            return jnp.maximum(qi // ratio - (nr - 1) + r, 0)

    in_specs = [
        pl.BlockSpec((1, tq, GROUP_W), lambda bi, gi, qi, r: (bi, qi, kq + gi)),
        pl.BlockSpec((1, tk, HEAD_DIM), lambda bi, gi, qi, r: (bi, ktile(qi, r), ck + gi)),
        pl.BlockSpec((1, tk, HEAD_DIM), lambda bi, gi, qi, r: (bi, ktile(qi, r), cv + gi)),
        pl.BlockSpec((NSA_GROUP, n_chunks, tq, 128), lambda bi, gi, qi, r: (gi, 0, 0, 0)),
    ]
    args = [proj3, proj3, proj3, bias_strip]
    if selmask is not None:
        in_specs.append(pl.BlockSpec((1, 1, tq, tk), lambda bi, gi, qi, r: (bi, gi, qi, ktile(qi, r))))
        args.append(selmask)
    head_tile = pl.BlockSpec((1, tq, GROUP_W), lambda bi, gi, qi, r: (bi, qi, gi))
    if combine_with is not None:
        assert g == 2
        in_specs += [pl.BlockSpec((1, tq, 128), lambda bi, gi, qi, r: (bi, qi, COL_SMALL // 128)), head_tile, head_tile]
        args += [proj3, combine_with[0], combine_with[1]]
    kern = functools.partial(_nsa_band_kernel, nr=nr, ratio=ratio, o_max=o_max, mode=mode,
                             has_sel=selmask is not None, combine=combine_with is not None)
    rows = NSA_GROUP * tq
    return pl.pallas_call(
        kern,
        out_shape=jax.ShapeDtypeStruct((b, t, NSA_W), F32 if combine_with is None else BF16),
        grid=(b, g, nq, nr),
        in_specs=in_specs,
        out_specs=pl.BlockSpec((1, tq, GROUP_W), lambda bi, gi, qi, r: (bi, qi, gi)),
        scratch_shapes=[pltpu.VMEM((rows, HEAD_DIM), BF16), pltpu.VMEM((rows, 128), F32),
                        pltpu.VMEM((rows, 128), F32), pltpu.VMEM((rows, HEAD_DIM), F32)],
        compiler_params=_params("parallel", "parallel", "parallel", "arbitrary"),
        name="nsa_band_" + mode,
    )(*args)


_T5_THRESHOLDS = tuple(int(np.searchsorted(_bucket_np(np.arange(2 * MAX_DISTANCE)), k)) for k in range(1, N_BUCKETS))


def _t5_bias(rel_bias, dist):
    per_head = lambda k: rel_bias[k].reshape((-1,) + (1,) * dist.ndim)
    d = dist[None]
    out = jnp.broadcast_to(per_head(0), (rel_bias.shape[1],) + dist.shape)
    for k, thr in enumerate(_T5_THRESHOLDS, start=1):
        out = jnp.where(d >= thr, per_head(k), out)
    return out


def _band_strip(rel_bias, o_max, ratio, window=None):
    c = jnp.arange(o_max + ratio, dtype=jnp.int32)[:, None, None]
    i = jnp.arange(NSA_TQ, dtype=jnp.int32)[None, :, None]
    jj = jnp.arange(128, dtype=jnp.int32)[None, None, :]
    d = i - (c * 128 + jj) + o_max * NSA_TQ
    ok = d >= 0
    if window is not None:
        ok = ok & (d < window)
    return jnp.where(ok[None], _t5_bias(rel_bias, d), NEG_INF)


def _combine_kernel(small_ref, oc_ref, os_ref, ow_ref, o_ref):
    gates = _sigmoid(small_ref[...])
    for h in range(NSA_HEADS):
        sl = slice(h * HEAD_DIM, (h + 1) * HEAD_DIM)
        base = FOX_HEADS + h
        g0 = gates[:, base:base + 1]
        g1 = gates[:, base + NSA_HEADS:base + NSA_HEADS + 1]
        g2 = gates[:, base + 2 * NSA_HEADS:base + 2 * NSA_HEADS + 1]
        o = g0 * oc_ref[:, sl] + g1 * os_ref[:, sl] + g2 * ow_ref[:, sl]
        o_ref[:, sl] = o.astype(o_ref.dtype)


def nsa_combine(proj, o_cmp, o_slc, o_win):
    m = proj.shape[0]
    tm = min(m, 256)
    wide = pl.BlockSpec((tm, NSA_W), lambda i: (i, 0))
    return pl.pallas_call(
        _combine_kernel,
        out_shape=jax.ShapeDtypeStruct((m, NSA_W), BF16),
        grid=(m // tm,),
        in_specs=[pl.BlockSpec((tm, 128), lambda i: (i, COL_SMALL // 128)), wide, wide, wide],
        out_specs=wide,
        compiler_params=_params("parallel"),
        name="nsa_combine",
    )(proj, o_cmp, o_slc, o_win)


def _lru_kernel(xr_ref, gate_ref, cw_ref, cb_ref, wa_ref, wx_ref, ba_ref, bx_ref, lam_ref, buf_ref, h0_ref,
                y_ref, hlast_ref, nbuf_ref, xbuf, hcar, hsbuf, *, nt, tt, last_row):
    ti = pl.program_id(2)
    kw = LRU_CONV

    @pl.when(ti == 0)
    def _():
        xbuf[8 - (kw - 1):8, :] = buf_ref[0]
        hcar[...] = h0_ref[0]

    x = xr_ref[0]
    xbuf[8:8 + tt, :] = x
    xc = x * cw_ref[kw - 1:kw, :] + cb_ref[...]
    for t in range(kw - 1):
        off = 8 - (kw - 1) + t
        xc = xc + xbuf[off:off + tt, :] * cw_ref[t:t + 1, :]
    xcb = xc.astype(BF16)
    r = _sigmoid(_dot(xcb, wa_ref[0]) + ba_ref[...])
    ig = _sigmoid(_dot(xcb, wx_ref[0]) + bx_ref[...])
    neg_lam = -lam_ref[...]
    softplus = jnp.maximum(neg_lam, 0.0) + jnp.log1p(jnp.exp(-jnp.abs(neg_lam)))
    log_a = -LRU_C * r * softplus
    a = jnp.exp(log_a)
    th = jnp.tanh(log_a)
    u = jnp.sqrt(-2.0 * th / (1.0 - th)) * (ig * xc)
    ng = tt // 8
    a = a.reshape(ng, 8, a.shape[-1])
    u = u.reshape(ng, 8, u.shape[-1])
    sub = lax.broadcasted_iota(jnp.int32, a.shape, 1)
    for step in (1, 2, 4):
        keep = sub >= step
        a_sh = jnp.where(keep, pltpu.roll(a, step, 1), 1.0)
        u_sh = jnp.where(keep, pltpu.roll(u, step, 1), 0.0)
        u = a * u_sh + u
        a = a * a_sh
    h = hcar[...]
    for gi in range(ng):
        hs_g = a[gi] * h + u[gi]
        hsbuf[gi * 8:(gi + 1) * 8, :] = hs_g
        h = hs_g[7:8, :]
    hcar[...] = h
    y_ref[0] = (_gelu(gate_ref[0]) * hsbuf[...]).astype(y_ref.dtype)

    @pl.when(ti == nt - 1)
    def _():
        hlast_ref[0] = hsbuf[last_row:last_row + 1, :]
        nbuf_ref[0] = xbuf[8 + last_row - (kw - 2):8 + last_row + 1, :]

    xbuf[8 - (kw - 1):8, :] = xbuf[8 + tt - (kw - 1):8 + tt, :]


def lru_block(proj1, conv_w, conv_b, w_a, b_a, w_x, b_x, lam, conv_buf, h0, t_valid):
    b, tp, w2 = proj1.shape
    w = w2 // 2
    nh = LRU_HEADS
    bw = w // nh
    tt = min(tp, 256)
    assert tp % tt == 0 and tt % 8 == 0
    nt = tp // tt
    last_row = (t_valid - 1) - (nt - 1) * tt
    assert 0 <= last_row < tt and (nt > 1 or last_row >= 0)
    kw = LRU_CONV
    vec = lambda a: a.reshape(1, w)
    chan = pl.BlockSpec((1, bw), lambda bi, hi, ti: (0, hi))
    kern = functools.partial(_lru_kernel, nt=nt, tt=tt, last_row=last_row)
    return pl.pallas_call(
        kern,
        out_shape=(jax.ShapeDtypeStruct((b, tp, w), BF16), jax.ShapeDtypeStruct((b, 1, w), F32),
                   jax.ShapeDtypeStruct((b, kw - 1, w), F32)),
        grid=(b, nh, nt),
        in_specs=[
            pl.BlockSpec((1, tt, bw), lambda bi, hi, ti: (bi, ti, nh + hi)),
            pl.BlockSpec((1, tt, bw), lambda bi, hi, ti: (bi, ti, hi)),
            pl.BlockSpec((kw, bw), lambda bi, hi, ti: (0, hi)),
            chan,
            pl.BlockSpec((1, bw, bw), lambda bi, hi, ti: (hi, 0, 0)),
            pl.BlockSpec((1, bw, bw), lambda bi, hi, ti: (hi, 0, 0)),
            chan, chan, chan,
            pl.BlockSpec((1, kw - 1, bw), lambda bi, hi, ti: (bi, 0, hi)),
            pl.BlockSpec((1, 1, bw), lambda bi, hi, ti: (bi, 0, hi)),
        ],
        out_specs=(
            pl.BlockSpec((1, tt, bw), lambda bi, hi, ti: (bi, ti, hi)),
            pl.BlockSpec((1, 1, bw), lambda bi, hi, ti: (bi, 0, hi)),
            pl.BlockSpec((1, kw - 1, bw), lambda bi, hi, ti: (bi, 0, hi)),
        ),
        scratch_shapes=[pltpu.VMEM((8 + tt, bw), F32), pltpu.VMEM((1, bw), F32), pltpu.VMEM((tt, bw), F32)],
        compiler_params=_params("parallel", "parallel", "arbitrary"),
        name="rglru",
    )(proj1, proj1, conv_w, vec(conv_b), w_a.astype(BF16), w_x.astype(BF16), vec(b_a), vec(b_x), vec(lam),
      conv_buf, h0.reshape(b, 1, w))


GATHER_PAGES = 8


def _gather_kernel(pt_ref, *refs, n_pools, pg):
    ins, outs = refs[:n_pools * pg], refs[n_pools * pg:]
    for c in range(n_pools):
        for i in range(pg):
            src = ins[c * pg + i]
            rows = src.shape[1]
            outs[c][0, i * rows:(i + 1) * rows, :] = src[0]


def gather_pools(pools, page_table):
    b, n_pages = page_table.shape
    pg = math.gcd(GATHER_PAGES, n_pages)
    in_specs, args, out_shape, out_specs = [], [], [], []
    for pool in pools:
        _, rows, c = pool.shape
        assert (pg * rows) % 8 == 0
        for i in range(pg):
            in_specs.append(pl.BlockSpec((1, rows, c), lambda bi, j, pt, i=i: (pt[bi, j * pg + i], 0, 0)))
            args.append(pool)
        out_shape.append(jax.ShapeDtypeStruct((b, n_pages * rows, c), pool.dtype))
        out_specs.append(pl.BlockSpec((1, pg * rows, c), lambda bi, j, pt: (bi, j, 0)))
    out_shape, out_specs = tuple(out_shape), tuple(out_specs)
    return pl.pallas_call(
        functools.partial(_gather_kernel, n_pools=len(pools), pg=pg),
        out_shape=out_shape,
        grid_spec=pltpu.PrefetchScalarGridSpec(
            num_scalar_prefetch=1, grid=(b, n_pages // pg), in_specs=in_specs, out_specs=out_specs),
        compiler_params=_params("parallel", "arbitrary"),
        name="gather_pages",
    )(page_table, *args)


FOX_DECODE_PAGES = 4


def _fox_decode_kernel(pt_ref, q_ref, cq_ref, ck_ref, cknew_ref, *refs, n_steps, pf):
    kp, vp = refs[:pf], refs[pf:2 * pf]
    knew_ref, vnew_ref, o_ref, m_sc, l_sc, acc_sc, cq_sc = refs[2 * pf:]
    j = pl.program_id(1)
    rows = q_ref.shape[1]
    nh = FOX_HEADS
    w = PAGE_SIZE * nh

    @pl.when(j == 0)
    def _():
        m_sc[...] = jnp.full_like(m_sc, NEG_INF)
        l_sc[...] = jnp.zeros_like(l_sc)
        acc_sc[...] = jnp.zeros_like(acc_sc)
        col = lax.broadcasted_iota(jnp.int32, (rows, w), 1)
        row = lax.broadcasted_iota(jnp.int32, (rows, w), 0)
        cq_sc[...] = jnp.where((col % nh) == (row % nh), cq_ref[0], NEG_INF)

    def update(k3, v3, ck_row, causal):
        n = k3.shape[0] * nh
        k2 = k3.reshape(n, HEAD_DIM).astype(BF16)
        v2 = v3.reshape(n, HEAD_DIM).astype(BF16)
        s = _dot_nt(q_ref[0], k2) * SCALE + (cq_sc[:, :n] - ck_row)
        if causal:
            col = lax.broadcasted_iota(jnp.int32, (rows, n), 1)
            row = lax.broadcasted_iota(jnp.int32, (rows, n), 0)
            s = jnp.where(col // nh <= row // nh, s, NEG_INF)
        _online_softmax_step(s, v2, m_sc, l_sc, acc_sc)

    @pl.when(j < n_steps)
    def _():
        for i in range(pf):
            update(kp[i][0], vp[i][0], ck_ref[0, :, i * w:(i + 1) * w], False)

    @pl.when(j == n_steps)
    def _():
        update(knew_ref[0], vnew_ref[0], cknew_ref[0], True)
        o_ref[0] = acc_sc[...] / jnp.maximum(l_sc[...], 1e-30)


def fox_decode(q_rows, cq_rows, ck_past, ck_new, k_pool, v_pool, k_new, v_new, page_table):
    b, n_pages = page_table.shape
    pf = math.gcd(FOX_DECODE_PAGES, n_pages)
    n_steps = n_pages // pf
    rows = q_rows.shape[1]
    tp = k_new.shape[1]
    nh = FOX_HEADS

    def page(i):
        return lambda bi, j, pt: (pt[bi, jnp.minimum(j * pf + i, n_pages - 1)], 0, 0, 0)

    pool_spec = [pl.BlockSpec((1, PAGE_SIZE, nh, HEAD_DIM), page(i)) for i in range(pf)]
    per_seq = lambda shape: pl.BlockSpec(shape, lambda bi, j, pt: (bi,) + (0,) * (len(shape) - 1))
    kern = functools.partial(_fox_decode_kernel, n_steps=n_steps, pf=pf)
    return pl.pallas_call(
        kern,
        out_shape=jax.ShapeDtypeStruct((b, rows, HEAD_DIM), F32),
        grid_spec=pltpu.PrefetchScalarGridSpec(
            num_scalar_prefetch=1,
            grid=(b, n_steps + 1),
            in_specs=[per_seq((1, rows, HEAD_DIM)), per_seq((1, rows, 1)),
                      pl.BlockSpec((1, 1, pf * PAGE_SIZE * nh), lambda bi, j, pt: (bi, 0, jnp.minimum(j, n_steps - 1))),
                      per_seq((1, 1, tp * nh))]
                     + pool_spec + pool_spec
                     + [per_seq((1, tp, nh, HEAD_DIM)), per_seq((1, tp, nh, HEAD_DIM))],
            out_specs=per_seq((1, rows, HEAD_DIM)),
            scratch_shapes=[pltpu.VMEM((rows, 128), F32), pltpu.VMEM((rows, 128), F32),
                            pltpu.VMEM((rows, HEAD_DIM), F32), pltpu.VMEM((rows, PAGE_SIZE * nh), F32)],
        ),
        compiler_params=_params("parallel", "arbitrary"),
        name="fox_decode",
    )(page_table, q_rows, cq_rows, ck_past, ck_new, *([k_pool] * pf), *([v_pool] * pf), k_new, v_new)


SLC_DECODE_PAGES = 16


def _slc_decode_kernel(pt_ref, q_ref, bias_ref, biasnew_ref, *refs, n_steps, pf):
    kp, vp = refs[:pf], refs[pf:2 * pf]
    knew_ref, vnew_ref, o_ref, m_sc, l_sc, acc_sc = refs[2 * pf:]
    j = pl.program_id(1)

    @pl.when(j == 0)
    def _():
        m_sc[...] = jnp.full_like(m_sc, NEG_INF)
        l_sc[...] = jnp.zeros_like(l_sc)
        acc_sc[...] = jnp.zeros_like(acc_sc)

    def update(g, k2, v2, bias):
        s = _dot_nt(q_ref[0, g], k2.astype(BF16)) * SCALE + bias
        _online_softmax_step(s, v2.astype(BF16), m_sc.at[g], l_sc.at[g], acc_sc.at[g])

    @pl.when(j < n_steps)
    def _():
        for g in range(q_ref.shape[1]):
            k2 = jnp.concatenate([kp[i][0, :, g, :] for i in range(pf)], axis=0)
            v2 = jnp.concatenate([vp[i][0, :, g, :] for i in range(pf)], axis=0)
            update(g, k2, v2, bias_ref[0, g])

    @pl.when(j == n_steps)
    def _():
        for g in range(q_ref.shape[1]):
            update(g, knew_ref[0, :, g, :], vnew_ref[0, :, g, :], biasnew_ref[0, g])
        o_ref[0] = acc_sc[...] / jnp.maximum(l_sc[...], 1e-30)


def slc_decode(q_rows, bias_past, bias_new, k_pool, v_pool, k_new, v_new, page_table):
    b, n_pages = page_table.shape
    pf = math.gcd(SLC_DECODE_PAGES, n_pages)
    n_steps = n_pages // pf
    g, rows = q_rows.shape[1], q_rows.shape[2]
    tp = k_new.shape[1]
    assert tp % 128 == 0

    def page(i):
        return lambda bi, j, pt: (pt[bi, jnp.minimum(j * pf + i, n_pages - 1)], 0, 0, 0)

    pool_spec = [pl.BlockSpec((1, PAGE_SIZE, g, HEAD_DIM), page(i)) for i in range(pf)]
    per_seq = lambda shape: pl.BlockSpec(shape, lambda bi, j, pt: (bi,) + (0,) * (len(shape) - 1))
    return pl.pallas_call(
        functools.partial(_slc_decode_kernel, n_steps=n_steps, pf=pf),
        out_shape=jax.ShapeDtypeStruct((b, g, rows, HEAD_DIM), F32),
        grid_spec=pltpu.PrefetchScalarGridSpec(
            num_scalar_prefetch=1,
            grid=(b, n_steps + 1),
            in_specs=[per_seq((1, g, rows, HEAD_DIM)),
                      pl.BlockSpec((1, g, rows, pf * PAGE_SIZE),
                                   lambda bi, j, pt: (bi, 0, 0, jnp.minimum(j, n_steps - 1))),
                      per_seq((1, g, rows, tp))]
                     + pool_spec + pool_spec
                     + [per_seq((1, tp, g, HEAD_DIM)), per_seq((1, tp, g, HEAD_DIM))],
            out_specs=per_seq((1, g, rows, HEAD_DIM)),
            scratch_shapes=[pltpu.VMEM((g, rows, 128), F32), pltpu.VMEM((g, rows, 128), F32),
                            pltpu.VMEM((g, rows, HEAD_DIM), F32)],
        ),
        compiler_params=_params("parallel", "arbitrary"),
        name="nsa_slc_decode",
    )(page_table, q_rows, bias_past, bias_new, *([k_pool] * pf), *([v_pool] * pf), k_new, v_new)


def _cmp_small_kernel(q_ref, k_ref, v_ref, bias_ref, o_ref, imp_ref, *, t_new):
    s = _dot_nt(q_ref[0, 0], k_ref[0].astype(BF16)) * SCALE + bias_ref[0]
    mask = s > VISIBLE_MIN
    m = jnp.max(s, axis=-1, keepdims=True)
    e = jnp.where(mask, jnp.exp(s - m), 0.0)
    p = e / jnp.maximum(jnp.sum(e, axis=-1, keepdims=True), 1e-30)
    o_ref[0, 0] = _dot(p.astype(BF16), v_ref[0].astype(BF16))
    imp_ref[0, 0] = jnp.sum(p.reshape(t_new, NSA_GROUP, p.shape[-1]), axis=1)


def cmp_small(q_rows, ck, cv, bias, t_new):
    b, g, rows, d = q_rows.shape
    s = ck.shape[1]
    return pl.pallas_call(
        functools.partial(_cmp_small_kernel, t_new=t_new),
        out_shape=(jax.ShapeDtypeStruct((b, g, rows, d), F32), jax.ShapeDtypeStruct((b, g, t_new, s), F32)),
        grid=(b, g),
        in_specs=[
            pl.BlockSpec((1, 1, rows, d), lambda bi, gi: (bi, gi, 0, 0)),
            pl.BlockSpec((1, s, d), lambda bi, gi: (bi, 0, gi)),
            pl.BlockSpec((1, s, d), lambda bi, gi: (bi, 0, gi)),
            pl.BlockSpec((1, rows, s), lambda bi, gi: (gi, 0, 0)),
        ],
        out_specs=(pl.BlockSpec((1, 1, rows, d), lambda bi, gi: (bi, gi, 0, 0)),
                   pl.BlockSpec((1, 1, t_new, s), lambda bi, gi: (bi, gi, 0, 0))),
        compiler_params=_params("parallel", "parallel"),
        name="nsa_cmp_sample",
    )(q_rows, ck, cv, bias)


def _flash_small_kernel(q_ref, k_ref, v_ref, bias_ref, o_ref, m_sc, l_sc, acc_sc, *, nk):
    j = pl.program_id(2)

    @pl.when(j == 0)
    def _():
        m_sc[...] = jnp.full_like(m_sc, NEG_INF)
        l_sc[...] = jnp.zeros_like(l_sc)
        acc_sc[...] = jnp.zeros_like(acc_sc)

    s = _dot_nt(q_ref[0, 0], k_ref[0, 0].astype(BF16)) * SCALE + bias_ref[0, 0]
    vis = s > VISIBLE_MIN
    m_prev = m_sc[...]
    m_new = jnp.maximum(m_prev, jnp.max(s, axis=-1, keepdims=True))
    alpha = jnp.exp(m_prev - m_new)
    p = jnp.where(vis, jnp.exp(s - m_new), 0.0)
    l_sc[...] = alpha * l_sc[...] + jnp.sum(p, axis=-1, keepdims=True)
    acc_sc[...] = alpha * acc_sc[...] + _dot(p.astype(BF16), v_ref[0, 0].astype(BF16))
    m_sc[...] = m_new

    @pl.when(j == nk - 1)
    def _():
        o_ref[0, 0] = acc_sc[...] / jnp.maximum(l_sc[...], 1e-30)


def flash_small(q_rows, k, v, bias, tk):
    b, g, rows, d = q_rows.shape
    s = k.shape[2]
    assert s % tk == 0
    nk = s // tk
    return pl.pallas_call(
        functools.partial(_flash_small_kernel, nk=nk),
        out_shape=jax.ShapeDtypeStruct((b, g, rows, d), F32),
        grid=(b, g, nk),
        in_specs=[
            pl.BlockSpec((1, 1, rows, d), lambda bi, gi, j: (bi, gi, 0, 0)),
            pl.BlockSpec((1, 1, tk, d), lambda bi, gi, j: (bi, gi, j, 0)),
            pl.BlockSpec((1, 1, tk, d), lambda bi, gi, j: (bi, gi, j, 0)),
            pl.BlockSpec((1, 1, rows, tk), lambda bi, gi, j: (bi, gi, 0, j)),
        ],
        out_specs=pl.BlockSpec((1, 1, rows, d), lambda bi, gi, j: (bi, gi, 0, 0)),
        scratch_shapes=[pltpu.VMEM((rows, 1), F32), pltpu.VMEM((rows, 1), F32), pltpu.VMEM((rows, d), F32)],
        compiler_params=_params("parallel", "parallel", "arbitrary"),
        name="nsa_flash_sample",
    )(q_rows, k, v, bias)


def _topk_kernel(col_ref, row_ref, o_ref, *, n_top):
    a = col_ref[...]
    bb = row_ref[...]
    shape = (a.shape[0], a.shape[1], bb.shape[2])
    j_idx = lax.broadcasted_iota(jnp.int32, shape, 1)
    i_idx = lax.broadcasted_iota(jnp.int32, shape, 2)
    ahead = (a > bb) | ((a == bb) & (j_idx < i_idx))
    rank = jnp.sum(jnp.where(ahead, 1.0, 0.0), axis=1, keepdims=True)
    o_ref[...] = jnp.where(rank < n_top, 1.0, 0.0)


def topk_mask(score, n_top):
    r, ns = score.shape
    tr = 8 if r % 8 == 0 else r
    out = pl.pallas_call(
        functools.partial(_topk_kernel, n_top=n_top),
        out_shape=jax.ShapeDtypeStruct((r, 1, ns), F32),
        grid=(r // tr,),
        in_specs=[pl.BlockSpec((tr, ns, 1), lambda i: (i, 0, 0)), pl.BlockSpec((tr, 1, ns), lambda i: (i, 0, 0))],
        out_specs=pl.BlockSpec((tr, 1, ns), lambda i: (i, 0, 0)),
        compiler_params=_params("parallel"),
        name="nsa_topk",
    )(score.reshape(r, ns, 1), score.reshape(r, 1, ns))
    return out.reshape(r, ns)


def _pack_w_in0(w_in0):
    d = w_in0.shape[0]
    off_fz = 3 * FOX_W
    off_nq = off_fz + FOX_HEADS
    off_kv = off_nq + NSA_W
    off_ng = off_kv + 6 * NSA_KV_W
    parts = [w_in0[:, :off_fz], w_in0[:, off_nq:off_ng], w_in0[:, off_fz:off_nq], w_in0[:, off_ng:]]
    used = sum(p.shape[1] for p in parts)
    parts.append(jnp.zeros((d, L0_PACKED - used), w_in0.dtype))
    return jnp.concatenate(parts, axis=1).astype(BF16)


def _forget_terms(proj3, b_forget):
    fz = proj3[:, :, COL_SMALL:COL_SMALL + FOX_HEADS]
    lf = jax.nn.log_sigmoid(fz + b_forget.astype(F32))
    return lf, jnp.cumsum(lf, axis=1)


def _kv_cols(proj3, idx):
    off = COL_KV + idx * NSA_KV_W
    return proj3[:, :, off:off + NSA_KV_W]


def attn_prompt(h_bf16, b, t, wts):
    m = b * t
    g = NSA_KV_HEADS
    proj = matmul(h_bf16, wts["w_in0"])
    proj3 = proj.reshape(b, t, L0_PACKED)
    lf, c_new = _forget_terms(proj3, wts["b_forget"])
    o_fox = fox_prompt(proj3, c_new)

    kc, vc, ks, vs, kw, vw = [_kv_cols(proj3, i) for i in range(6)]
    nc, ns = t // CMP_BLOCK, t // SEL_BLOCK
    tk = min(NSA_TK, t)
    assert t % (2 * CMP_BLOCK) == 0 and t % tk == 0
    ck = compress(proj, COL_KV // HEAD_DIM, g, wts["cmp_k"]).reshape(g, b, nc, HEAD_DIM)
    cv = compress(proj, (COL_KV + NSA_KV_W) // HEAD_DIM, g, wts["cmp_v"]).reshape(g, b, nc, HEAD_DIM)
    even_odd = lambda a: jnp.concatenate([a[:, :, 0::2], a[:, :, 1::2]], axis=2)
    rel = wts["rel_bias"].astype(F32)
    blk_c = jnp.concatenate([jnp.arange(0, nc, 2, dtype=jnp.int32), jnp.arange(1, nc, 2, dtype=jnp.int32)])
    c_dist = jnp.arange(t, dtype=jnp.int32)[:, None] - ((blk_c + 1) * CMP_BLOCK - 1)[None, :]
    bias_cmp = _t5_bias(rel, c_dist)
    o_cmp, selmask = cmp_prompt(proj3, even_odd(ck), even_odd(cv), bias_cmp)

    ratio = tk // NSA_TQ
    o_far = -(-(_T5_THRESHOLDS[-1] + tk - 1) // NSA_TQ)
    o_slc = nsa_band(proj3, _band_strip(rel, o_far, ratio), tk, COL_KV + 2 * NSA_KV_W, COL_KV + 3 * NSA_KV_W,
                     "causal", selmask)
    o_win_max = ratio * (-(-WINDOW // tk) + 1) - 1
    o_nsa = nsa_band(proj3, _band_strip(rel, o_win_max, ratio, window=WINDOW), tk,
                     COL_KV + 4 * NSA_KV_W, COL_KV + 5 * NSA_KV_W, "window", combine_with=(o_cmp, o_slc))
    mix = (o_fox.reshape(m, FOX_W), o_nsa.reshape(m, NSA_W))
    win_len = wts["win_len"]
    zeros = jnp.zeros((b, WINDOW, NSA_KV_W), F32)
    kw_a = jnp.concatenate([zeros, kw], axis=1)
    vw_a = jnp.concatenate([zeros, vw], axis=1)
    fk = proj3[:, :, COL_FK:COL_FK + FOX_W]
    fv = proj3[:, :, COL_FV:COL_FV + FOX_W]
    new = (fk, fv, lf, kc, vc, ks, vs, kw_a[:, kw_a.shape[1] - win_len:], vw_a[:, vw_a.shape[1] - win_len:])
    return mix, new


def attn_sample(h_bf16, b, t, wts, caches, page_table, state_win_k, state_win_v):
    m = b * t
    g, hg = NSA_KV_HEADS, NSA_GROUP
    n_pages = page_table.shape[1]
    p = n_pages * PAGE_SIZE
    assert t <= PAGE_SIZE and p % SEL_BLOCK == 0
    proj = matmul(h_bf16, wts["w_in0"])
    proj3 = proj.reshape(b, t, L0_PACKED)
    lf, c_new = _forget_terms(proj3, wts["b_forget"])
    kc, vc, ks, vs, kw, vw = [_kv_cols(proj3, i) for i in range(6)]
    fq = proj3[:, :, COL_FQ:COL_FQ + FOX_W]
    fk = proj3[:, :, COL_FK:COL_FK + FOX_W]
    fv = proj3[:, :, COL_FV:COL_FV + FOX_W]
    q_pos = p + np.arange(t)

    cache_fox_k, cache_fox_v, cache_fox_lf, cache_cmp_k, cache_cmp_v, cache_slc_k, cache_slc_v = caches
    assert PAGE_SIZE % CMP_BLOCK == 0
    per_page = PAGE_SIZE // CMP_BLOCK
    n_phys = cache_cmp_k.shape[0]
    plf, ck, cv = gather_pools(
        [cache_fox_lf,
         compress_pool(cache_cmp_k, wts["cmp_k"]).reshape(n_phys, per_page, NSA_KV_W),
         compress_pool(cache_cmp_v, wts["cmp_v"]).reshape(n_phys, per_page, NSA_KV_W)], page_table)

    c_past = plf - lax.cumsum(plf, axis=1, reverse=True)
    tp = -(-t // 8) * 8
    pad_new = lambda a: jnp.concatenate([a, jnp.zeros((b, tp - t) + a.shape[2:], F32)], axis=1)
    o_fox = fox_decode(fq.reshape(b, t * FOX_HEADS, HEAD_DIM).astype(BF16), c_new.reshape(b, t * FOX_HEADS, 1),
                       c_past.reshape(b, 1, p * FOX_HEADS), pad_new(c_new).reshape(b, 1, tp * FOX_HEADS),
                       cache_fox_k, cache_fox_v,
                       pad_new(fk.reshape(b, t, FOX_HEADS, HEAD_DIM)), pad_new(fv.reshape(b, t, FOX_HEADS, HEAD_DIM)),
                       page_table).reshape(m, FOX_W)

    rel = wts["rel_bias"].astype(F32)
    nq = proj3[:, :, COL_NQ:COL_NQ + NSA_W].reshape(b, t, g, hg, HEAD_DIM)
    q_rows = nq.transpose(0, 2, 1, 3, 4).reshape(b, g, t * hg, HEAD_DIM).astype(BF16)

    def head_rows(x):
        tt, ss = x.shape[1], x.shape[2]
        return x.reshape(g, hg, tt, ss).transpose(0, 2, 1, 3).reshape(g, tt * hg, ss)

    l_tot = p + t
    l_pad = -(-l_tot // SEL_BLOCK) * SEL_BLOCK
    nc, ns = l_pad // CMP_BLOCK, l_pad // SEL_BLOCK
    nc_past = p // CMP_BLOCK
    assert nc_past % 2 == 0 and (nc_past + 1) * CMP_BLOCK - 1 > q_pos[-1]
    even_odd = lambda a: jnp.concatenate([a[:, 0::2], a[:, 1::2]], axis=1)
    blk_c = jnp.concatenate([jnp.arange(0, nc_past, 2, dtype=jnp.int32), jnp.arange(1, nc_past, 2, dtype=jnp.int32)])
    c_dist = jnp.asarray(q_pos, jnp.int32)[:, None] - ((blk_c + 1) * CMP_BLOCK - 1)[None, :]
    bias_cmp = jnp.where((c_dist >= 0)[None], _t5_bias(rel, c_dist), NEG_INF)
    o_cmp, imp = cmp_small(q_rows, even_odd(ck), even_odd(cv), head_rows(bias_cmp), t)
    ns_past = nc_past // 2
    imp_s = imp[..., :ns_past] + imp[..., ns_past:]
    imp_s = jnp.concatenate([imp_s, jnp.zeros((b, g, t, ns - ns_past), F32)], axis=-1)
    blk = np.arange(ns)[None, :]
    cur = (q_pos // SEL_BLOCK)[:, None]
    valid = blk * SEL_BLOCK <= q_pos[:, None]
    forced = (blk == 0) | (blk == cur) | (blk == cur - 1)
    score = jnp.where(jnp.asarray(forced), SEL_FORCE, jnp.where(jnp.asarray(valid), imp_s, -1.0))
    sel = topk_mask(score.reshape(b * g * t, ns), min(N_SEL, ns)).reshape(b, g, t, ns)

    qp = jnp.asarray(q_pos, jnp.int32)[:, None]
    n_full = p // SEL_BLOCK
    t_pad = 128
    assert ns - 1 == n_full and t <= SEL_BLOCK

    def slc_bias(dist, visible, sel_keys):
        s = dist.shape[1]
        t5 = head_rows(jnp.where(visible[None], _t5_bias(rel, dist), NEG_INF))
        chosen = jnp.broadcast_to(sel_keys[:, :, :, None, :], (b, g, t, hg, s)).reshape(b, g, t * hg, s)
        return jnp.where(chosen > 0.5, t5[None], NEG_INF)

    d_past = qp - jnp.arange(p, dtype=jnp.int32)[None, :]
    bias_past = slc_bias(d_past, d_past >= 0, jnp.repeat(sel[..., :n_full], SEL_BLOCK, axis=-1))
    d_new = qp - (p + jnp.arange(t_pad, dtype=jnp.int32))[None, :]
    new_ok = (d_new >= 0) & (jnp.arange(t_pad)[None, :] < t)
    bias_new = slc_bias(d_new, new_ok, jnp.broadcast_to(sel[..., ns - 1:ns], (b, g, t, t_pad)))
    pad_rows = lambda a: jnp.concatenate(
        [a.reshape(b, t, g, HEAD_DIM), jnp.zeros((b, t_pad - t, g, HEAD_DIM), F32)], axis=1)
    o_slc = slc_decode(q_rows, bias_past, bias_new, cache_slc_k, cache_slc_v, pad_rows(ks), pad_rows(vs), page_table)

    def head_major(a, rows):
        a = a.reshape(b, t, g, HEAD_DIM).transpose(0, 2, 1, 3)
        return jnp.concatenate([a, jnp.zeros((b, g, rows - t, HEAD_DIM), F32)], axis=2)

    p_w = state_win_k.shape[1]
    s_win = -(-(p_w + t) // 128) * 128
    kw_a = jnp.concatenate([state_win_k.reshape(b, p_w, NSA_KV_W), kw], axis=1)
    vw_a = jnp.concatenate([state_win_v.reshape(b, p_w, NSA_KV_W), vw], axis=1)
    win_major = lambda st, new: jnp.concatenate([st.transpose(0, 2, 1, 3), head_major(new, s_win - p_w)], axis=2)
    w_idx = jnp.arange(s_win, dtype=jnp.int32)[None, :]
    wpos = (p - p_w) + w_idx
    wd = qp - wpos
    wok = (wd >= 0) & (wd < WINDOW) & (wpos >= 0) & (w_idx < p_w + t)
    bias_win = head_rows(jnp.where(wok[None], _t5_bias(rel, wd), NEG_INF))
    bias_win = jnp.broadcast_to(bias_win[None], (b,) + bias_win.shape)
    o_win = flash_small(q_rows, win_major(state_win_k, kw), win_major(state_win_v, vw), bias_win, s_win)

    def to_tokens(o):
        return o.reshape(b, g, t, hg, HEAD_DIM).transpose(0, 2, 1, 3, 4).reshape(m, NSA_W)

    o_nsa = nsa_combine(proj, to_tokens(o_cmp), to_tokens(o_slc), to_tokens(o_win))
    mix = (o_fox.astype(BF16), o_nsa)
    win_len = wts["win_len"]
    new = (fk, fv, lf, kc, vc, ks, vs, kw_a[:, kw_a.shape[1] - win_len:], vw_a[:, vw_a.shape[1] - win_len:])
    return mix, new


def _weight_matmul(wts, key, layer, call):
    cache = wts["bf16"]
    if (key, layer) in cache:
        return call(cache[(key, layer)], None)
    result = call(wts[key], layer)
    if isinstance(result, tuple):
        result, cache[(key, layer)] = result
    return result


def trunk(x, p_emb, wts, lru_conv0, lru_h0, ffn_conv0, past=None):
    b, t, d = x.shape
    m = b * t
    x2 = x.reshape(m, d)
    ffn_bufs = []
    for i in range(2):
        h = rmsnorm(x2, wts["norm_mix"][i], BF16)
        if i == 0:
            if past is None:
                mix, attn_new = attn_prompt(h, b, t, wts)
            else:
                mix, attn_new = attn_sample(h, b, t, wts, *past)
            x2 = _weight_matmul(wts, "w_out0", None, lambda w, _, r=x2: matmul_pair(mix[0], mix[1], w, r))
        else:
            proj1 = _weight_matmul(wts, "w_in1", None, lambda w, _: matmul(h, w)).reshape(b, t, -1)
            tp = -(-t // 8) * 8
            if tp != t:
                proj1 = jnp.concatenate([proj1, jnp.zeros((b, tp - t, proj1.shape[-1]), F32)], axis=1)
            gated, h_last, lru_buf = lru_block(proj1, wts["lru_conv_w"], wts["lru_conv_b"], wts["lru_w_a"],
                                               wts["lru_b_a"], wts["lru_w_x"], wts["lru_b_x"], wts["lru_lambda"],
                                               lru_conv0, lru_h0, t)
            g_in = gated[:, :t].reshape(m, -1)
            x2 = _weight_matmul(wts, "w_out1", None, lambda w, _: matmul(g_in, w, res=x2))
        hf = rmsnorm(x2, wts["norm_ffn"][i], BF16)
        dff = wts["ffn_w_up"].shape[2] // 2
        if t % 8 == 0:
            if ("ffn_w_up", i) not in wts["bf16"]:
                wts["bf16"][("ffn_w_up", i)] = wts["ffn_w_up"][i].astype(BF16)
            act, buf = ffn_up_fused(hf, wts["bf16"][("ffn_w_up", i)], wts["ffn_conv_w"][i], wts["ffn_conv_b"][i],
                                    ffn_conv0[i], t)
        else:
            gu = _weight_matmul(wts, "ffn_w_up", i, lambda w, l: matmul(hf, w, layer=l))
            xp = jnp.concatenate([ffn_conv0[i], gu[:, :dff].reshape(b, t, dff)], axis=1)
            taps = [xp[:, FFN_CONV - 1 - s:FFN_CONV - 1 - s + t].reshape(m, dff) for s in range(FFN_CONV)]
            act = convgate(taps[0], taps[1], taps[2], gu[:, dff:], wts["ffn_conv_w"][i], wts["ffn_conv_b"][i])
            buf = xp[:, xp.shape[1] - (FFN_CONV - 1):]
        ffn_bufs.append(buf)
        x2 = _weight_matmul(wts, "ffn_w_down", i, lambda w, l: matmul(act, w, layer=l, res=x2))
        hp = rmsnorm(x2, wts["ple_norm"][i], BF16)
        p_in = p_emb[i].reshape(m, -1).astype(BF16)
        emb = _weight_matmul(wts, "ple_w_proj", i, lambda w, l: matmul(p_in, w, layer=l))
        x2 = _weight_matmul(wts, "ple_w_gate", i, lambda w, l: matmul(hp, w, layer=l, res=x2, aux=emb))
    y = rmsnorm(x2, wts["final_norm"], F32).reshape(b, t, d)
    return y, attn_new, lru_buf, h_last.reshape(b, -1), jnp.stack(ffn_bufs)


def kernel(x_prompt, x_sample, cache_fox_k, cache_fox_v, cache_fox_lf, cache_cmp_k, cache_cmp_v, cache_slc_k, cache_slc_v, state_win_k, state_win_v, state_lru_conv, state_lru_h, state_ffn_conv, page_table, p_prompt, p_sample, norm_mix, norm_ffn, final_norm, w_in0, b_forget, cmp_w1_k, cmp_pe_k, cmp_w2_k, cmp_w1_v, cmp_pe_v, cmp_w2_v, rel_bias, w_out0, w_in1, lru_conv_w, lru_conv_b, lru_w_a, lru_b_a, lru_w_x, lru_b_x, lru_lambda, w_out1, ffn_w_up, ffn_conv_w, ffn_conv_b, ffn_w_down, ple_w_proj, ple_w_gate, ple_norm):
    depth = norm_mix.shape[0]
    assert depth == 2
    d = x_prompt.shape[-1]
    dff = ffn_w_down.shape[1]
    wts = {
        "norm_mix": norm_mix, "norm_ffn": norm_ffn, "ple_norm": ple_norm, "final_norm": final_norm,
        "w_in0": _pack_w_in0(w_in0), "b_forget": b_forget, "rel_bias": rel_bias,
        "win_len": state_win_k.shape[1],
        "w_out0": w_out0, "w_in1": w_in1, "w_out1": w_out1,
        "lru_conv_w": lru_conv_w, "lru_conv_b": lru_conv_b, "lru_w_a": lru_w_a, "lru_b_a": lru_b_a,
        "lru_w_x": lru_w_x, "lru_b_x": lru_b_x, "lru_lambda": lru_lambda,
        "ffn_w_up": ffn_w_up, "ffn_conv_w": ffn_conv_w, "ffn_conv_b": ffn_conv_b,
        "ffn_w_down": ffn_w_down, "ple_w_proj": ple_w_proj, "ple_w_gate": ple_w_gate,
        "bf16": {},
    }
    wts["cmp_k"] = (cmp_w1_k, cmp_pe_k, cmp_w2_k)
    wts["cmp_v"] = (cmp_w1_v, cmp_pe_v, cmp_w2_v)

    caches = (cache_fox_k, cache_fox_v, cache_fox_lf, cache_cmp_k, cache_cmp_v, cache_slc_k, cache_slc_v)
    y_s, attn_s, lru_conv_s, lru_h_s, ffn_conv_s = trunk(
        x_sample, p_sample, wts, state_lru_conv, state_lru_h, state_ffn_conv,
        past=(caches, page_table, state_win_k, state_win_v))
    bp = x_prompt.shape[0]
    y_p, attn_p, lru_conv_p, lru_h_p, ffn_conv_p = trunk(
        x_prompt, p_prompt, wts,
        jnp.zeros((bp, LRU_CONV - 1, d), F32), jnp.zeros((bp, d), F32),
        jnp.zeros((depth, bp, FFN_CONV - 1, dff), F32))

    def shape_attn(new, b, t):
        fk, fv, lf, kc, vc, ks, vs, wk, wv = new
        h4 = lambda a: a.reshape(b, a.shape[1], FOX_HEADS, HEAD_DIM)
        g4 = lambda a: a.reshape(b, a.shape[1], NSA_KV_HEADS, HEAD_DIM)
        return h4(fk), h4(fv), lf, g4(kc), g4(vc), g4(ks), g4(vs), g4(wk), g4(wv)

    ap = shape_attn(attn_p, bp, x_prompt.shape[1])
    asmp = shape_attn(attn_s, x_sample.shape[0], x_sample.shape[1])
    out = [y_p, y_s]
    for a, s in zip(ap, asmp):
        out += [a, s]
    out += [lru_conv_p, lru_conv_s, lru_h_p, lru_h_s, ffn_conv_p, ffn_conv_s]
    return tuple(out)
```

```python
import functools
import math

import numpy as np
import jax
import jax.numpy as jnp
from jax import lax
from jax.experimental import pallas as pl
from jax.experimental.pallas import tpu as pltpu

PAGE_SIZE = 128
HEAD_DIM = 128
FOX_HEADS = 16
NSA_HEADS = 16
NSA_KV_HEADS = 2
NSA_GROUP = NSA_HEADS // NSA_KV_HEADS
CMP_BLOCK = 32
SEL_BLOCK = 64
N_SEL = 16
WINDOW = 512
SEL_FORCE = 1000.0
N_BUCKETS = 32
MAX_DISTANCE = 1024
LRU_HEADS = 16
LRU_C = 8.0
LRU_CONV = 4
FFN_CONV = 3
EPS = 1e-6
NEG_INF = -1e30
VISIBLE_MIN = -5e29

FOX_W = FOX_HEADS * HEAD_DIM
NSA_W = NSA_HEADS * HEAD_DIM
NSA_KV_W = NSA_KV_HEADS * HEAD_DIM
GROUP_W = NSA_GROUP * HEAD_DIM
SCALE = HEAD_DIM ** -0.5

COL_FQ, COL_FK, COL_FV = 0, FOX_W, 2 * FOX_W
COL_NQ = 3 * FOX_W
COL_KV = COL_NQ + NSA_W
COL_SMALL = COL_KV + 6 * NSA_KV_W
L0_PACKED = -(-(COL_SMALL + 128) // 1024) * 1024

F32 = jnp.float32
BF16 = jnp.bfloat16

VMEM_LIMIT_BYTES = 56 * 1024 * 1024


def _params(*sem):
    return pltpu.CompilerParams(dimension_semantics=sem, vmem_limit_bytes=VMEM_LIMIT_BYTES)


def _dot(a, b):
    return jnp.dot(a, b, preferred_element_type=F32)


def _dot_nt(a, b):
    return lax.dot_general(a, b, (((1,), (1,)), ((), ())), preferred_element_type=F32)


def _gelu(x):
    c = math.sqrt(2.0 / math.pi)
    return 0.5 * x * (1.0 + jnp.tanh(c * (x + 0.044715 * (x * x * x))))


def _sigmoid(x):
    return 1.0 / (1.0 + jnp.exp(-x))


def _online_softmax_step(s, v, m_sc, l_sc, acc_sc):
    reps = s.shape[1] // 128
    m_prev = m_sc[...]
    m_new = jnp.maximum(m_prev, jnp.max(s, axis=-1, keepdims=True))
    alpha = jnp.exp(m_prev - m_new)
    m_wide = m_new if reps == 1 else jnp.concatenate([m_new] * reps, axis=1)
    p = jnp.exp(s - m_wide)
    l_sc[...] = alpha * l_sc[...] + jnp.sum(p, axis=-1, keepdims=True)
    acc_sc[...] = alpha * acc_sc[...] + _dot(p.astype(BF16), v)
    m_sc[...] = m_new


def _bucket_np(dist):
    n = np.maximum(np.asarray(dist, np.int64), 0)
    exact = N_BUCKETS // 2
    nf = np.maximum(n, 1).astype(np.float64)
    large = exact + (np.log(nf / exact) / math.log(MAX_DISTANCE / exact) * (N_BUCKETS - exact)).astype(np.int64)
    return np.where(n < exact, n, np.minimum(large, N_BUCKETS - 1)).astype(np.int32)


def _rmsnorm_kernel(x_ref, g_ref, o_ref):
    x = x_ref[...]
    y = x * lax.rsqrt(jnp.mean(x * x, axis=-1, keepdims=True) + EPS)
    o_ref[...] = (y * g_ref[...]).astype(o_ref.dtype)


def rmsnorm(x, g, out_dtype):
    m, d = x.shape
    tm = min(m, 256)
    return pl.pallas_call(
        _rmsnorm_kernel,
        out_shape=jax.ShapeDtypeStruct((m, d), out_dtype),
        grid=(m // tm,),
        in_specs=[pl.BlockSpec((tm, d), lambda i: (i, 0)), pl.BlockSpec((1, d), lambda i: (0, 0))],
        out_specs=pl.BlockSpec((tm, d), lambda i: (i, 0)),
        compiler_params=_params("parallel"),
        name="rmsnorm",
    )(x, g.reshape(1, d))


def _mm_kernel(*refs, nk, epilogue, emit):
    a_ref, w_ref = refs[0], refs[1]
    n_in = 2 + (epilogue is not None) + (epilogue == "ple")
    o_ref = refs[n_in]
    k = pl.program_id(2)

    @pl.when(k == 0)
    def _():
        o_ref[...] = jnp.zeros_like(o_ref)

    if emit:
        refs[n_in + 1][...] = w_ref[...].astype(BF16)
        o_ref[...] += _dot(a_ref[...], refs[n_in + 1][...])
    else:
        o_ref[...] += _dot(a_ref[...], w_ref[...])

    if epilogue is not None:
        @pl.when(k == nk - 1)
        def _():
            acc = o_ref[...]
            if epilogue == "res":
                o_ref[...] = refs[2][...] + acc
            else:
                o_ref[...] = refs[2][...] + _sigmoid(acc) * refs[3][...]


def matmul(a, w, *, layer=None, res=None, aux=None, tm=1024, tn=1024, tk=2048):
    m, kdim = a.shape
    assert w.shape[-2] == kdim and (layer is None) == (w.ndim == 2)
    n = w.shape[-1]
    tm, tn, tk = min(tm, m), min(tn, n), min(tk, kdim)
    assert m % tm == 0 and n % tn == 0 and kdim % tk == 0
    nk = kdim // tk
    emit = w.dtype != BF16
    assert not emit or m == tm
    epilogue = None if res is None else ("res" if aux is None else "ple")
    if layer is None:
        w_spec = pl.BlockSpec((tk, tn), lambda i, j, k: (k, j))
    else:
        w_spec = pl.BlockSpec((None, tk, tn), lambda i, j, k: (layer, k, j))
    in_specs = [pl.BlockSpec((tm, tk), lambda i, j, k: (i, k)), w_spec]
    args = [a, w]
    for extra in (res, aux):
        if extra is not None:
            in_specs.append(pl.BlockSpec((tm, tn), lambda i, j, k: (i, j)))
            args.append(extra)
    out_shape = jax.ShapeDtypeStruct((m, n), F32)
    out_specs = pl.BlockSpec((tm, tn), lambda i, j, k: (i, j))
    if emit:
        out_shape = (out_shape, jax.ShapeDtypeStruct((kdim, n), BF16))
        out_specs = (out_specs, pl.BlockSpec((tk, tn), lambda i, j, k: (k, j)))
    return pl.pallas_call(
        functools.partial(_mm_kernel, nk=nk, epilogue=epilogue, emit=emit),
        out_shape=out_shape,
        grid=(m // tm, n // tn, nk),
        in_specs=in_specs,
        out_specs=out_specs,
        compiler_params=_params("parallel", "parallel", "arbitrary"),
        name="matmul_" + (epilogue or "plain") + ("_cast" if emit else ""),
    )(*args)


def _mm_pair_kernel(a0_ref, a1_ref, w_ref, res_ref, o_ref, *wb_ref):
    k = pl.program_id(2)
    if wb_ref:
        wb_ref[0][...] = w_ref[...].astype(BF16)
        w_ref = wb_ref[0]

    @pl.when(k == 0)
    def _():
        o_ref[...] = res_ref[...] + _dot(a0_ref[...], w_ref[...])

    @pl.when(k == 1)
    def _():
        o_ref[...] += _dot(a1_ref[...], w_ref[...])


def matmul_pair(a0, a1, w, res, *, tm=1024, tn=1024):
    m, k0 = a0.shape
    assert a1.shape == (m, k0) and w.shape[0] == 2 * k0
    n = w.shape[1]
    tm, tn = min(tm, m), min(tn, n)
    assert m % tm == 0 and n % tn == 0
    emit = w.dtype != BF16
    assert not emit or m == tm
    piece = pl.BlockSpec((tm, k0), lambda i, j, k: (i, 0))
    tile = pl.BlockSpec((tm, tn), lambda i, j, k: (i, j))
    w_tile = pl.BlockSpec((k0, tn), lambda i, j, k: (k, j))
    out_shape, out_specs = jax.ShapeDtypeStruct((m, n), F32), tile
    if emit:
        out_shape, out_specs = (out_shape, jax.ShapeDtypeStruct(w.shape, BF16)), (tile, w_tile)
    return pl.pallas_call(
        _mm_pair_kernel,
        out_shape=out_shape,
        grid=(m // tm, n // tn, 2),
        in_specs=[piece, piece, w_tile, tile],
        out_specs=out_specs,
        compiler_params=_params("parallel", "parallel", "arbitrary"),
        name="matmul_pair" + ("_cast" if emit else ""),
    )(a0, a1, w, res)


def _ffn_up_kernel(a_ref, wg_ref, wu_ref, cw_ref, cb_ref, buf_ref, act_ref, tail_ref,
                   accg, accu, gbuf, *, ni, tiles_per_seq, tm):
    i = pl.program_id(1)
    k = pl.program_id(2)
    kw = FFN_CONV
    cur = i % 2
    first = k == 0

    def epilogue():
        prev = 1 - cur

        @pl.when((i - 1) % tiles_per_seq == 0)
        def _():
            gbuf[8 - (kw - 1):8, :] = buf_ref[0]

        g = accg[prev]
        gbuf[8:8 + tm, :] = g
        y = g * cw_ref[kw - 1:kw, :] + cb_ref[...]
        for t in range(kw - 1):
            off = 8 - (kw - 1) + t
            y = y + gbuf[off:off + tm, :] * cw_ref[t:t + 1, :]
        act_ref[...] = (_gelu(y) * accu[prev]).astype(act_ref.dtype)
        tail = gbuf[8 + tm - (kw - 1):8 + tm, :]
        gbuf[8 - (kw - 1):8, :] = tail
        tail_ref[0] = tail

    def start_tile():
        a = a_ref[...]
        accg[cur] = _dot(a, wg_ref[...])
        accu[cur] = _dot(a, wu_ref[...])

    @pl.when(first & (i == 0))
    def _():
        start_tile()

    @pl.when(first & (i > 0) & (i < ni))
    def _():
        epilogue()
        start_tile()

    @pl.when(first & (i == ni))
    def _():
        epilogue()

    @pl.when(jnp.logical_not(first) & (i < ni))
    def _():
        a = a_ref[...]
        accg[cur] += _dot(a, wg_ref[...])
        accu[cur] += _dot(a, wu_ref[...])


def ffn_up_fused(a, w_up, conv_w, conv_b, conv_buf, seq_len, *, tm=1024, tn=1024, tk=2048):
    m, kdim = a.shape
    dff = w_up.shape[1] // 2
    tm, tn, tk = min(tm, seq_len), min(tn, dff), min(tk, kdim)
    assert seq_len % tm == 0 and dff % tn == 0 and kdim % tk == 0 and tm % 8 == 0
    nb = m // seq_len
    tps = seq_len // tm
    nj, nk = dff // tn, kdim // tk
    ni = m // tm
    kw = FFN_CONV
    kern = functools.partial(_ffn_up_kernel, ni=ni, tiles_per_seq=tps, tm=tm)
    in_k = lambda i, k: jnp.where(i == ni, nk - 1, k)
    in_i = lambda i: jnp.minimum(i, ni - 1)
    out_i = lambda i: jnp.maximum(i - 1, 0)
    return pl.pallas_call(
        kern,
        out_shape=(jax.ShapeDtypeStruct((m, dff), BF16), jax.ShapeDtypeStruct((nb, kw - 1, dff), F32)),
        grid=(nj, ni + 1, nk),
        in_specs=[
            pl.BlockSpec((tm, tk), lambda j, i, k: (in_i(i), in_k(i, k))),
            pl.BlockSpec((tk, tn), lambda j, i, k: (in_k(i, k), j)),
            pl.BlockSpec((tk, tn), lambda j, i, k: (in_k(i, k), j + nj)),
            pl.BlockSpec((kw, tn), lambda j, i, k: (0, j)),
            pl.BlockSpec((1, tn), lambda j, i, k: (0, j)),
            pl.BlockSpec((1, kw - 1, tn), lambda j, i, k: (out_i(i) // tps, 0, j)),
        ],
        out_specs=(
            pl.BlockSpec((tm, tn), lambda j, i, k: (out_i(i), j)),
            pl.BlockSpec((1, kw - 1, tn), lambda j, i, k: (out_i(i) // tps, 0, j)),
        ),
        scratch_shapes=[pltpu.VMEM((2, tm, tn), F32), pltpu.VMEM((2, tm, tn), F32), pltpu.VMEM((8 + tm, tn), F32)],
        compiler_params=_params("parallel", "arbitrary", "arbitrary"),
        name="ffn_up_fused",
    )(a, w_up, w_up, conv_w, conv_b.reshape(1, dff), conv_buf)


def _convgate_kernel(s0_ref, s1_ref, s2_ref, u_ref, cw_ref, cb_ref, o_ref):
    y = s0_ref[...] * cw_ref[2:3, :] + cb_ref[...]
    y = y + s2_ref[...] * cw_ref[0:1, :]
    y = y + s1_ref[...] * cw_ref[1:2, :]
    o_ref[...] = (_gelu(y) * u_ref[...]).astype(o_ref.dtype)


def convgate(s0, s1, s2, u, conv_w, conv_b, *, tn=2048):
    m, dff = s0.shape
    tn = min(tn, dff)
    row = pl.BlockSpec((m, tn), lambda j: (0, j))
    return pl.pallas_call(
        _convgate_kernel,
        out_shape=jax.ShapeDtypeStruct((m, dff), BF16),
        grid=(dff // tn,),
        in_specs=[row, row, row, row, pl.BlockSpec((FFN_CONV, tn), lambda j: (0, j)),
                  pl.BlockSpec((1, tn), lambda j: (0, j))],
        out_specs=row,
        compiler_params=_params("parallel"),
        name="convgate",
    )(s0, s1, s2, u, conv_w, conv_b.reshape(1, dff))


def _fox_prompt_kernel(q_ref, k_ref, v_ref, cq_ref, ck_ref, o_ref, m_sc, l_sc, acc_sc, cq_sc, *, nk, tq, tk):
    qi = pl.program_id(2)
    kj = pl.program_id(3)

    @pl.when(kj == 0)
    def _():
        m_sc[...] = jnp.full_like(m_sc, NEG_INF)
        l_sc[...] = jnp.zeros_like(l_sc)
        acc_sc[...] = jnp.zeros_like(acc_sc)
        cq_sc[...] = jnp.transpose(jnp.broadcast_to(cq_ref[0, 0], (128, tq)))

    @pl.when(kj * tk <= qi * tq + tq - 1)
    def _():
        q = q_ref[0].astype(BF16)
        k = k_ref[0].astype(BF16)
        cq = jnp.concatenate([cq_sc[...]] * (tk // 128), axis=1)
        s = _dot_nt(q, k) * SCALE + (cq - ck_ref[0, 0])
        row = qi * tq + lax.broadcasted_iota(jnp.int32, (tq, tk), 0)
        col = kj * tk + lax.broadcasted_iota(jnp.int32, (tq, tk), 1)
        s = jnp.where(row >= col, s, NEG_INF)
        _online_softmax_step(s, v_ref[0].astype(BF16), m_sc, l_sc, acc_sc)

    @pl.when(kj == nk - 1)
    def _():
        o_ref[0] = (acc_sc[...] / jnp.maximum(l_sc[...], 1e-30)).astype(o_ref.dtype)


def fox_prompt(proj3, c_new):
    b, t, _ = proj3.shape
    tq = min(512, t)
    tk = min(1024, t)
    assert tq % 128 == 0 and tk % 128 == 0
    nq, nk = t // tq, t // tk
    h = FOX_HEADS
    c_row = c_new.transpose(0, 2, 1).reshape(b, h, 1, t)
    kq, kk, kv = COL_FQ // HEAD_DIM, COL_FK // HEAD_DIM, COL_FV // HEAD_DIM

    def kidx(qi, kj):
        return jnp.minimum(kj, (qi * tq + tq - 1) // tk)

    kern = functools.partial(_fox_prompt_kernel, nk=nk, tq=tq, tk=tk)
    return pl.pallas_call(
        kern,
        out_shape=jax.ShapeDtypeStruct((b, t, FOX_W), BF16),
        grid=(b, h, nq, nk),
        in_specs=[
            pl.BlockSpec((1, tq, HEAD_DIM), lambda bi, hi, qi, kj: (bi, qi, kq + hi)),
            pl.BlockSpec((1, tk, HEAD_DIM), lambda bi, hi, qi, kj: (bi, kidx(qi, kj), kk + hi)),
            pl.BlockSpec((1, tk, HEAD_DIM), lambda bi, hi, qi, kj: (bi, kidx(qi, kj), kv + hi)),
            pl.BlockSpec((1, 1, 1, tq), lambda bi, hi, qi, kj: (bi, hi, 0, qi)),
            pl.BlockSpec((1, 1, 1, tk), lambda bi, hi, qi, kj: (bi, hi, 0, kidx(qi, kj))),
        ],
        out_specs=pl.BlockSpec((1, tq, HEAD_DIM), lambda bi, hi, qi, kj: (bi, qi, hi)),
        scratch_shapes=[pltpu.VMEM((tq, 128), F32), pltpu.VMEM((tq, 128), F32), pltpu.VMEM((tq, HEAD_DIM), F32),
                        pltpu.VMEM((tq, 128), F32)],
        compiler_params=_params("parallel", "parallel", "parallel", "arbitrary"),
        name="fox_prompt",
    )(proj3, proj3, proj3, c_row, c_row)


def _compress_kernel(x_ref, pe_ref, w1_ref, w2_ref, o_ref, *, tr):
    acc = jnp.zeros((tr, HEAD_DIM), F32)
    for l in range(CMP_BLOCK):
        xl = x_ref[pl.ds(l, tr, stride=CMP_BLOCK), :] + pe_ref[l:l + 1, :]
        acc = acc + _dot(xl.astype(BF16), w1_ref[l])
    o_ref[0] = _dot(_gelu(acc).astype(BF16), w2_ref[...])


def compress(x2d, col_block, n_groups, cmp_w):
    w1, pe, w2 = cmp_w
    rows = x2d.shape[0]
    r = rows // CMP_BLOCK
    tr = min(r, 256)
    assert r % tr == 0 and rows % CMP_BLOCK == 0
    return pl.pallas_call(
        functools.partial(_compress_kernel, tr=tr),
        out_shape=jax.ShapeDtypeStruct((n_groups, r, HEAD_DIM), F32),
        grid=(n_groups, r // tr),
        in_specs=[pl.BlockSpec((tr * CMP_BLOCK, HEAD_DIM), lambda g, i: (i, col_block + g)),
                  pl.BlockSpec((CMP_BLOCK, HEAD_DIM), lambda g, i: (0, 0)),
                  pl.BlockSpec((CMP_BLOCK, HEAD_DIM, HEAD_DIM), lambda g, i: (0, 0, 0)),
                  pl.BlockSpec((HEAD_DIM, HEAD_DIM), lambda g, i: (0, 0))],
        out_specs=pl.BlockSpec((1, tr, HEAD_DIM), lambda g, i: (g, i, 0)),
        compiler_params=_params("parallel", "parallel"),
        name="nsa_compress",
    )(x2d, pe, w1.astype(BF16), w2.astype(BF16))


def _compress_pool_kernel(x_ref, pe_ref, w1_ref, w2_ref, o_ref, *, tr):
    for g in range(x_ref.shape[1]):
        acc = jnp.zeros((tr, HEAD_DIM), F32)
        for l in range(0, CMP_BLOCK, 2):
            xa = x_ref[pl.ds(l, tr, stride=CMP_BLOCK), g, :] + pe_ref[l:l + 1, :]
            xb = x_ref[pl.ds(l + 1, tr, stride=CMP_BLOCK), g, :] + pe_ref[l + 1:l + 2, :]
            acc = acc + _dot(jnp.concatenate([xa, xb], axis=1).astype(BF16), w1_ref[l // 2])
        o_ref[:, g * HEAD_DIM:(g + 1) * HEAD_DIM] = _dot(_gelu(acc).astype(BF16), w2_ref[...])


def compress_pool(pool, cmp_w):
    w1, pe, w2 = cmp_w
    n_phys, page, g, d = pool.shape
    r = n_phys * page // CMP_BLOCK
    tr = max(c for c in range(8, 257, 8) if r % c == 0)
    return pl.pallas_call(
        functools.partial(_compress_pool_kernel, tr=tr),
        out_shape=jax.ShapeDtypeStruct((r, g * d), F32),
        grid=(r // tr,),
        in_specs=[pl.BlockSpec((tr * CMP_BLOCK, g, d), lambda i: (i, 0, 0)),
                  pl.BlockSpec((CMP_BLOCK, d), lambda i: (0, 0)),
                  pl.BlockSpec((CMP_BLOCK // 2, 2 * d, d), lambda i: (0, 0, 0)),
                  pl.BlockSpec((d, d), lambda i: (0, 0))],
        out_specs=pl.BlockSpec((tr, g * d), lambda i: (i, 0)),
        compiler_params=_params("parallel"),
        name="nsa_compress_pool",
    )(pool.reshape(n_phys * page, g, d), pe, w1.astype(BF16).reshape(CMP_BLOCK // 2, 2 * d, d), w2.astype(BF16))


def _cmp_prompt_kernel(q_ref, ck_ref, cv_ref, bias_ref, exp_ref, o_ref, sel_ref, *, tq, nc, n_top):
    qi = pl.program_id(2)
    ns = nc // 2
    t_col = qi * tq + lax.broadcasted_iota(jnp.int32, (tq, nc), 0)
    lane = lax.broadcasted_iota(jnp.int32, (tq, nc), 1)
    blk_c = jnp.where(lane < ns, 2 * lane, 2 * (lane - ns) + 1)
    mask = t_col >= (blk_c + 1) * CMP_BLOCK - 1
    ck = ck_ref[0, 0].astype(BF16)
    cv = cv_ref[0, 0].astype(BF16)
    imp = jnp.zeros((tq, nc), F32)
    for hg in range(NSA_GROUP):
        sl = slice(hg * HEAD_DIM, (hg + 1) * HEAD_DIM)
        s = _dot_nt(q_ref[0, :, sl].astype(BF16), ck) * SCALE + bias_ref[hg]
        s = jnp.where(mask, s, NEG_INF)
        m = jnp.max(s, axis=-1, keepdims=True)
        e = jnp.where(mask, jnp.exp(s - m), 0.0)
        p = e / jnp.maximum(jnp.sum(e, axis=-1, keepdims=True), 1e-30)
        imp = imp + p
        o_ref[0, :, sl] = _dot(p.astype(BF16), cv)
    imp_s = imp[:, :ns] + imp[:, ns:]
    t_s = qi * tq + lax.broadcasted_iota(jnp.int32, (tq, ns), 0)
    blk = lax.broadcasted_iota(jnp.int32, (tq, ns), 1)
    cur = t_s // SEL_BLOCK
    valid = blk * SEL_BLOCK <= t_s
    forced = (blk == 0) | (blk == cur) | (blk == cur - 1)
    score = jnp.where(forced, SEL_FORCE, jnp.where(valid, imp_s, -1.0))
    rank = jnp.zeros((tq, ns), F32)
    for j in range(ns):
        cj = score[:, j:j + 1]
        ahead = (cj > score) | ((cj == score) & (blk > j))
        rank = rank + jnp.where(ahead, 1.0, 0.0)
    sel = jnp.where(rank < n_top, 1.0, 0.0).astype(BF16)
    sel_ref[0, 0] = (1.0 - _dot(sel, exp_ref[...])) * NEG_INF


def cmp_prompt(proj3, ck_perm, cv_perm, bias_cmp):
    b, t, _ = proj3.shape
    nc = t // CMP_BLOCK
    ns = t // SEL_BLOCK
    tq = min(256, t)
    g = NSA_KV_HEADS
    n_top = min(N_SEL, ns)
    expand = jnp.asarray(np.repeat(np.eye(ns, dtype=np.float32), SEL_BLOCK, axis=1), BF16)
    kq = COL_NQ // GROUP_W
    kern = functools.partial(_cmp_prompt_kernel, tq=tq, nc=nc, n_top=n_top)
    return pl.pallas_call(
        kern,
        out_shape=(jax.ShapeDtypeStruct((b, t, NSA_W), F32), jax.ShapeDtypeStruct((b, g, t, t), F32)),
        grid=(b, g, t // tq),
        in_specs=[
            pl.BlockSpec((1, tq, GROUP_W), lambda bi, gi, qi: (bi, qi, kq + gi)),
            pl.BlockSpec((1, 1, nc, HEAD_DIM), lambda bi, gi, qi: (gi, bi, 0, 0)),
            pl.BlockSpec((1, 1, nc, HEAD_DIM), lambda bi, gi, qi: (gi, bi, 0, 0)),
            pl.BlockSpec((NSA_GROUP, tq, nc), lambda bi, gi, qi: (gi, qi, 0)),
            pl.BlockSpec((ns, t), lambda bi, gi, qi: (0, 0)),
        ],
        out_specs=(
            pl.BlockSpec((1, tq, GROUP_W), lambda bi, gi, qi: (bi, qi, gi)),
            pl.BlockSpec((1, 1, tq, t), lambda bi, gi, qi: (bi, gi, qi, 0)),
        ),
        compiler_params=_params("parallel", "parallel", "parallel"),
        name="nsa_cmp_prompt",
    )(proj3, ck_perm, cv_perm, bias_cmp, expand)


NSA_TQ = 128
NSA_TK = 512


def _nsa_band_kernel(*refs, nr, ratio, o_max, mode, has_sel, combine):
    q_ref, k_ref, v_ref, bias_ref = refs[:4]
    o_ref, qs, m_sc, l_sc, acc_sc = refs[-5:]
    extra = list(refs[4:-5])
    sel_ref = extra.pop(0) if has_sel else None
    small_ref, oc_ref, os_ref = extra if combine else (None, None, None)
    qi = pl.program_id(2)
    r = pl.program_id(3)
    tq = NSA_TQ
    tk = k_ref.shape[1]
    hg_n = NSA_GROUP

    @pl.when(r == 0)
    def _():
        for hg in range(hg_n):
            qs[hg * tq:(hg + 1) * tq, :] = q_ref[0, :, hg * HEAD_DIM:(hg + 1) * HEAD_DIM].astype(BF16)
        m_sc[...] = jnp.full_like(m_sc, NEG_INF)
        l_sc[...] = jnp.zeros_like(l_sc)
        acc_sc[...] = jnp.zeros_like(acc_sc)

    if mode == "causal":
        active = r <= qi // ratio
        kj = jnp.minimum(r, qi // ratio)
    else:
        active = qi // ratio - (nr - 1) + r >= 0
        kj = jnp.maximum(qi // ratio - (nr - 1) + r, 0)
    chunk0 = o_max - jnp.clip(qi - ratio * kj, 0, o_max)

    @pl.when(active)
    def _():
        s = _dot_nt(qs[...], k_ref[0].astype(BF16)) * SCALE
        bias = jnp.concatenate([bias_ref[:, chunk0 + u] for u in range(ratio)], axis=-1)
        s3 = s.reshape(hg_n, tq, tk) + bias
        if has_sel:
            s3 = s3 + sel_ref[0, 0][None]
        _online_softmax_step(s3.reshape(hg_n * tq, tk), v_ref[0].astype(BF16), m_sc, l_sc, acc_sc)

    @pl.when(r == nr - 1)
    def _():
        o = acc_sc[...] / jnp.maximum(l_sc[...], 1e-30)
        if combine:
            gates = _sigmoid(small_ref[0])
            second_group = pl.program_id(1) == 1
        for hg in range(hg_n):
            sl = slice(hg * HEAD_DIM, (hg + 1) * HEAD_DIM)
            o_h = o[hg * tq:(hg + 1) * tq, :]
            if combine:
                def gate(branch):
                    c = FOX_HEADS + branch * NSA_HEADS + hg
                    return jnp.where(second_group, gates[:, c + hg_n:c + hg_n + 1], gates[:, c:c + 1])

                o_h = gate(0) * oc_ref[0, :, sl] + gate(1) * os_ref[0, :, sl] + gate(2) * o_h
            o_ref[0, :, sl] = o_h.astype(o_ref.dtype)


def nsa_band(proj3, bias_strip, tk, col_k, col_v, mode, selmask=None, combine_with=None):
    b, t, _ = proj3.shape
    tq = NSA_TQ
    assert t % tk == 0 and tk % tq == 0
    ratio = tk // tq
    nq, nk = t // tq, t // tk
    g = NSA_KV_HEADS
    n_chunks = bias_strip.shape[1]
    o_max = n_chunks - ratio
    nr = nk if mode == "causal" else -(-WINDOW // tk) + 1
    assert mode == "causal" or o_max == ratio * nr - 1
    kq = COL_NQ // GROUP_W
    ck, cv = col_k // HEAD_DIM, col_v // HEAD_DIM
    if mode == "causal":
        def ktile(qi, r):
            return jnp.minimum(r, qi // ratio)
    else:
        def ktile(qi, r):
            return jnp.maximum(qi // ratio - (nr - 1) + r, 0)

    in_specs = [
        pl.BlockSpec((1, tq, GROUP_W), lambda bi, gi, qi, r: (bi, qi, kq + gi)),
        pl.BlockSpec((1, tk, HEAD_DIM), lambda bi, gi, qi, r: (bi, ktile(qi, r), ck + gi)),
        pl.BlockSpec((1, tk, HEAD_DIM), lambda bi, gi, qi, r: (bi, ktile(qi, r), cv + gi)),
        pl.BlockSpec((NSA_GROUP, n_chunks, tq, 128), lambda bi, gi, qi, r: (gi, 0, 0, 0)),
    ]
    args = [proj3, proj3, proj3, bias_strip]
    if selmask is not None:
        in_specs.append(pl.BlockSpec((1, 1, tq, tk), lambda bi, gi, qi, r: (bi, gi, qi, ktile(qi, r))))
        args.append(selmask)
    head_tile = pl.BlockSpec((1, tq, GROUP_W), lambda bi, gi, qi, r: (bi, qi, gi))
    if combine_with is not None:
        assert g == 2
        in_specs += [pl.BlockSpec((1, tq, 128), lambda bi, gi, qi, r: (bi, qi, COL_SMALL // 128)), head_tile, head_tile]
        args += [proj3, combine_with[0], combine_with[1]]
    kern = functools.partial(_nsa_band_kernel, nr=nr, ratio=ratio, o_max=o_max, mode=mode,
                             has_sel=selmask is not None, combine=combine_with is not None)
    rows = NSA_GROUP * tq
    return pl.pallas_call(
        kern,
        out_shape=jax.ShapeDtypeStruct((b, t, NSA_W), F32 if combine_with is None else BF16),
        grid=(b, g, nq, nr),
        in_specs=in_specs,
        out_specs=pl.BlockSpec((1, tq, GROUP_W), lambda bi, gi, qi, r: (bi, qi, gi)),
        scratch_shapes=[pltpu.VMEM((rows, HEAD_DIM), BF16), pltpu.VMEM((rows, 128), F32),
                        pltpu.VMEM((rows, 128), F32), pltpu.VMEM((rows, HEAD_DIM), F32)],
        compiler_params=_params("parallel", "parallel", "parallel", "arbitrary"),
        name="nsa_band_" + mode,
    )(*args)


_T5_THRESHOLDS = tuple(int(np.searchsorted(_bucket_np(np.arange(2 * MAX_DISTANCE)), k)) for k in range(1, N_BUCKETS))


def _t5_bias(rel_bias, dist):
    per_head = lambda k: rel_bias[k].reshape((-1,) + (1,) * dist.ndim)
    d = dist[None]
    out = jnp.broadcast_to(per_head(0), (rel_bias.shape[1],) + dist.shape)
    for k, thr in enumerate(_T5_THRESHOLDS, start=1):
        out = jnp.where(d >= thr, per_head(k), out)
    return out


def _band_strip(rel_bias, o_max, ratio, window=None):
    c = jnp.arange(o_max + ratio, dtype=jnp.int32)[:, None, None]
    i = jnp.arange(NSA_TQ, dtype=jnp.int32)[None, :, None]
    jj = jnp.arange(128, dtype=jnp.int32)[None, None, :]
    d = i - (c * 128 + jj) + o_max * NSA_TQ
    ok = d >= 0
    if window is not None:
        ok = ok & (d < window)
    return jnp.where(ok[None], _t5_bias(rel_bias, d), NEG_INF)


def _combine_kernel(small_ref, oc_ref, os_ref, ow_ref, o_ref):
    gates = _sigmoid(small_ref[...])
    for h in range(NSA_HEADS):
        sl = slice(h * HEAD_DIM, (h + 1) * HEAD_DIM)
        base = FOX_HEADS + h
        g0 = gates[:, base:base + 1]
        g1 = gates[:, base + NSA_HEADS:base + NSA_HEADS + 1]
        g2 = gates[:, base + 2 * NSA_HEADS:base + 2 * NSA_HEADS + 1]
        o = g0 * oc_ref[:, sl] + g1 * os_ref[:, sl] + g2 * ow_ref[:, sl]
        o_ref[:, sl] = o.astype(o_ref.dtype)


def nsa_combine(proj, o_cmp, o_slc, o_win):
    m = proj.shape[0]
    tm = min(m, 256)
    wide = pl.BlockSpec((tm, NSA_W), lambda i: (i, 0))
    return pl.pallas_call(
        _combine_kernel,
        out_shape=jax.ShapeDtypeStruct((m, NSA_W), BF16),
        grid=(m // tm,),
        in_specs=[pl.BlockSpec((tm, 128), lambda i: (i, COL_SMALL // 128)), wide, wide, wide],
        out_specs=wide,
        compiler_params=_params("parallel"),
        name="nsa_combine",
    )(proj, o_cmp, o_slc, o_win)


def _lru_kernel(xr_ref, gate_ref, cw_ref, cb_ref, wa_ref, wx_ref, ba_ref, bx_ref, lam_ref, buf_ref, h0_ref,
                y_ref, hlast_ref, nbuf_ref, xbuf, hcar, hsbuf, *, nt, tt, last_row):
    ti = pl.program_id(2)
    kw = LRU_CONV

    @pl.when(ti == 0)
    def _():
        xbuf[8 - (kw - 1):8, :] = buf_ref[0]
        hcar[...] = h0_ref[0]

    x = xr_ref[0]
    xbuf[8:8 + tt, :] = x
    xc = x * cw_ref[kw - 1:kw, :] + cb_ref[...]
    for t in range(kw - 1):
        off = 8 - (kw - 1) + t
        xc = xc + xbuf[off:off + tt, :] * cw_ref[t:t + 1, :]
    xcb = xc.astype(BF16)
    r = _sigmoid(_dot(xcb, wa_ref[0]) + ba_ref[...])
    ig = _sigmoid(_dot(xcb, wx_ref[0]) + bx_ref[...])
    neg_lam = -lam_ref[...]
    softplus = jnp.maximum(neg_lam, 0.0) + jnp.log1p(jnp.exp(-jnp.abs(neg_lam)))
    log_a = -LRU_C * r * softplus
    a = jnp.exp(log_a)
    th = jnp.tanh(log_a)
    u = jnp.sqrt(-2.0 * th / (1.0 - th)) * (ig * xc)
    ng = tt // 8
    a = a.reshape(ng, 8, a.shape[-1])
    u = u.reshape(ng, 8, u.shape[-1])
    sub = lax.broadcasted_iota(jnp.int32, a.shape, 1)
    for step in (1, 2, 4):
        keep = sub >= step
        a_sh = jnp.where(keep, pltpu.roll(a, step, 1), 1.0)
        u_sh = jnp.where(keep, pltpu.roll(u, step, 1), 0.0)
        u = a * u_sh + u
        a = a * a_sh
    h = hcar[...]
    for gi in range(ng):
        hs_g = a[gi] * h + u[gi]
        hsbuf[gi * 8:(gi + 1) * 8, :] = hs_g
        h = hs_g[7:8, :]
    hcar[...] = h
    y_ref[0] = (_gelu(gate_ref[0]) * hsbuf[...]).astype(y_ref.dtype)

    @pl.when(ti == nt - 1)
    def _():
        hlast_ref[0] = hsbuf[last_row:last_row + 1, :]
        nbuf_ref[0] = xbuf[8 + last_row - (kw - 2):8 + last_row + 1, :]

    xbuf[8 - (kw - 1):8, :] = xbuf[8 + tt - (kw - 1):8 + tt, :]


def lru_block(proj1, conv_w, conv_b, w_a, b_a, w_x, b_x, lam, conv_buf, h0, t_valid):
    b, tp, w2 = proj1.shape
    w = w2 // 2
    nh = LRU_HEADS
    bw = w // nh
    tt = min(tp, 256)
    assert tp % tt == 0 and tt % 8 == 0
    nt = tp // tt
    last_row = (t_valid - 1) - (nt - 1) * tt
    assert 0 <= last_row < tt and (nt > 1 or last_row >= 0)
    kw = LRU_CONV
    vec = lambda a: a.reshape(1, w)
    chan = pl.BlockSpec((1, bw), lambda bi, hi, ti: (0, hi))
    kern = functools.partial(_lru_kernel, nt=nt, tt=tt, last_row=last_row)
    return pl.pallas_call(
        kern,
        out_shape=(jax.ShapeDtypeStruct((b, tp, w), BF16), jax.ShapeDtypeStruct((b, 1, w), F32),
                   jax.ShapeDtypeStruct((b, kw - 1, w), F32)),
        grid=(b, nh, nt),
        in_specs=[
            pl.BlockSpec((1, tt, bw), lambda bi, hi, ti: (bi, ti, nh + hi)),
            pl.BlockSpec((1, tt, bw), lambda bi, hi, ti: (bi, ti, hi)),
            pl.BlockSpec((kw, bw), lambda bi, hi, ti: (0, hi)),
            chan,
            pl.BlockSpec((1, bw, bw), lambda bi, hi, ti: (hi, 0, 0)),
            pl.BlockSpec((1, bw, bw), lambda bi, hi, ti: (hi, 0, 0)),
            chan, chan, chan,
            pl.BlockSpec((1, kw - 1, bw), lambda bi, hi, ti: (bi, 0, hi)),
            pl.BlockSpec((1, 1, bw), lambda bi, hi, ti: (bi, 0, hi)),
        ],
        out_specs=(
            pl.BlockSpec((1, tt, bw), lambda bi, hi, ti: (bi, ti, hi)),
            pl.BlockSpec((1, 1, bw), lambda bi, hi, ti: (bi, 0, hi)),
            pl.BlockSpec((1, kw - 1, bw), lambda bi, hi, ti: (bi, 0, hi)),
        ),
        scratch_shapes=[pltpu.VMEM((8 + tt, bw), F32), pltpu.VMEM((1, bw), F32), pltpu.VMEM((tt, bw), F32)],
        compiler_params=_params("parallel", "parallel", "arbitrary"),
        name="rglru",
    )(proj1, proj1, conv_w, vec(conv_b), w_a.astype(BF16), w_x.astype(BF16), vec(b_a), vec(b_x), vec(lam),
      conv_buf, h0.reshape(b, 1, w))


GATHER_PAGES = 8


def _gather_kernel(pt_ref, *refs, n_pools, pg):
    ins, outs = refs[:n_pools * pg], refs[n_pools * pg:]
    for c in range(n_pools):
        for i in range(pg):
            src = ins[c * pg + i]
            rows = src.shape[1]
            outs[c][0, i * rows:(i + 1) * rows, :] = src[0]


def gather_pools(pools, page_table):
    b, n_pages = page_table.shape
    pg = math.gcd(GATHER_PAGES, n_pages)
    in_specs, args, out_shape, out_specs = [], [], [], []
    for pool in pools:
        _, rows, c = pool.shape
        assert (pg * rows) % 8 == 0
        for i in range(pg):
            in_specs.append(pl.BlockSpec((1, rows, c), lambda bi, j, pt, i=i: (pt[bi, j * pg + i], 0, 0)))
            args.append(pool)
        out_shape.append(jax.ShapeDtypeStruct((b, n_pages * rows, c), pool.dtype))
        out_specs.append(pl.BlockSpec((1, pg * rows, c), lambda bi, j, pt: (bi, j, 0)))
    out_shape, out_specs = tuple(out_shape), tuple(out_specs)
    return pl.pallas_call(
        functools.partial(_gather_kernel, n_pools=len(pools), pg=pg),
        out_shape=out_shape,
        grid_spec=pltpu.PrefetchScalarGridSpec(
            num_scalar_prefetch=1, grid=(b, n_pages // pg), in_specs=in_specs, out_specs=out_specs),
        compiler_params=_params("parallel", "arbitrary"),
        name="gather_pages",
    )(page_table, *args)


FOX_DECODE_PAGES = 8


def _fox_decode_kernel(pt_ref, q_ref, cq_ref, ck_ref, cknew_ref, *refs, n_steps, pf):
    kp, vp = refs[:pf], refs[pf:2 * pf]
    knew_ref, vnew_ref, o_ref, m_sc, l_sc, acc_sc, cq_sc = refs[2 * pf:]
    j = pl.program_id(1)
    rows = q_ref.shape[1]
    nh = FOX_HEADS
    w = PAGE_SIZE * nh

    @pl.when(j == 0)
    def _():
        m_sc[...] = jnp.full_like(m_sc, NEG_INF)
        l_sc[...] = jnp.zeros_like(l_sc)
        acc_sc[...] = jnp.zeros_like(acc_sc)
        col = lax.broadcasted_iota(jnp.int32, (rows, w), 1)
        row = lax.broadcasted_iota(jnp.int32, (rows, w), 0)
        cq_sc[...] = jnp.where((col % nh) == (row % nh), cq_ref[0], NEG_INF)

    def update(k3, v3, ck_row, causal):
        n = k3.shape[0] * nh
        k2 = k3.reshape(n, HEAD_DIM).astype(BF16)
        v2 = v3.reshape(n, HEAD_DIM).astype(BF16)
        s = _dot_nt(q_ref[0], k2) * SCALE + (cq_sc[:, :n] - ck_row)
        if causal:
            col = lax.broadcasted_iota(jnp.int32, (rows, n), 1)
            row = lax.broadcasted_iota(jnp.int32, (rows, n), 0)
            s = jnp.where(col // nh <= row // nh, s, NEG_INF)
        _online_softmax_step(s, v2, m_sc, l_sc, acc_sc)

    @pl.when(j < n_steps)
    def _():
        for i in range(pf):
            update(kp[i][0], vp[i][0], ck_ref[0, :, i * w:(i + 1) * w], False)

    @pl.when(j == n_steps)
    def _():
        update(knew_ref[0], vnew_ref[0], cknew_ref[0], True)
        o_ref[0] = acc_sc[...] / jnp.maximum(l_sc[...], 1e-30)


def fox_decode(q_rows, cq_rows, ck_past, ck_new, k_pool, v_pool, k_new, v_new, page_table):
    b, n_pages = page_table.shape
    pf = math.gcd(FOX_DECODE_PAGES, n_pages)
    n_steps = n_pages // pf
    rows = q_rows.shape[1]
    tp = k_new.shape[1]
    nh = FOX_HEADS

    def page(i):
        return lambda bi, j, pt: (pt[bi, jnp.minimum(j * pf + i, n_pages - 1)], 0, 0, 0)

    pool_spec = [pl.BlockSpec((1, PAGE_SIZE, nh, HEAD_DIM), page(i)) for i in range(pf)]
    per_seq = lambda shape: pl.BlockSpec(shape, lambda bi, j, pt: (bi,) + (0,) * (len(shape) - 1))
    kern = functools.partial(_fox_decode_kernel, n_steps=n_steps, pf=pf)
    return pl.pallas_call(
        kern,
        out_shape=jax.ShapeDtypeStruct((b, rows, HEAD_DIM), F32),
        grid_spec=pltpu.PrefetchScalarGridSpec(
            num_scalar_prefetch=1,
            grid=(b, n_steps + 1),
            in_specs=[per_seq((1, rows, HEAD_DIM)), per_seq((1, rows, 1)),
                      pl.BlockSpec((1, 1, pf * PAGE_SIZE * nh), lambda bi, j, pt: (bi, 0, jnp.minimum(j, n_steps - 1))),
                      per_seq((1, 1, tp * nh))]
                     + pool_spec + pool_spec
                     + [per_seq((1, tp, nh, HEAD_DIM)), per_seq((1, tp, nh, HEAD_DIM))],
            out_specs=per_seq((1, rows, HEAD_DIM)),
            scratch_shapes=[pltpu.VMEM((rows, 128), F32), pltpu.VMEM((rows, 128), F32),
                            pltpu.VMEM((rows, HEAD_DIM), F32), pltpu.VMEM((rows, PAGE_SIZE * nh), F32)],
        ),
        compiler_params=_params("parallel", "arbitrary"),
        name="fox_decode",
    )(page_table, q_rows, cq_rows, ck_past, ck_new, *([k_pool] * pf), *([v_pool] * pf), k_new, v_new)


SLC_DECODE_PAGES = 16


def _slc_decode_kernel(pt_ref, q_ref, bias_ref, biasnew_ref, *refs, n_steps, pf):
    kp, vp = refs[:pf], refs[pf:2 * pf]
    knew_ref, vnew_ref, o_ref, m_sc, l_sc, acc_sc = refs[2 * pf:]
    j = pl.program_id(1)

    @pl.when(j == 0)
    def _():
        m_sc[...] = jnp.full_like(m_sc, NEG_INF)
        l_sc[...] = jnp.zeros_like(l_sc)
        acc_sc[...] = jnp.zeros_like(acc_sc)

    def update(g, k2, v2, bias):
        s = _dot_nt(q_ref[0, g], k2.astype(BF16)) * SCALE + bias
        _online_softmax_step(s, v2.astype(BF16), m_sc.at[g], l_sc.at[g], acc_sc.at[g])

    @pl.when(j < n_steps)
    def _():
        for g in range(q_ref.shape[1]):
            k2 = jnp.concatenate([kp[i][0, :, g, :] for i in range(pf)], axis=0)
            v2 = jnp.concatenate([vp[i][0, :, g, :] for i in range(pf)], axis=0)
            update(g, k2, v2, bias_ref[0, g])

    @pl.when(j == n_steps)
    def _():
        for g in range(q_ref.shape[1]):
            update(g, knew_ref[0, :, g, :], vnew_ref[0, :, g, :], biasnew_ref[0, g])
        o_ref[0] = acc_sc[...] / jnp.maximum(l_sc[...], 1e-30)


def slc_decode(q_rows, bias_past, bias_new, k_pool, v_pool, k_new, v_new, page_table):
    b, n_pages = page_table.shape
    pf = math.gcd(SLC_DECODE_PAGES, n_pages)
    n_steps = n_pages // pf
    g, rows = q_rows.shape[1], q_rows.shape[2]
    tp = k_new.shape[1]
    assert tp % 128 == 0

    def page(i):
        return lambda bi, j, pt: (pt[bi, jnp.minimum(j * pf + i, n_pages - 1)], 0, 0, 0)

    pool_spec = [pl.BlockSpec((1, PAGE_SIZE, g, HEAD_DIM), page(i)) for i in range(pf)]
    per_seq = lambda shape: pl.BlockSpec(shape, lambda bi, j, pt: (bi,) + (0,) * (len(shape) - 1))
    return pl.pallas_call(
        functools.partial(_slc_decode_kernel, n_steps=n_steps, pf=pf),
        out_shape=jax.ShapeDtypeStruct((b, g, rows, HEAD_DIM), F32),
        grid_spec=pltpu.PrefetchScalarGridSpec(
            num_scalar_prefetch=1,
            grid=(b, n_steps + 1),
            in_specs=[per_seq((1, g, rows, HEAD_DIM)),
                      pl.BlockSpec((1, g, rows, pf * PAGE_SIZE),
                                   lambda bi, j, pt: (bi, 0, 0, jnp.minimum(j, n_steps - 1))),
                      per_seq((1, g, rows, tp))]
                     + pool_spec + pool_spec
                     + [per_seq((1, tp, g, HEAD_DIM)), per_seq((1, tp, g, HEAD_DIM))],
            out_specs=per_seq((1, g, rows, HEAD_DIM)),
            scratch_shapes=[pltpu.VMEM((g, rows, 128), F32), pltpu.VMEM((g, rows, 128), F32),
                            pltpu.VMEM((g, rows, HEAD_DIM), F32)],
        ),
        compiler_params=_params("parallel", "arbitrary"),
        name="nsa_slc_decode",
    )(page_table, q_rows, bias_past, bias_new, *([k_pool] * pf), *([v_pool] * pf), k_new, v_new)


def _cmp_small_kernel(q_ref, k_ref, v_ref, bias_ref, o_ref, imp_ref, *, t_new):
    s = _dot_nt(q_ref[0, 0], k_ref[0].astype(BF16)) * SCALE + bias_ref[0]
    mask = s > VISIBLE_MIN
    m = jnp.max(s, axis=-1, keepdims=True)
    e = jnp.where(mask, jnp.exp(s - m), 0.0)
    p = e / jnp.maximum(jnp.sum(e, axis=-1, keepdims=True), 1e-30)
    o_ref[0, 0] = _dot(p.astype(BF16), v_ref[0].astype(BF16))
    imp_ref[0, 0] = jnp.sum(p.reshape(t_new, NSA_GROUP, p.shape[-1]), axis=1)


def cmp_small(q_rows, ck, cv, bias, t_new):
    b, g, rows, d = q_rows.shape
    s = ck.shape[1]
    return pl.pallas_call(
        functools.partial(_cmp_small_kernel, t_new=t_new),
        out_shape=(jax.ShapeDtypeStruct((b, g, rows, d), F32), jax.ShapeDtypeStruct((b, g, t_new, s), F32)),
        grid=(b, g),
        in_specs=[
            pl.BlockSpec((1, 1, rows, d), lambda bi, gi: (bi, gi, 0, 0)),
            pl.BlockSpec((1, s, d), lambda bi, gi: (bi, 0, gi)),
            pl.BlockSpec((1, s, d), lambda bi, gi: (bi, 0, gi)),
            pl.BlockSpec((1, rows, s), lambda bi, gi: (gi, 0, 0)),
        ],
        out_specs=(pl.BlockSpec((1, 1, rows, d), lambda bi, gi: (bi, gi, 0, 0)),
                   pl.BlockSpec((1, 1, t_new, s), lambda bi, gi: (bi, gi, 0, 0))),
        compiler_params=_params("parallel", "parallel"),
        name="nsa_cmp_sample",
    )(q_rows, ck, cv, bias)


def _flash_small_kernel(q_ref, k_ref, v_ref, bias_ref, o_ref, m_sc, l_sc, acc_sc, *, nk):
    j = pl.program_id(2)

    @pl.when(j == 0)
    def _():
        m_sc[...] = jnp.full_like(m_sc, NEG_INF)
        l_sc[...] = jnp.zeros_like(l_sc)
        acc_sc[...] = jnp.zeros_like(acc_sc)

    s = _dot_nt(q_ref[0, 0], k_ref[0, 0].astype(BF16)) * SCALE + bias_ref[0, 0]
    vis = s > VISIBLE_MIN
    m_prev = m_sc[...]
    m_new = jnp.maximum(m_prev, jnp.max(s, axis=-1, keepdims=True))
    alpha = jnp.exp(m_prev - m_new)
    p = jnp.where(vis, jnp.exp(s - m_new), 0.0)
    l_sc[...] = alpha * l_sc[...] + jnp.sum(p, axis=-1, keepdims=True)
    acc_sc[...] = alpha * acc_sc[...] + _dot(p.astype(BF16), v_ref[0, 0].astype(BF16))
    m_sc[...] = m_new

    @pl.when(j == nk - 1)
    def _():
        o_ref[0, 0] = acc_sc[...] / jnp.maximum(l_sc[...], 1e-30)


def flash_small(q_rows, k, v, bias, tk):
    b, g, rows, d = q_rows.shape
    s = k.shape[2]
    assert s % tk == 0
    nk = s // tk
    return pl.pallas_call(
        functools.partial(_flash_small_kernel, nk=nk),
        out_shape=jax.ShapeDtypeStruct((b, g, rows, d), F32),
        grid=(b, g, nk),
        in_specs=[
            pl.BlockSpec((1, 1, rows, d), lambda bi, gi, j: (bi, gi, 0, 0)),
            pl.BlockSpec((1, 1, tk, d), lambda bi, gi, j: (bi, gi, j, 0)),
            pl.BlockSpec((1, 1, tk, d), lambda bi, gi, j: (bi, gi, j, 0)),
            pl.BlockSpec((1, 1, rows, tk), lambda bi, gi, j: (bi, gi, 0, j)),
        ],
        out_specs=pl.BlockSpec((1, 1, rows, d), lambda bi, gi, j: (bi, gi, 0, 0)),
        scratch_shapes=[pltpu.VMEM((rows, 1), F32), pltpu.VMEM((rows, 1), F32), pltpu.VMEM((rows, d), F32)],
        compiler_params=_params("parallel", "parallel", "arbitrary"),
        name="nsa_flash_sample",
    )(q_rows, k, v, bias)


def _topk_kernel(col_ref, row_ref, o_ref, *, n_top):
    a = col_ref[...]
    bb = row_ref[...]
    shape = (a.shape[0], a.shape[1], bb.shape[2])
    j_idx = lax.broadcasted_iota(jnp.int32, shape, 1)
    i_idx = lax.broadcasted_iota(jnp.int32, shape, 2)
    ahead = (a > bb) | ((a == bb) & (j_idx < i_idx))
    rank = jnp.sum(jnp.where(ahead, 1.0, 0.0), axis=1, keepdims=True)
    o_ref[...] = jnp.where(rank < n_top, 1.0, 0.0)


def topk_mask(score, n_top):
    r, ns = score.shape
    tr = 8 if r % 8 == 0 else r
    out = pl.pallas_call(
        functools.partial(_topk_kernel, n_top=n_top),
        out_shape=jax.ShapeDtypeStruct((r, 1, ns), F32),
        grid=(r // tr,),
        in_specs=[pl.BlockSpec((tr, ns, 1), lambda i: (i, 0, 0)), pl.BlockSpec((tr, 1, ns), lambda i: (i, 0, 0))],
        out_specs=pl.BlockSpec((tr, 1, ns), lambda i: (i, 0, 0)),
        compiler_params=_params("parallel"),
        name="nsa_topk",
    )(score.reshape(r, ns, 1), score.reshape(r, 1, ns))
    return out.reshape(r, ns)


def _pack_w_in0(w_in0):
    d = w_in0.shape[0]
    off_fz = 3 * FOX_W
    off_nq = off_fz + FOX_HEADS
    off_kv = off_nq + NSA_W
    off_ng = off_kv + 6 * NSA_KV_W
    parts = [w_in0[:, :off_fz], w_in0[:, off_nq:off_ng], w_in0[:, off_fz:off_nq], w_in0[:, off_ng:]]
    used = sum(p.shape[1] for p in parts)
    parts.append(jnp.zeros((d, L0_PACKED - used), w_in0.dtype))
    return jnp.concatenate(parts, axis=1).astype(BF16)


def _forget_terms(proj3, b_forget):
    fz = proj3[:, :, COL_SMALL:COL_SMALL + FOX_HEADS]
    lf = jax.nn.log_sigmoid(fz + b_forget.astype(F32))
    return lf, jnp.cumsum(lf, axis=1)


def _kv_cols(proj3, idx):
    off = COL_KV + idx * NSA_KV_W
    return proj3[:, :, off:off + NSA_KV_W]


def attn_prompt(h_bf16, b, t, wts):
    m = b * t
    g = NSA_KV_HEADS
    proj = matmul(h_bf16, wts["w_in0"])
    proj3 = proj.reshape(b, t, L0_PACKED)
    lf, c_new = _forget_terms(proj3, wts["b_forget"])
    o_fox = fox_prompt(proj3, c_new)

    kc, vc, ks, vs, kw, vw = [_kv_cols(proj3, i) for i in range(6)]
    nc, ns = t // CMP_BLOCK, t // SEL_BLOCK
    tk = min(NSA_TK, t)
    assert t % (2 * CMP_BLOCK) == 0 and t % tk == 0
    ck = compress(proj, COL_KV // HEAD_DIM, g, wts["cmp_k"]).reshape(g, b, nc, HEAD_DIM)
    cv = compress(proj, (COL_KV + NSA_KV_W) // HEAD_DIM, g, wts["cmp_v"]).reshape(g, b, nc, HEAD_DIM)
    even_odd = lambda a: jnp.concatenate([a[:, :, 0::2], a[:, :, 1::2]], axis=2)
    rel = wts["rel_bias"].astype(F32)
    blk_c = jnp.concatenate([jnp.arange(0, nc, 2, dtype=jnp.int32), jnp.arange(1, nc, 2, dtype=jnp.int32)])
    c_dist = jnp.arange(t, dtype=jnp.int32)[:, None] - ((blk_c + 1) * CMP_BLOCK - 1)[None, :]
    bias_cmp = _t5_bias(rel, c_dist)
    o_cmp, selmask = cmp_prompt(proj3, even_odd(ck), even_odd(cv), bias_cmp)

    ratio = tk // NSA_TQ
    o_far = -(-(_T5_THRESHOLDS[-1] + tk - 1) // NSA_TQ)
    o_slc = nsa_band(proj3, _band_strip(rel, o_far, ratio), tk, COL_KV + 2 * NSA_KV_W, COL_KV + 3 * NSA_KV_W,
                     "causal", selmask)
    o_win_max = ratio * (-(-WINDOW // tk) + 1) - 1
    o_nsa = nsa_band(proj3, _band_strip(rel, o_win_max, ratio, window=WINDOW), tk,
                     COL_KV + 4 * NSA_KV_W, COL_KV + 5 * NSA_KV_W, "window", combine_with=(o_cmp, o_slc))
    mix = (o_fox.reshape(m, FOX_W), o_nsa.reshape(m, NSA_W))
    win_len = wts["win_len"]
    zeros = jnp.zeros((b, WINDOW, NSA_KV_W), F32)
    kw_a = jnp.concatenate([zeros, kw], axis=1)
    vw_a = jnp.concatenate([zeros, vw], axis=1)
    fk = proj3[:, :, COL_FK:COL_FK + FOX_W]
    fv = proj3[:, :, COL_FV:COL_FV + FOX_W]
    new = (fk, fv, lf, kc, vc, ks, vs, kw_a[:, kw_a.shape[1] - win_len:], vw_a[:, vw_a.shape[1] - win_len:])
    return mix, new


def attn_sample(h_bf16, b, t, wts, caches, page_table, state_win_k, state_win_v):
    m = b * t
    g, hg = NSA_KV_HEADS, NSA_GROUP
    n_pages = page_table.shape[1]
    p = n_pages * PAGE_SIZE
    assert t <= PAGE_SIZE and p % SEL_BLOCK == 0
    proj = matmul(h_bf16, wts["w_in0"])
    proj3 = proj.reshape(b, t, L0_PACKED)
    lf, c_new = _forget_terms(proj3, wts["b_forget"])
    kc, vc, ks, vs, kw, vw = [_kv_cols(proj3, i) for i in range(6)]
    fq = proj3[:, :, COL_FQ:COL_FQ + FOX_W]
    fk = proj3[:, :, COL_FK:COL_FK + FOX_W]
    fv = proj3[:, :, COL_FV:COL_FV + FOX_W]
    q_pos = p + np.arange(t)

    cache_fox_k, cache_fox_v, cache_fox_lf, cache_cmp_k, cache_cmp_v, cache_slc_k, cache_slc_v = caches
    assert PAGE_SIZE % CMP_BLOCK == 0
    per_page = PAGE_SIZE // CMP_BLOCK
    n_phys = cache_cmp_k.shape[0]
    plf, ck, cv = gather_pools(
        [cache_fox_lf,
         compress_pool(cache_cmp_k, wts["cmp_k"]).reshape(n_phys, per_page, NSA_KV_W),
         compress_pool(cache_cmp_v, wts["cmp_v"]).reshape(n_phys, per_page, NSA_KV_W)], page_table)

    c_past = plf - lax.cumsum(plf, axis=1, reverse=True)
    tp = -(-t // 8) * 8
    pad_new = lambda a: jnp.concatenate([a, jnp.zeros((b, tp - t) + a.shape[2:], F32)], axis=1)
    o_fox = fox_decode(fq.reshape(b, t * FOX_HEADS, HEAD_DIM).astype(BF16), c_new.reshape(b, t * FOX_HEADS, 1),
                       c_past.reshape(b, 1, p * FOX_HEADS), pad_new(c_new).reshape(b, 1, tp * FOX_HEADS),
                       cache_fox_k, cache_fox_v,
                       pad_new(fk.reshape(b, t, FOX_HEADS, HEAD_DIM)), pad_new(fv.reshape(b, t, FOX_HEADS, HEAD_DIM)),
                       page_table).reshape(m, FOX_W)

    rel = wts["rel_bias"].astype(F32)
    nq = proj3[:, :, COL_NQ:COL_NQ + NSA_W].reshape(b, t, g, hg, HEAD_DIM)
    q_rows = nq.transpose(0, 2, 1, 3, 4).reshape(b, g, t * hg, HEAD_DIM).astype(BF16)

    def head_rows(x):
        tt, ss = x.shape[1], x.shape[2]
        return x.reshape(g, hg, tt, ss).transpose(0, 2, 1, 3).reshape(g, tt * hg, ss)

    l_tot = p + t
    l_pad = -(-l_tot // SEL_BLOCK) * SEL_BLOCK
    nc, ns = l_pad // CMP_BLOCK, l_pad // SEL_BLOCK
    nc_past = p // CMP_BLOCK
    assert nc_past % 2 == 0 and (nc_past + 1) * CMP_BLOCK - 1 > q_pos[-1]
    even_odd = lambda a: jnp.concatenate([a[:, 0::2], a[:, 1::2]], axis=1)
    blk_c = jnp.concatenate([jnp.arange(0, nc_past, 2, dtype=jnp.int32), jnp.arange(1, nc_past, 2, dtype=jnp.int32)])
    c_dist = jnp.asarray(q_pos, jnp.int32)[:, None] - ((blk_c + 1) * CMP_BLOCK - 1)[None, :]
    bias_cmp = jnp.where((c_dist >= 0)[None], _t5_bias(rel, c_dist), NEG_INF)
    o_cmp, imp = cmp_small(q_rows, even_odd(ck), even_odd(cv), head_rows(bias_cmp), t)
    ns_past = nc_past // 2
    imp_s = imp[..., :ns_past] + imp[..., ns_past:]
    imp_s = jnp.concatenate([imp_s, jnp.zeros((b, g, t, ns - ns_past), F32)], axis=-1)
    blk = np.arange(ns)[None, :]
    cur = (q_pos // SEL_BLOCK)[:, None]
    valid = blk * SEL_BLOCK <= q_pos[:, None]
    forced = (blk == 0) | (blk == cur) | (blk == cur - 1)
    score = jnp.where(jnp.asarray(forced), SEL_FORCE, jnp.where(jnp.asarray(valid), imp_s, -1.0))
    sel = topk_mask(score.reshape(b * g * t, ns), min(N_SEL, ns)).reshape(b, g, t, ns)

    qp = jnp.asarray(q_pos, jnp.int32)[:, None]
    n_full = p // SEL_BLOCK
    t_pad = 128
    assert ns - 1 == n_full and t <= SEL_BLOCK

    def slc_bias(dist, visible, sel_keys):
        s = dist.shape[1]
        t5 = head_rows(jnp.where(visible[None], _t5_bias(rel, dist), NEG_INF))
        chosen = jnp.broadcast_to(sel_keys[:, :, :, None, :], (b, g, t, hg, s)).reshape(b, g, t * hg, s)
        return jnp.where(chosen > 0.5, t5[None], NEG_INF)

    d_past = qp - jnp.arange(p, dtype=jnp.int32)[None, :]
    bias_past = slc_bias(d_past, d_past >= 0, jnp.repeat(sel[..., :n_full], SEL_BLOCK, axis=-1))
    d_new = qp - (p + jnp.arange(t_pad, dtype=jnp.int32))[None, :]
    new_ok = (d_new >= 0) & (jnp.arange(t_pad)[None, :] < t)
    bias_new = slc_bias(d_new, new_ok, jnp.broadcast_to(sel[..., ns - 1:ns], (b, g, t, t_pad)))
    pad_rows = lambda a: jnp.concatenate(
        [a.reshape(b, t, g, HEAD_DIM), jnp.zeros((b, t_pad - t, g, HEAD_DIM), F32)], axis=1)
    o_slc = slc_decode(q_rows, bias_past, bias_new, cache_slc_k, cache_slc_v, pad_rows(ks), pad_rows(vs), page_table)

    def head_major(a, rows):
        a = a.reshape(b, t, g, HEAD_DIM).transpose(0, 2, 1, 3)
        return jnp.concatenate([a, jnp.zeros((b, g, rows - t, HEAD_DIM), F32)], axis=2)

    p_w = state_win_k.shape[1]
    s_win = -(-(p_w + t) // 128) * 128
    kw_a = jnp.concatenate([state_win_k.reshape(b, p_w, NSA_KV_W), kw], axis=1)
    vw_a = jnp.concatenate([state_win_v.reshape(b, p_w, NSA_KV_W), vw], axis=1)
    win_major = lambda st, new: jnp.concatenate([st.transpose(0, 2, 1, 3), head_major(new, s_win - p_w)], axis=2)
    w_idx = jnp.arange(s_win, dtype=jnp.int32)[None, :]
    wpos = (p - p_w) + w_idx
    wd = qp - wpos
    wok = (wd >= 0) & (wd < WINDOW) & (wpos >= 0) & (w_idx < p_w + t)
    bias_win = head_rows(jnp.where(wok[None], _t5_bias(rel, wd), NEG_INF))
    bias_win = jnp.broadcast_to(bias_win[None], (b,) + bias_win.shape)
    o_win = flash_small(q_rows, win_major(state_win_k, kw), win_major(state_win_v, vw), bias_win, s_win)

    def to_tokens(o):
        return o.reshape(b, g, t, hg, HEAD_DIM).transpose(0, 2, 1, 3, 4).reshape(m, NSA_W)

    o_nsa = nsa_combine(proj, to_tokens(o_cmp), to_tokens(o_slc), to_tokens(o_win))
    mix = (o_fox.astype(BF16), o_nsa)
    win_len = wts["win_len"]
    new = (fk, fv, lf, kc, vc, ks, vs, kw_a[:, kw_a.shape[1] - win_len:], vw_a[:, vw_a.shape[1] - win_len:])
    return mix, new


def _weight_matmul(wts, key, layer, call):
    cache = wts["bf16"]
    if (key, layer) in cache:
        return call(cache[(key, layer)], None)
    result = call(wts[key], layer)
    if isinstance(result, tuple):
        result, cache[(key, layer)] = result
    return result


def trunk(x, p_emb, wts, lru_conv0, lru_h0, ffn_conv0, past=None):
    b, t, d = x.shape
    m = b * t
    x2 = x.reshape(m, d)
    ffn_bufs = []
    for i in range(2):
        h = rmsnorm(x2, wts["norm_mix"][i], BF16)
        if i == 0:
            if past is None:
                mix, attn_new = attn_prompt(h, b, t, wts)
            else:
                mix, attn_new = attn_sample(h, b, t, wts, *past)
            x2 = _weight_matmul(wts, "w_out0", None, lambda w, _, r=x2: matmul_pair(mix[0], mix[1], w, r))
        else:
            proj1 = _weight_matmul(wts, "w_in1", None, lambda w, _: matmul(h, w)).reshape(b, t, -1)
            tp = -(-t // 8) * 8
            if tp != t:
                proj1 = jnp.concatenate([proj1, jnp.zeros((b, tp - t, proj1.shape[-1]), F32)], axis=1)
            gated, h_last, lru_buf = lru_block(proj1, wts["lru_conv_w"], wts["lru_conv_b"], wts["lru_w_a"],
                                               wts["lru_b_a"], wts["lru_w_x"], wts["lru_b_x"], wts["lru_lambda"],
                                               lru_conv0, lru_h0, t)
            g_in = gated[:, :t].reshape(m, -1)
            x2 = _weight_matmul(wts, "w_out1", None, lambda w, _: matmul(g_in, w, res=x2))
        hf = rmsnorm(x2, wts["norm_ffn"][i], BF16)
        dff = wts["ffn_w_up"].shape[2] // 2
        if t % 8 == 0:
            if ("ffn_w_up", i) not in wts["bf16"]:
                wts["bf16"][("ffn_w_up", i)] = wts["ffn_w_up"][i].astype(BF16)
            act, buf = ffn_up_fused(hf, wts["bf16"][("ffn_w_up", i)], wts["ffn_conv_w"][i], wts["ffn_conv_b"][i],
                                    ffn_conv0[i], t)
        else:
            gu = _weight_matmul(wts, "ffn_w_up", i, lambda w, l: matmul(hf, w, layer=l))
            xp = jnp.concatenate([ffn_conv0[i], gu[:, :dff].reshape(b, t, dff)], axis=1)
            taps = [xp[:, FFN_CONV - 1 - s:FFN_CONV - 1 - s + t].reshape(m, dff) for s in range(FFN_CONV)]
            act = convgate(taps[0], taps[1], taps[2], gu[:, dff:], wts["ffn_conv_w"][i], wts["ffn_conv_b"][i])
            buf = xp[:, xp.shape[1] - (FFN_CONV - 1):]
        ffn_bufs.append(buf)
        x2 = _weight_matmul(wts, "ffn_w_down", i, lambda w, l: matmul(act, w, layer=l, res=x2))
        hp = rmsnorm(x2, wts["ple_norm"][i], BF16)
        p_in = p_emb[i].reshape(m, -1).astype(BF16)
        emb = _weight_matmul(wts, "ple_w_proj", i, lambda w, l: matmul(p_in, w, layer=l))
        x2 = _weight_matmul(wts, "ple_w_gate", i, lambda w, l: matmul(hp, w, layer=l, res=x2, aux=emb))
    y = rmsnorm(x2, wts["final_norm"], F32).reshape(b, t, d)
    return y, attn_new, lru_buf, h_last.reshape(b, -1), jnp.stack(ffn_bufs)


def kernel(x_prompt, x_sample, cache_fox_k, cache_fox_v, cache_fox_lf, cache_cmp_k, cache_cmp_v, cache_slc_k, cache_slc_v, state_win_k, state_win_v, state_lru_conv, state_lru_h, state_ffn_conv, page_table, p_prompt, p_sample, norm_mix, norm_ffn, final_norm, w_in0, b_forget, cmp_w1_k, cmp_pe_k, cmp_w2_k, cmp_w1_v, cmp_pe_v, cmp_w2_v, rel_bias, w_out0, w_in1, lru_conv_w, lru_conv_b, lru_w_a, lru_b_a, lru_w_x, lru_b_x, lru_lambda, w_out1, ffn_w_up, ffn_conv_w, ffn_conv_b, ffn_w_down, ple_w_proj, ple_w_gate, ple_norm):
    depth = norm_mix.shape[0]
    assert depth == 2
    d = x_prompt.shape[-1]
    dff = ffn_w_down.shape[1]
    wts = {
        "norm_mix": norm_mix, "norm_ffn": norm_ffn, "ple_norm": ple_norm, "final_norm": final_norm,
        "w_in0": _pack_w_in0(w_in0), "b_forget": b_forget, "rel_bias": rel_bias,
        "win_len": state_win_k.shape[1],
        "w_out0": w_out0, "w_in1": w_in1, "w_out1": w_out1,
        "lru_conv_w": lru_conv_w, "lru_conv_b": lru_conv_b, "lru_w_a": lru_w_a, "lru_b_a": lru_b_a,
        "lru_w_x": lru_w_x, "lru_b_x": lru_b_x, "lru_lambda": lru_lambda,
        "ffn_w_up": ffn_w_up, "ffn_conv_w": ffn_conv_w, "ffn_conv_b": ffn_conv_b,
        "ffn_w_down": ffn_w_down, "ple_w_proj": ple_w_proj, "ple_w_gate": ple_w_gate,
        "bf16": {},
    }
    wts["cmp_k"] = (cmp_w1_k, cmp_pe_k, cmp_w2_k)
    wts["cmp_v"] = (cmp_w1_v, cmp_pe_v, cmp_w2_v)

    caches = (cache_fox_k, cache_fox_v, cache_fox_lf, cache_cmp_k, cache_cmp_v, cache_slc_k, cache_slc_v)
    y_s, attn_s, lru_conv_s, lru_h_s, ffn_conv_s = trunk(
        x_sample, p_sample, wts, state_lru_conv, state_lru_h, state_ffn_conv,
        past=(caches, page_table, state_win_k, state_win_v))
    bp = x_prompt.shape[0]
    y_p, attn_p, lru_conv_p, lru_h_p, ffn_conv_p = trunk(
        x_prompt, p_prompt, wts,
        jnp.zeros((bp, LRU_CONV - 1, d), F32), jnp.zeros((bp, d), F32),
        jnp.zeros((depth, bp, FFN_CONV - 1, dff), F32))

    def shape_attn(new, b, t):
        fk, fv, lf, kc, vc, ks, vs, wk, wv = new
        h4 = lambda a: a.reshape(b, a.shape[1], FOX_HEADS, HEAD_DIM)
        g4 = lambda a: a.reshape(b, a.shape[1], NSA_KV_HEADS, HEAD_DIM)
        return h4(fk), h4(fv), lf, g4(kc), g4(vc), g4(ks), g4(vs), g4(wk), g4(wv)

    ap = shape_attn(attn_p, bp, x_prompt.shape[1])
    asmp = shape_attn(attn_s, x_sample.shape[0], x_sample.shape[1])
    out = [y_p, y_s]
    for a, s in zip(ap, asmp):
        out += [a, s]
    out += [lru_conv_p, lru_conv_s, lru_h_p, lru_h_s, ffn_conv_p, ffn_conv_s]
    return tuple(out)
```

```python
import functools
import math

import numpy as np
import jax
import jax.numpy as jnp
from jax import lax
from jax.experimental import pallas as pl
from jax.experimental.pallas import tpu as pltpu

PAGE_SIZE = 128
HEAD_DIM = 128
FOX_HEADS = 16
NSA_HEADS = 16
NSA_KV_HEADS = 2
NSA_GROUP = NSA_HEADS // NSA_KV_HEADS
CMP_BLOCK = 32
SEL_BLOCK = 64
N_SEL = 16
WINDOW = 512
SEL_FORCE = 1000.0
N_BUCKETS = 32
MAX_DISTANCE = 1024
LRU_HEADS = 16
LRU_C = 8.0
LRU_CONV = 4
FFN_CONV = 3
EPS = 1e-6
NEG_INF = -1e30
VISIBLE_MIN = -5e29

FOX_W = FOX_HEADS * HEAD_DIM
NSA_W = NSA_HEADS * HEAD_DIM
NSA_KV_W = NSA_KV_HEADS * HEAD_DIM
GROUP_W = NSA_GROUP * HEAD_DIM
SCALE = HEAD_DIM ** -0.5

COL_FQ, COL_FK, COL_FV = 0, FOX_W, 2 * FOX_W
COL_NQ = 3 * FOX_W
COL_KV = COL_NQ + NSA_W
COL_SMALL = COL_KV + 6 * NSA_KV_W
L0_PACKED = -(-(COL_SMALL + 128) // 1024) * 1024

F32 = jnp.float32
BF16 = jnp.bfloat16

VMEM_LIMIT_BYTES = 56 * 1024 * 1024


def _params(*sem):
    return pltpu.CompilerParams(dimension_semantics=sem, vmem_limit_bytes=VMEM_LIMIT_BYTES)


def _dot(a, b):
    return jnp.dot(a, b, preferred_element_type=F32)


def _dot_nt(a, b):
    return lax.dot_general(a, b, (((1,), (1,)), ((), ())), preferred_element_type=F32)


def _gelu(x):
    c = math.sqrt(2.0 / math.pi)
    return 0.5 * x * (1.0 + jnp.tanh(c * (x + 0.044715 * (x * x * x))))


def _sigmoid(x):
    return 1.0 / (1.0 + jnp.exp(-x))


def _online_softmax_step(s, v, m_sc, l_sc, acc_sc):
    reps = s.shape[1] // 128
    m_prev = m_sc[...]
    m_new = jnp.maximum(m_prev, jnp.max(s, axis=-1, keepdims=True))
    alpha = jnp.exp(m_prev - m_new)
    m_wide = m_new if reps == 1 else jnp.concatenate([m_new] * reps, axis=1)
    p = jnp.exp(s - m_wide)
    l_sc[...] = alpha * l_sc[...] + jnp.sum(p, axis=-1, keepdims=True)
    acc_sc[...] = alpha * acc_sc[...] + _dot(p.astype(BF16), v)
    m_sc[...] = m_new


def _bucket_np(dist):
    n = np.maximum(np.asarray(dist, np.int64), 0)
    exact = N_BUCKETS // 2
    nf = np.maximum(n, 1).astype(np.float64)
    large = exact + (np.log(nf / exact) / math.log(MAX_DISTANCE / exact) * (N_BUCKETS - exact)).astype(np.int64)
    return np.where(n < exact, n, np.minimum(large, N_BUCKETS - 1)).astype(np.int32)


def _rmsnorm_kernel(x_ref, g_ref, o_ref):
    x = x_ref[...]
    y = x * lax.rsqrt(jnp.mean(x * x, axis=-1, keepdims=True) + EPS)
    o_ref[...] = (y * g_ref[...]).astype(o_ref.dtype)


def rmsnorm(x, g, out_dtype):
    m, d = x.shape
    tm = min(m, 256)
    return pl.pallas_call(
        _rmsnorm_kernel,
        out_shape=jax.ShapeDtypeStruct((m, d), out_dtype),
        grid=(m // tm,),
        in_specs=[pl.BlockSpec((tm, d), lambda i: (i, 0)), pl.BlockSpec((1, d), lambda i: (0, 0))],
        out_specs=pl.BlockSpec((tm, d), lambda i: (i, 0)),
        compiler_params=_params("parallel"),
        name="rmsnorm",
    )(x, g.reshape(1, d))


def _mm_kernel(*refs, nk, epilogue, emit):
    a_ref, w_ref = refs[0], refs[1]
    n_in = 2 + (epilogue is not None) + (epilogue == "ple")
    o_ref = refs[n_in]
    k = pl.program_id(2)

    @pl.when(k == 0)
    def _():
        o_ref[...] = jnp.zeros_like(o_ref)

    if emit:
        refs[n_in + 1][...] = w_ref[...].astype(BF16)
        o_ref[...] += _dot(a_ref[...], refs[n_in + 1][...])
    else:
        o_ref[...] += _dot(a_ref[...], w_ref[...])

    if epilogue is not None:
        @pl.when(k == nk - 1)
        def _():
            acc = o_ref[...]
            if epilogue == "res":
                o_ref[...] = refs[2][...] + acc
            else:
                o_ref[...] = refs[2][...] + _sigmoid(acc) * refs[3][...]


def matmul(a, w, *, layer=None, res=None, aux=None, tm=1024, tn=1024, tk=2048):
    m, kdim = a.shape
    assert w.shape[-2] == kdim and (layer is None) == (w.ndim == 2)
    n = w.shape[-1]
    tm, tn, tk = min(tm, m), min(tn, n), min(tk, kdim)
    assert m % tm == 0 and n % tn == 0 and kdim % tk == 0
    nk = kdim // tk
    emit = w.dtype != BF16
    assert not emit or m == tm
    epilogue = None if res is None else ("res" if aux is None else "ple")
    if layer is None:
        w_spec = pl.BlockSpec((tk, tn), lambda i, j, k: (k, j))
    else:
        w_spec = pl.BlockSpec((None, tk, tn), lambda i, j, k: (layer, k, j))
    in_specs = [pl.BlockSpec((tm, tk), lambda i, j, k: (i, k)), w_spec]
    args = [a, w]
    for extra in (res, aux):
        if extra is not None:
            in_specs.append(pl.BlockSpec((tm, tn), lambda i, j, k: (i, j)))
            args.append(extra)
    out_shape = jax.ShapeDtypeStruct((m, n), F32)
    out_specs = pl.BlockSpec((tm, tn), lambda i, j, k: (i, j))
    if emit:
        out_shape = (out_shape, jax.ShapeDtypeStruct((kdim, n), BF16))
        out_specs = (out_specs, pl.BlockSpec((tk, tn), lambda i, j, k: (k, j)))
    return pl.pallas_call(
        functools.partial(_mm_kernel, nk=nk, epilogue=epilogue, emit=emit),
        out_shape=out_shape,
        grid=(m // tm, n // tn, nk),
        in_specs=in_specs,
        out_specs=out_specs,
        compiler_params=_params("parallel", "parallel", "arbitrary"),
        name="matmul_" + (epilogue or "plain") + ("_cast" if emit else ""),
    )(*args)


def _mm_pair_kernel(a0_ref, a1_ref, w_ref, res_ref, o_ref, *wb_ref):
    k = pl.program_id(2)
    if wb_ref:
        wb_ref[0][...] = w_ref[...].astype(BF16)
        w_ref = wb_ref[0]

    @pl.when(k == 0)
    def _():
        o_ref[...] = res_ref[...] + _dot(a0_ref[...], w_ref[...])

    @pl.when(k == 1)
    def _():
        o_ref[...] += _dot(a1_ref[...], w_ref[...])


def matmul_pair(a0, a1, w, res, *, tm=1024, tn=1024):
    m, k0 = a0.shape
    assert a1.shape == (m, k0) and w.shape[0] == 2 * k0
    n = w.shape[1]
    tm, tn = min(tm, m), min(tn, n)
    assert m % tm == 0 and n % tn == 0
    emit = w.dtype != BF16
    assert not emit or m == tm
    piece = pl.BlockSpec((tm, k0), lambda i, j, k: (i, 0))
    tile = pl.BlockSpec((tm, tn), lambda i, j, k: (i, j))
    w_tile = pl.BlockSpec((k0, tn), lambda i, j, k: (k, j))
    out_shape, out_specs = jax.ShapeDtypeStruct((m, n), F32), tile
    if emit:
        out_shape, out_specs = (out_shape, jax.ShapeDtypeStruct(w.shape, BF16)), (tile, w_tile)
    return pl.pallas_call(
        _mm_pair_kernel,
        out_shape=out_shape,
        grid=(m // tm, n // tn, 2),
        in_specs=[piece, piece, w_tile, tile],
        out_specs=out_specs,
        compiler_params=_params("parallel", "parallel", "arbitrary"),
        name="matmul_pair" + ("_cast" if emit else ""),
    )(a0, a1, w, res)


def _ffn_up_kernel(a_ref, wg_ref, wu_ref, cw_ref, cb_ref, buf_ref, act_ref, tail_ref,
                   accg, accu, gbuf, *, nk, tiles_per_seq, tm):
    i = pl.program_id(1)
    k = pl.program_id(2)
    kw = FFN_CONV

    @pl.when(k == 0)
    def _():
        accg[...] = jnp.zeros_like(accg)
        accu[...] = jnp.zeros_like(accu)

    a = a_ref[...]
    accg[...] += _dot(a, wg_ref[...])
    accu[...] += _dot(a, wu_ref[...])

    @pl.when(k == nk - 1)
    def _():
        @pl.when(i % tiles_per_seq == 0)
        def _():
            gbuf[8 - (kw - 1):8, :] = buf_ref[0]

        g = accg[...]
        gbuf[8:8 + tm, :] = g
        y = g * cw_ref[kw - 1:kw, :] + cb_ref[...]
        for t in range(kw - 1):
            off = 8 - (kw - 1) + t
            y = y + gbuf[off:off + tm, :] * cw_ref[t:t + 1, :]
        act_ref[...] = (_gelu(y) * accu[...]).astype(act_ref.dtype)
        tail = gbuf[8 + tm - (kw - 1):8 + tm, :]
        gbuf[8 - (kw - 1):8, :] = tail
        tail_ref[0] = tail


def ffn_up_fused(a, w_up, conv_w, conv_b, conv_buf, seq_len, *, tm=1024, tn=1024, tk=2048):
    m, kdim = a.shape
    dff = w_up.shape[1] // 2
    tm, tn, tk = min(tm, seq_len), min(tn, dff), min(tk, kdim)
    assert seq_len % tm == 0 and dff % tn == 0 and kdim % tk == 0 and tm % 8 == 0
    nb = m // seq_len
    tps = seq_len // tm
    nj, nk = dff // tn, kdim // tk
    kw = FFN_CONV
    kern = functools.partial(_ffn_up_kernel, nk=nk, tiles_per_seq=tps, tm=tm)
    return pl.pallas_call(
        kern,
        out_shape=(jax.ShapeDtypeStruct((m, dff), BF16), jax.ShapeDtypeStruct((nb, kw - 1, dff), F32)),
        grid=(nj, m // tm, nk),
        in_specs=[
            pl.BlockSpec((tm, tk), lambda j, i, k: (i, k)),
            pl.BlockSpec((tk, tn), lambda j, i, k: (k, j)),
            pl.BlockSpec((tk, tn), lambda j, i, k: (k, j + nj)),
            pl.BlockSpec((kw, tn), lambda j, i, k: (0, j)),
            pl.BlockSpec((1, tn), lambda j, i, k: (0, j)),
            pl.BlockSpec((1, kw - 1, tn), lambda j, i, k: (i // tps, 0, j)),
        ],
        out_specs=(
            pl.BlockSpec((tm, tn), lambda j, i, k: (i, j)),
            pl.BlockSpec((1, kw - 1, tn), lambda j, i, k: (i // tps, 0, j)),
        ),
        scratch_shapes=[pltpu.VMEM((tm, tn), F32), pltpu.VMEM((tm, tn), F32), pltpu.VMEM((8 + tm, tn), F32)],
        compiler_params=_params("parallel", "arbitrary", "arbitrary"),
        name="ffn_up_fused",
    )(a, w_up, w_up, conv_w, conv_b.reshape(1, dff), conv_buf)


def _convgate_kernel(s0_ref, s1_ref, s2_ref, u_ref, cw_ref, cb_ref, o_ref):
    y = s0_ref[...] * cw_ref[2:3, :] + cb_ref[...]
    y = y + s2_ref[...] * cw_ref[0:1, :]
    y = y + s1_ref[...] * cw_ref[1:2, :]
    o_ref[...] = (_gelu(y) * u_ref[...]).astype(o_ref.dtype)


def convgate(s0, s1, s2, u, conv_w, conv_b, *, tn=2048):
    m, dff = s0.shape
    tn = min(tn, dff)
    row = pl.BlockSpec((m, tn), lambda j: (0, j))
    return pl.pallas_call(
        _convgate_kernel,
        out_shape=jax.ShapeDtypeStruct((m, dff), BF16),
        grid=(dff // tn,),
        in_specs=[row, row, row, row, pl.BlockSpec((FFN_CONV, tn), lambda j: (0, j)),
                  pl.BlockSpec((1, tn), lambda j: (0, j))],
        out_specs=row,
        compiler_params=_params("parallel"),
        name="convgate",
    )(s0, s1, s2, u, conv_w, conv_b.reshape(1, dff))


def _fox_prompt_kernel(q_ref, k_ref, v_ref, cq_ref, ck_ref, o_ref, m_sc, l_sc, acc_sc, cq_sc, *, nk, tq, tk):
    qi = pl.program_id(2)
    kj = pl.program_id(3)

    @pl.when(kj == 0)
    def _():
        m_sc[...] = jnp.full_like(m_sc, NEG_INF)
        l_sc[...] = jnp.zeros_like(l_sc)
        acc_sc[...] = jnp.zeros_like(acc_sc)
        cq_sc[...] = jnp.transpose(jnp.broadcast_to(cq_ref[0, 0], (128, tq)))

    @pl.when(kj * tk <= qi * tq + tq - 1)
    def _():
        q = q_ref[0].astype(BF16)
        k = k_ref[0].astype(BF16)
        cq = jnp.concatenate([cq_sc[...]] * (tk // 128), axis=1)
        s = _dot_nt(q, k) * SCALE + (cq - ck_ref[0, 0])
        row = qi * tq + lax.broadcasted_iota(jnp.int32, (tq, tk), 0)
        col = kj * tk + lax.broadcasted_iota(jnp.int32, (tq, tk), 1)
        s = jnp.where(row >= col, s, NEG_INF)
        _online_softmax_step(s, v_ref[0].astype(BF16), m_sc, l_sc, acc_sc)

    @pl.when(kj == nk - 1)
    def _():
        o_ref[0] = (acc_sc[...] / jnp.maximum(l_sc[...], 1e-30)).astype(o_ref.dtype)


def fox_prompt(proj3, c_new):
    b, t, _ = proj3.shape
    tq = min(1024, t)
    tk = min(1024, t)
    assert tq % 128 == 0 and tk % 128 == 0
    nq, nk = t // tq, t // tk
    h = FOX_HEADS
    c_row = c_new.transpose(0, 2, 1).reshape(b, h, 1, t)
    kq, kk, kv = COL_FQ // HEAD_DIM, COL_FK // HEAD_DIM, COL_FV // HEAD_DIM

    def kidx(qi, kj):
        return jnp.minimum(kj, (qi * tq + tq - 1) // tk)

    kern = functools.partial(_fox_prompt_kernel, nk=nk, tq=tq, tk=tk)
    return pl.pallas_call(
        kern,
        out_shape=jax.ShapeDtypeStruct((b, t, FOX_W), BF16),
        grid=(b, h, nq, nk),
        in_specs=[
            pl.BlockSpec((1, tq, HEAD_DIM), lambda bi, hi, qi, kj: (bi, qi, kq + hi)),
            pl.BlockSpec((1, tk, HEAD_DIM), lambda bi, hi, qi, kj: (bi, kidx(qi, kj), kk + hi)),
            pl.BlockSpec((1, tk, HEAD_DIM), lambda bi, hi, qi, kj: (bi, kidx(qi, kj), kv + hi)),
            pl.BlockSpec((1, 1, 1, tq), lambda bi, hi, qi, kj: (bi, hi, 0, qi)),
            pl.BlockSpec((1, 1, 1, tk), lambda bi, hi, qi, kj: (bi, hi, 0, kidx(qi, kj))),
        ],
        out_specs=pl.BlockSpec((1, tq, HEAD_DIM), lambda bi, hi, qi, kj: (bi, qi, hi)),
        scratch_shapes=[pltpu.VMEM((tq, 128), F32), pltpu.VMEM((tq, 128), F32), pltpu.VMEM((tq, HEAD_DIM), F32),
                        pltpu.VMEM((tq, 128), F32)],
        compiler_params=_params("parallel", "parallel", "parallel", "arbitrary"),
        name="fox_prompt",
    )(proj3, proj3, proj3, c_row, c_row)


def _compress_kernel(x_ref, pe_ref, w1_ref, w2_ref, o_ref, *, tr):
    acc = jnp.zeros((tr, HEAD_DIM), F32)
    for l in range(CMP_BLOCK):
        xl = x_ref[pl.ds(l, tr, stride=CMP_BLOCK), :] + pe_ref[l:l + 1, :]
        acc = acc + _dot(xl.astype(BF16), w1_ref[l])
    o_ref[0] = _dot(_gelu(acc).astype(BF16), w2_ref[...])


def compress(x2d, col_block, n_groups, cmp_w):
    w1, pe, w2 = cmp_w
    rows = x2d.shape[0]
    r = rows // CMP_BLOCK
    tr = min(r, 256)
    assert r % tr == 0 and rows % CMP_BLOCK == 0
    return pl.pallas_call(
        functools.partial(_compress_kernel, tr=tr),
        out_shape=jax.ShapeDtypeStruct((n_groups, r, HEAD_DIM), F32),
        grid=(n_groups, r // tr),
        in_specs=[pl.BlockSpec((tr * CMP_BLOCK, HEAD_DIM), lambda g, i: (i, col_block + g)),
                  pl.BlockSpec((CMP_BLOCK, HEAD_DIM), lambda g, i: (0, 0)),
                  pl.BlockSpec((CMP_BLOCK, HEAD_DIM, HEAD_DIM), lambda g, i: (0, 0, 0)),
                  pl.BlockSpec((HEAD_DIM, HEAD_DIM), lambda g, i: (0, 0))],
        out_specs=pl.BlockSpec((1, tr, HEAD_DIM), lambda g, i: (g, i, 0)),
        compiler_params=_params("parallel", "parallel"),
        name="nsa_compress",
    )(x2d, pe, w1.astype(BF16), w2.astype(BF16))


def _compress_pool_kernel(x_ref, pe_ref, w1_ref, w2_ref, o_ref, *, tr):
    for g in range(x_ref.shape[1]):
        acc = jnp.zeros((tr, HEAD_DIM), F32)
        for l in range(0, CMP_BLOCK, 2):
            xa = x_ref[pl.ds(l, tr, stride=CMP_BLOCK), g, :] + pe_ref[l:l + 1, :]
            xb = x_ref[pl.ds(l + 1, tr, stride=CMP_BLOCK), g, :] + pe_ref[l + 1:l + 2, :]
            acc = acc + _dot(jnp.concatenate([xa, xb], axis=1).astype(BF16), w1_ref[l // 2])
        o_ref[:, g * HEAD_DIM:(g + 1) * HEAD_DIM] = _dot(_gelu(acc).astype(BF16), w2_ref[...])


def compress_pool(pool, cmp_w):
    w1, pe, w2 = cmp_w
    n_phys, page, g, d = pool.shape
    r = n_phys * page // CMP_BLOCK
    tr = max(c for c in range(8, 257, 8) if r % c == 0)
    return pl.pallas_call(
        functools.partial(_compress_pool_kernel, tr=tr),
        out_shape=jax.ShapeDtypeStruct((r, g * d), F32),
        grid=(r // tr,),
        in_specs=[pl.BlockSpec((tr * CMP_BLOCK, g, d), lambda i: (i, 0, 0)),
                  pl.BlockSpec((CMP_BLOCK, d), lambda i: (0, 0)),
                  pl.BlockSpec((CMP_BLOCK // 2, 2 * d, d), lambda i: (0, 0, 0)),
                  pl.BlockSpec((d, d), lambda i: (0, 0))],
        out_specs=pl.BlockSpec((tr, g * d), lambda i: (i, 0)),
        compiler_params=_params("parallel"),
        name="nsa_compress_pool",
    )(pool.reshape(n_phys * page, g, d), pe, w1.astype(BF16).reshape(CMP_BLOCK // 2, 2 * d, d), w2.astype(BF16))


def _cmp_prompt_kernel(q_ref, ck_ref, cv_ref, bias_ref, exp_ref, o_ref, sel_ref, *, tq, nc, n_top):
    qi = pl.program_id(2)
    ns = nc // 2
    t_col = qi * tq + lax.broadcasted_iota(jnp.int32, (tq, nc), 0)
    lane = lax.broadcasted_iota(jnp.int32, (tq, nc), 1)
    blk_c = jnp.where(lane < ns, 2 * lane, 2 * (lane - ns) + 1)
    mask = t_col >= (blk_c + 1) * CMP_BLOCK - 1
    ck = ck_ref[0, 0].astype(BF16)
    cv = cv_ref[0, 0].astype(BF16)
    imp = jnp.zeros((tq, nc), F32)
    for hg in range(NSA_GROUP):
        sl = slice(hg * HEAD_DIM, (hg + 1) * HEAD_DIM)
        s = _dot_nt(q_ref[0, :, sl].astype(BF16), ck) * SCALE + bias_ref[hg]
        s = jnp.where(mask, s, NEG_INF)
        m = jnp.max(s, axis=-1, keepdims=True)
        e = jnp.where(mask, jnp.exp(s - m), 0.0)
        p = e / jnp.maximum(jnp.sum(e, axis=-1, keepdims=True), 1e-30)
        imp = imp + p
        o_ref[0, :, sl] = _dot(p.astype(BF16), cv)
    imp_s = imp[:, :ns] + imp[:, ns:]
    t_s = qi * tq + lax.broadcasted_iota(jnp.int32, (tq, ns), 0)
    blk = lax.broadcasted_iota(jnp.int32, (tq, ns), 1)
    cur = t_s // SEL_BLOCK
    valid = blk * SEL_BLOCK <= t_s
    forced = (blk == 0) | (blk == cur) | (blk == cur - 1)
    score = jnp.where(forced, SEL_FORCE, jnp.where(valid, imp_s, -1.0))
    rank = jnp.zeros((tq, ns), F32)
    for j in range(ns):
        cj = score[:, j:j + 1]
        ahead = (cj > score) | ((cj == score) & (blk > j))
        rank = rank + jnp.where(ahead, 1.0, 0.0)
    sel = jnp.where(rank < n_top, 1.0, 0.0).astype(BF16)
    sel_ref[0, 0] = (1.0 - _dot(sel, exp_ref[...])) * NEG_INF


def cmp_prompt(proj3, ck_perm, cv_perm, bias_cmp):
    b, t, _ = proj3.shape
    nc = t // CMP_BLOCK
    ns = t // SEL_BLOCK
    tq = min(256, t)
    g = NSA_KV_HEADS
    n_top = min(N_SEL, ns)
    expand = jnp.asarray(np.repeat(np.eye(ns, dtype=np.float32), SEL_BLOCK, axis=1), BF16)
    kq = COL_NQ // GROUP_W
    kern = functools.partial(_cmp_prompt_kernel, tq=tq, nc=nc, n_top=n_top)
    return pl.pallas_call(
        kern,
        out_shape=(jax.ShapeDtypeStruct((b, t, NSA_W), F32), jax.ShapeDtypeStruct((b, g, t, t), F32)),
        grid=(b, g, t // tq),
        in_specs=[
            pl.BlockSpec((1, tq, GROUP_W), lambda bi, gi, qi: (bi, qi, kq + gi)),
            pl.BlockSpec((1, 1, nc, HEAD_DIM), lambda bi, gi, qi: (gi, bi, 0, 0)),
            pl.BlockSpec((1, 1, nc, HEAD_DIM), lambda bi, gi, qi: (gi, bi, 0, 0)),
            pl.BlockSpec((NSA_GROUP, tq, nc), lambda bi, gi, qi: (gi, qi, 0)),
            pl.BlockSpec((ns, t), lambda bi, gi, qi: (0, 0)),
        ],
        out_specs=(
            pl.BlockSpec((1, tq, GROUP_W), lambda bi, gi, qi: (bi, qi, gi)),
            pl.BlockSpec((1, 1, tq, t), lambda bi, gi, qi: (bi, gi, qi, 0)),
        ),
        compiler_params=_params("parallel", "parallel", "parallel"),
        name="nsa_cmp_prompt",
    )(proj3, ck_perm, cv_perm, bias_cmp, expand)


NSA_TQ = 128
NSA_TK = 512


def _nsa_band_kernel(*refs, nr, ratio, o_max, mode, has_sel, combine):
    q_ref, k_ref, v_ref, bias_ref = refs[:4]
    o_ref, qs, m_sc, l_sc, acc_sc = refs[-5:]
    extra = list(refs[4:-5])
    sel_ref = extra.pop(0) if has_sel else None
    small_ref, oc_ref, os_ref = extra if combine else (None, None, None)
    qi = pl.program_id(2)
    r = pl.program_id(3)
    tq = NSA_TQ
    tk = k_ref.shape[1]
    hg_n = NSA_GROUP

    @pl.when(r == 0)
    def _():
        for hg in range(hg_n):
            qs[hg * tq:(hg + 1) * tq, :] = q_ref[0, :, hg * HEAD_DIM:(hg + 1) * HEAD_DIM].astype(BF16)
        m_sc[...] = jnp.full_like(m_sc, NEG_INF)
        l_sc[...] = jnp.zeros_like(l_sc)
        acc_sc[...] = jnp.zeros_like(acc_sc)

    if mode == "causal":
        active = r <= qi // ratio
        kj = jnp.minimum(r, qi // ratio)
    else:
        active = qi // ratio - (nr - 1) + r >= 0
        kj = jnp.maximum(qi // ratio - (nr - 1) + r, 0)
    chunk0 = o_max - jnp.clip(qi - ratio * kj, 0, o_max)

    @pl.when(active)
    def _():
        s = _dot_nt(qs[...], k_ref[0].astype(BF16)) * SCALE
        bias = jnp.concatenate([bias_ref[:, chunk0 + u] for u in range(ratio)], axis=-1)
        s3 = s.reshape(hg_n, tq, tk) + bias
        if has_sel:
            s3 = s3 + sel_ref[0, 0][None]
        _online_softmax_step(s3.reshape(hg_n * tq, tk), v_ref[0].astype(BF16), m_sc, l_sc, acc_sc)

    @pl.when(r == nr - 1)
    def _():
        o = acc_sc[...] / jnp.maximum(l_sc[...], 1e-30)
        if combine:
            gates = _sigmoid(small_ref[0])
            second_group = pl.program_id(1) == 1
        for hg in range(hg_n):
            sl = slice(hg * HEAD_DIM, (hg + 1) * HEAD_DIM)
            o_h = o[hg * tq:(hg + 1) * tq, :]
            if combine:
                def gate(branch):
                    c = FOX_HEADS + branch * NSA_HEADS + hg
                    return jnp.where(second_group, gates[:, c + hg_n:c + hg_n + 1], gates[:, c:c + 1])

                o_h = gate(0) * oc_ref[0, :, sl] + gate(1) * os_ref[0, :, sl] + gate(2) * o_h
            o_ref[0, :, sl] = o_h.astype(o_ref.dtype)


def nsa_band(proj3, bias_strip, tk, col_k, col_v, mode, selmask=None, combine_with=None):
    b, t, _ = proj3.shape
    tq = NSA_TQ
    assert t % tk == 0 and tk % tq == 0
    ratio = tk // tq
    nq, nk = t // tq, t // tk
    g = NSA_KV_HEADS
    n_chunks = bias_strip.shape[1]
    o_max = n_chunks - ratio
    nr = nk if mode == "causal" else -(-WINDOW // tk) + 1
    assert mode == "causal" or o_max == ratio * nr - 1
    kq = COL_NQ // GROUP_W
    ck, cv = col_k // HEAD_DIM, col_v // HEAD_DIM
    if mode == "causal":
        def ktile(qi, r):
            return jnp.minimum(r, qi // ratio)
    else:
        def ktile(qi, r):
            return jnp.maximum(qi // ratio - (nr - 1) + r, 0)

    in_specs = [
        pl.BlockSpec((1, tq, GROUP_W), lambda bi, gi, qi, r: (bi, qi, kq + gi)),
        pl.BlockSpec((1, tk, HEAD_DIM), lambda bi, gi, qi, r: (bi, ktile(qi, r), ck + gi)),
        pl.BlockSpec((1, tk, HEAD_DIM), lambda bi, gi, qi, r: (bi, ktile(qi, r), cv + gi)),
        pl.BlockSpec((NSA_GROUP, n_chunks, tq, 128), lambda bi, gi, qi, r: (gi, 0, 0, 0)),
    ]
    args = [proj3, proj3, proj3, bias_strip]
    if selmask is not None:
        in_specs.append(pl.BlockSpec((1, 1, tq, tk), lambda bi, gi, qi, r: (bi, gi, qi, ktile(qi, r))))
        args.append(selmask)
    head_tile = pl.BlockSpec((1, tq, GROUP_W), lambda bi, gi, qi, r: (bi, qi, gi))
    if combine_with is not None:
        assert g == 2
        in_specs += [pl.BlockSpec((1, tq, 128), lambda bi, gi, qi, r: (bi, qi, COL_SMALL // 128)), head_tile, head_tile]
        args += [proj3, combine_with[0], combine_with[1]]
    kern = functools.partial(_nsa_band_kernel, nr=nr, ratio=ratio, o_max=o_max, mode=mode,
                             has_sel=selmask is not None, combine=combine_with is not None)
    rows = NSA_GROUP * tq
    return pl.pallas_call(
        kern,
        out_shape=jax.ShapeDtypeStruct((b, t, NSA_W), F32 if combine_with is None else BF16),
        grid=(b, g, nq, nr),
        in_specs=in_specs,
        out_specs=pl.BlockSpec((1, tq, GROUP_W), lambda bi, gi, qi, r: (bi, qi, gi)),
        scratch_shapes=[pltpu.VMEM((rows, HEAD_DIM), BF16), pltpu.VMEM((rows, 128), F32),
                        pltpu.VMEM((rows, 128), F32), pltpu.VMEM((rows, HEAD_DIM), F32)],
        compiler_params=_params("parallel", "parallel", "parallel", "arbitrary"),
        name="nsa_band_" + mode,
    )(*args)


_T5_THRESHOLDS = tuple(int(np.searchsorted(_bucket_np(np.arange(2 * MAX_DISTANCE)), k)) for k in range(1, N_BUCKETS))


def _t5_bias(rel_bias, dist):
    per_head = lambda k: rel_bias[k].reshape((-1,) + (1,) * dist.ndim)
    d = dist[None]
    out = jnp.broadcast_to(per_head(0), (rel_bias.shape[1],) + dist.shape)
    for k, thr in enumerate(_T5_THRESHOLDS, start=1):
        out = jnp.where(d >= thr, per_head(k), out)
    return out


def _band_strip(rel_bias, o_max, ratio, window=None):
    c = jnp.arange(o_max + ratio, dtype=jnp.int32)[:, None, None]
    i = jnp.arange(NSA_TQ, dtype=jnp.int32)[None, :, None]
    jj = jnp.arange(128, dtype=jnp.int32)[None, None, :]
    d = i - (c * 128 + jj) + o_max * NSA_TQ
    ok = d >= 0
    if window is not None:
        ok = ok & (d < window)
    return jnp.where(ok[None], _t5_bias(rel_bias, d), NEG_INF)


def _combine_kernel(small_ref, oc_ref, os_ref, ow_ref, o_ref):
    gates = _sigmoid(small_ref[...])
    for h in range(NSA_HEADS):
        sl = slice(h * HEAD_DIM, (h + 1) * HEAD_DIM)
        base = FOX_HEADS + h
        g0 = gates[:, base:base + 1]
        g1 = gates[:, base + NSA_HEADS:base + NSA_HEADS + 1]
        g2 = gates[:, base + 2 * NSA_HEADS:base + 2 * NSA_HEADS + 1]
        o = g0 * oc_ref[:, sl] + g1 * os_ref[:, sl] + g2 * ow_ref[:, sl]
        o_ref[:, sl] = o.astype(o_ref.dtype)


def nsa_combine(proj, o_cmp, o_slc, o_win):
    m = proj.shape[0]
    tm = min(m, 256)
    wide = pl.BlockSpec((tm, NSA_W), lambda i: (i, 0))
    return pl.pallas_call(
        _combine_kernel,
        out_shape=jax.ShapeDtypeStruct((m, NSA_W), BF16),
        grid=(m // tm,),
        in_specs=[pl.BlockSpec((tm, 128), lambda i: (i, COL_SMALL // 128)), wide, wide, wide],
        out_specs=wide,
        compiler_params=_params("parallel"),
        name="nsa_combine",
    )(proj, o_cmp, o_slc, o_win)


def _lru_kernel(xr_ref, gate_ref, cw_ref, cb_ref, wa_ref, wx_ref, ba_ref, bx_ref, lam_ref, buf_ref, h0_ref,
                y_ref, hlast_ref, nbuf_ref, xbuf, hcar, hsbuf, *, nt, tt, last_row):
    ti = pl.program_id(2)
    kw = LRU_CONV

    @pl.when(ti == 0)
    def _():
        xbuf[8 - (kw - 1):8, :] = buf_ref[0]
        hcar[...] = h0_ref[0]

    x = xr_ref[0]
    xbuf[8:8 + tt, :] = x
    xc = x * cw_ref[kw - 1:kw, :] + cb_ref[...]
    for t in range(kw - 1):
        off = 8 - (kw - 1) + t
        xc = xc + xbuf[off:off + tt, :] * cw_ref[t:t + 1, :]
    xcb = xc.astype(BF16)
    r = _sigmoid(_dot(xcb, wa_ref[0]) + ba_ref[...])
    ig = _sigmoid(_dot(xcb, wx_ref[0]) + bx_ref[...])
    neg_lam = -lam_ref[...]
    softplus = jnp.maximum(neg_lam, 0.0) + jnp.log1p(jnp.exp(-jnp.abs(neg_lam)))
    log_a = -LRU_C * r * softplus
    a = jnp.exp(log_a)
    th = jnp.tanh(log_a)
    u = jnp.sqrt(-2.0 * th / (1.0 - th)) * (ig * xc)
    ng = tt // 8
    a = a.reshape(ng, 8, a.shape[-1])
    u = u.reshape(ng, 8, u.shape[-1])
    sub = lax.broadcasted_iota(jnp.int32, a.shape, 1)
    for step in (1, 2, 4):
        keep = sub >= step
        a_sh = jnp.where(keep, pltpu.roll(a, step, 1), 1.0)
        u_sh = jnp.where(keep, pltpu.roll(u, step, 1), 0.0)
        u = a * u_sh + u
        a = a * a_sh
    h = hcar[...]
    for gi in range(ng):
        hs_g = a[gi] * h + u[gi]
        hsbuf[gi * 8:(gi + 1) * 8, :] = hs_g
        h = hs_g[7:8, :]
    hcar[...] = h
    y_ref[0] = (_gelu(gate_ref[0]) * hsbuf[...]).astype(y_ref.dtype)

    @pl.when(ti == nt - 1)
    def _():
        hlast_ref[0] = hsbuf[last_row:last_row + 1, :]
        nbuf_ref[0] = xbuf[8 + last_row - (kw - 2):8 + last_row + 1, :]

    xbuf[8 - (kw - 1):8, :] = xbuf[8 + tt - (kw - 1):8 + tt, :]


def lru_block(proj1, conv_w, conv_b, w_a, b_a, w_x, b_x, lam, conv_buf, h0, t_valid):
    b, tp, w2 = proj1.shape
    w = w2 // 2
    nh = LRU_HEADS
    bw = w // nh
    tt = min(tp, 512)
    assert tp % tt == 0 and tt % 8 == 0
    nt = tp // tt
    last_row = (t_valid - 1) - (nt - 1) * tt
    assert 0 <= last_row < tt and (nt > 1 or last_row >= 0)
    kw = LRU_CONV
    vec = lambda a: a.reshape(1, w)
    chan = pl.BlockSpec((1, bw), lambda bi, hi, ti: (0, hi))
    kern = functools.partial(_lru_kernel, nt=nt, tt=tt, last_row=last_row)
    return pl.pallas_call(
        kern,
        out_shape=(jax.ShapeDtypeStruct((b, tp, w), BF16), jax.ShapeDtypeStruct((b, 1, w), F32),
                   jax.ShapeDtypeStruct((b, kw - 1, w), F32)),
        grid=(b, nh, nt),
        in_specs=[
            pl.BlockSpec((1, tt, bw), lambda bi, hi, ti: (bi, ti, nh + hi)),
            pl.BlockSpec((1, tt, bw), lambda bi, hi, ti: (bi, ti, hi)),
            pl.BlockSpec((kw, bw), lambda bi, hi, ti: (0, hi)),
            chan,
            pl.BlockSpec((1, bw, bw), lambda bi, hi, ti: (hi, 0, 0)),
            pl.BlockSpec((1, bw, bw), lambda bi, hi, ti: (hi, 0, 0)),
            chan, chan, chan,
            pl.BlockSpec((1, kw - 1, bw), lambda bi, hi, ti: (bi, 0, hi)),
            pl.BlockSpec((1, 1, bw), lambda bi, hi, ti: (bi, 0, hi)),
        ],
        out_specs=(
            pl.BlockSpec((1, tt, bw), lambda bi, hi, ti: (bi, ti, hi)),
            pl.BlockSpec((1, 1, bw), lambda bi, hi, ti: (bi, 0, hi)),
            pl.BlockSpec((1, kw - 1, bw), lambda bi, hi, ti: (bi, 0, hi)),
        ),
        scratch_shapes=[pltpu.VMEM((8 + tt, bw), F32), pltpu.VMEM((1, bw), F32), pltpu.VMEM((tt, bw), F32)],
        compiler_params=_params("parallel", "parallel", "arbitrary"),
        name="rglru",
    )(proj1, proj1, conv_w, vec(conv_b), w_a.astype(BF16), w_x.astype(BF16), vec(b_a), vec(b_x), vec(lam),
      conv_buf, h0.reshape(b, 1, w))


GATHER_PAGES = 8


def _gather_kernel(pt_ref, *refs, n_pools, pg):
    ins, outs = refs[:n_pools * pg], refs[n_pools * pg:]
    for c in range(n_pools):
        for i in range(pg):
            src = ins[c * pg + i]
            rows = src.shape[1]
            outs[c][0, i * rows:(i + 1) * rows, :] = src[0]


def gather_pools(pools, page_table):
    b, n_pages = page_table.shape
    pg = math.gcd(GATHER_PAGES, n_pages)
    in_specs, args, out_shape, out_specs = [], [], [], []
    for pool in pools:
        _, rows, c = pool.shape
        assert (pg * rows) % 8 == 0
        for i in range(pg):
            in_specs.append(pl.BlockSpec((1, rows, c), lambda bi, j, pt, i=i: (pt[bi, j * pg + i], 0, 0)))
            args.append(pool)
        out_shape.append(jax.ShapeDtypeStruct((b, n_pages * rows, c), pool.dtype))
        out_specs.append(pl.BlockSpec((1, pg * rows, c), lambda bi, j, pt: (bi, j, 0)))
    out_shape, out_specs = tuple(out_shape), tuple(out_specs)
    return pl.pallas_call(
        functools.partial(_gather_kernel, n_pools=len(pools), pg=pg),
        out_shape=out_shape,
        grid_spec=pltpu.PrefetchScalarGridSpec(
            num_scalar_prefetch=1, grid=(b, n_pages // pg), in_specs=in_specs, out_specs=out_specs),
        compiler_params=_params("parallel", "arbitrary"),
        name="gather_pages",
    )(page_table, *args)


FOX_DECODE_PAGES = 8


def _fox_decode_kernel(pt_ref, q_ref, cq_ref, ck_ref, cknew_ref, *refs, n_steps, pf):
    kp, vp = refs[:pf], refs[pf:2 * pf]
    knew_ref, vnew_ref, o_ref, m_sc, l_sc, acc_sc, cq_sc = refs[2 * pf:]
    j = pl.program_id(1)
    rows = q_ref.shape[1]
    nh = FOX_HEADS
    w = PAGE_SIZE * nh

    @pl.when(j == 0)
    def _():
        m_sc[...] = jnp.full_like(m_sc, NEG_INF)
        l_sc[...] = jnp.zeros_like(l_sc)
        acc_sc[...] = jnp.zeros_like(acc_sc)
        col = lax.broadcasted_iota(jnp.int32, (rows, w), 1)
        row = lax.broadcasted_iota(jnp.int32, (rows, w), 0)
        cq_sc[...] = jnp.where((col % nh) == (row % nh), cq_ref[0], NEG_INF)

    def update(k3, v3, ck_row, causal):
        n = k3.shape[0] * nh
        k2 = k3.reshape(n, HEAD_DIM).astype(BF16)
        v2 = v3.reshape(n, HEAD_DIM).astype(BF16)
        s = _dot_nt(q_ref[0], k2) * SCALE + (cq_sc[:, :n] - ck_row)
        if causal:
            col = lax.broadcasted_iota(jnp.int32, (rows, n), 1)
            row = lax.broadcasted_iota(jnp.int32, (rows, n), 0)
            s = jnp.where(col // nh <= row // nh, s, NEG_INF)
        _online_softmax_step(s, v2, m_sc, l_sc, acc_sc)

    @pl.when(j < n_steps)
    def _():
        for i in range(pf):
            update(kp[i][0], vp[i][0], ck_ref[0, :, i * w:(i + 1) * w], False)

    @pl.when(j == n_steps)
    def _():
        update(knew_ref[0], vnew_ref[0], cknew_ref[0], True)
        o_ref[0] = acc_sc[...] / jnp.maximum(l_sc[...], 1e-30)


def fox_decode(q_rows, cq_rows, ck_past, ck_new, k_pool, v_pool, k_new, v_new, page_table):
    b, n_pages = page_table.shape
    pf = math.gcd(FOX_DECODE_PAGES, n_pages)
    n_steps = n_pages // pf
    rows = q_rows.shape[1]
    tp = k_new.shape[1]
    nh = FOX_HEADS

    def page(i):
        return lambda bi, j, pt: (pt[bi, jnp.minimum(j * pf + i, n_pages - 1)], 0, 0, 0)

    pool_spec = [pl.BlockSpec((1, PAGE_SIZE, nh, HEAD_DIM), page(i)) for i in range(pf)]
    per_seq = lambda shape: pl.BlockSpec(shape, lambda bi, j, pt: (bi,) + (0,) * (len(shape) - 1))
    kern = functools.partial(_fox_decode_kernel, n_steps=n_steps, pf=pf)
    return pl.pallas_call(
        kern,
        out_shape=jax.ShapeDtypeStruct((b, rows, HEAD_DIM), F32),
        grid_spec=pltpu.PrefetchScalarGridSpec(
            num_scalar_prefetch=1,
            grid=(b, n_steps + 1),
            in_specs=[per_seq((1, rows, HEAD_DIM)), per_seq((1, rows, 1)),
                      pl.BlockSpec((1, 1, pf * PAGE_SIZE * nh), lambda bi, j, pt: (bi, 0, jnp.minimum(j, n_steps - 1))),
                      per_seq((1, 1, tp * nh))]
                     + pool_spec + pool_spec
                     + [per_seq((1, tp, nh, HEAD_DIM)), per_seq((1, tp, nh, HEAD_DIM))],
            out_specs=per_seq((1, rows, HEAD_DIM)),
            scratch_shapes=[pltpu.VMEM((rows, 128), F32), pltpu.VMEM((rows, 128), F32),
                            pltpu.VMEM((rows, HEAD_DIM), F32), pltpu.VMEM((rows, PAGE_SIZE * nh), F32)],
        ),
        compiler_params=_params("parallel", "arbitrary"),
        name="fox_decode",
    )(page_table, q_rows, cq_rows, ck_past, ck_new, *([k_pool] * pf), *([v_pool] * pf), k_new, v_new)


SLC_DECODE_PAGES = 16


def _slc_decode_kernel(pt_ref, q_ref, bias_ref, biasnew_ref, *refs, n_steps, pf):
    kp, vp = refs[:pf], refs[pf:2 * pf]
    knew_ref, vnew_ref, o_ref, m_sc, l_sc, acc_sc = refs[2 * pf:]
    j = pl.program_id(1)

    @pl.when(j == 0)
    def _():
        m_sc[...] = jnp.full_like(m_sc, NEG_INF)
        l_sc[...] = jnp.zeros_like(l_sc)
        acc_sc[...] = jnp.zeros_like(acc_sc)

    def update(g, k2, v2, bias):
        s = _dot_nt(q_ref[0, g], k2.astype(BF16)) * SCALE + bias
        _online_softmax_step(s, v2.astype(BF16), m_sc.at[g], l_sc.at[g], acc_sc.at[g])

    @pl.when(j < n_steps)
    def _():
        for g in range(q_ref.shape[1]):
            k2 = jnp.concatenate([kp[i][0, :, g, :] for i in range(pf)], axis=0)
            v2 = jnp.concatenate([vp[i][0, :, g, :] for i in range(pf)], axis=0)
            update(g, k2, v2, bias_ref[0, g])

    @pl.when(j == n_steps)
    def _():
        for g in range(q_ref.shape[1]):
            update(g, knew_ref[0, :, g, :], vnew_ref[0, :, g, :], biasnew_ref[0, g])
        o_ref[0] = acc_sc[...] / jnp.maximum(l_sc[...], 1e-30)


def slc_decode(q_rows, bias_past, bias_new, k_pool, v_pool, k_new, v_new, page_table):
    b, n_pages = page_table.shape
    pf = math.gcd(SLC_DECODE_PAGES, n_pages)
    n_steps = n_pages // pf
    g, rows = q_rows.shape[1], q_rows.shape[2]
    tp = k_new.shape[1]
    assert tp % 128 == 0

    def page(i):
        return lambda bi, j, pt: (pt[bi, jnp.minimum(j * pf + i, n_pages - 1)], 0, 0, 0)

    pool_spec = [pl.BlockSpec((1, PAGE_SIZE, g, HEAD_DIM), page(i)) for i in range(pf)]
    per_seq = lambda shape: pl.BlockSpec(shape, lambda bi, j, pt: (bi,) + (0,) * (len(shape) - 1))
    return pl.pallas_call(
        functools.partial(_slc_decode_kernel, n_steps=n_steps, pf=pf),
        out_shape=jax.ShapeDtypeStruct((b, g, rows, HEAD_DIM), F32),
        grid_spec=pltpu.PrefetchScalarGridSpec(
            num_scalar_prefetch=1,
            grid=(b, n_steps + 1),
            in_specs=[per_seq((1, g, rows, HEAD_DIM)),
                      pl.BlockSpec((1, g, rows, pf * PAGE_SIZE),
                                   lambda bi, j, pt: (bi, 0, 0, jnp.minimum(j, n_steps - 1))),
                      per_seq((1, g, rows, tp))]
                     + pool_spec + pool_spec
                     + [per_seq((1, tp, g, HEAD_DIM)), per_seq((1, tp, g, HEAD_DIM))],
            out_specs=per_seq((1, g, rows, HEAD_DIM)),
            scratch_shapes=[pltpu.VMEM((g, rows, 128), F32), pltpu.VMEM((g, rows, 128), F32),
                            pltpu.VMEM((g, rows, HEAD_DIM), F32)],
        ),
        compiler_params=_params("parallel", "arbitrary"),
        name="nsa_slc_decode",
    )(page_table, q_rows, bias_past, bias_new, *([k_pool] * pf), *([v_pool] * pf), k_new, v_new)


def _cmp_small_kernel(q_ref, k_ref, v_ref, bias_ref, o_ref, imp_ref, *, t_new):
    s = _dot_nt(q_ref[0, 0], k_ref[0].astype(BF16)) * SCALE + bias_ref[0]
    mask = s > VISIBLE_MIN
    m = jnp.max(s, axis=-1, keepdims=True)
    e = jnp.where(mask, jnp.exp(s - m), 0.0)
    p = e / jnp.maximum(jnp.sum(e, axis=-1, keepdims=True), 1e-30)
    o_ref[0, 0] = _dot(p.astype(BF16), v_ref[0].astype(BF16))
    imp_ref[0, 0] = jnp.sum(p.reshape(t_new, NSA_GROUP, p.shape[-1]), axis=1)


def cmp_small(q_rows, ck, cv, bias, t_new):
    b, g, rows, d = q_rows.shape
    s = ck.shape[1]
    return pl.pallas_call(
        functools.partial(_cmp_small_kernel, t_new=t_new),
        out_shape=(jax.ShapeDtypeStruct((b, g, rows, d), F32), jax.ShapeDtypeStruct((b, g, t_new, s), F32)),
        grid=(b, g),
        in_specs=[
            pl.BlockSpec((1, 1, rows, d), lambda bi, gi: (bi, gi, 0, 0)),
            pl.BlockSpec((1, s, d), lambda bi, gi: (bi, 0, gi)),
            pl.BlockSpec((1, s, d), lambda bi, gi: (bi, 0, gi)),
            pl.BlockSpec((1, rows, s), lambda bi, gi: (gi, 0, 0)),
        ],
        out_specs=(pl.BlockSpec((1, 1, rows, d), lambda bi, gi: (bi, gi, 0, 0)),
                   pl.BlockSpec((1, 1, t_new, s), lambda bi, gi: (bi, gi, 0, 0))),
        compiler_params=_params("parallel", "parallel"),
        name="nsa_cmp_sample",
    )(q_rows, ck, cv, bias)


def _flash_small_kernel(q_ref, k_ref, v_ref, bias_ref, o_ref, m_sc, l_sc, acc_sc, *, nk):
    j = pl.program_id(2)

    @pl.when(j == 0)
    def _():
        m_sc[...] = jnp.full_like(m_sc, NEG_INF)
        l_sc[...] = jnp.zeros_like(l_sc)
        acc_sc[...] = jnp.zeros_like(acc_sc)

    s = _dot_nt(q_ref[0, 0], k_ref[0, 0].astype(BF16)) * SCALE + bias_ref[0, 0]
    vis = s > VISIBLE_MIN
    m_prev = m_sc[...]
    m_new = jnp.maximum(m_prev, jnp.max(s, axis=-1, keepdims=True))
    alpha = jnp.exp(m_prev - m_new)
    p = jnp.where(vis, jnp.exp(s - m_new), 0.0)
    l_sc[...] = alpha * l_sc[...] + jnp.sum(p, axis=-1, keepdims=True)
    acc_sc[...] = alpha * acc_sc[...] + _dot(p.astype(BF16), v_ref[0, 0].astype(BF16))
    m_sc[...] = m_new

    @pl.when(j == nk - 1)
    def _():
        o_ref[0, 0] = acc_sc[...] / jnp.maximum(l_sc[...], 1e-30)


def flash_small(q_rows, k, v, bias, tk):
    b, g, rows, d = q_rows.shape
    s = k.shape[2]
    assert s % tk == 0
    nk = s // tk
    return pl.pallas_call(
        functools.partial(_flash_small_kernel, nk=nk),
        out_shape=jax.ShapeDtypeStruct((b, g, rows, d), F32),
        grid=(b, g, nk),
        in_specs=[
            pl.BlockSpec((1, 1, rows, d), lambda bi, gi, j: (bi, gi, 0, 0)),
            pl.BlockSpec((1, 1, tk, d), lambda bi, gi, j: (bi, gi, j, 0)),
            pl.BlockSpec((1, 1, tk, d), lambda bi, gi, j: (bi, gi, j, 0)),
            pl.BlockSpec((1, 1, rows, tk), lambda bi, gi, j: (bi, gi, 0, j)),
        ],
        out_specs=pl.BlockSpec((1, 1, rows, d), lambda bi, gi, j: (bi, gi, 0, 0)),
        scratch_shapes=[pltpu.VMEM((rows, 1), F32), pltpu.VMEM((rows, 1), F32), pltpu.VMEM((rows, d), F32)],
        compiler_params=_params("parallel", "parallel", "arbitrary"),
        name="nsa_flash_sample",
    )(q_rows, k, v, bias)


def _topk_kernel(col_ref, row_ref, o_ref, *, n_top):
    a = col_ref[...]
    bb = row_ref[...]
    shape = (a.shape[0], a.shape[1], bb.shape[2])
    j_idx = lax.broadcasted_iota(jnp.int32, shape, 1)
    i_idx = lax.broadcasted_iota(jnp.int32, shape, 2)
    ahead = (a > bb) | ((a == bb) & (j_idx < i_idx))
    rank = jnp.sum(jnp.where(ahead, 1.0, 0.0), axis=1, keepdims=True)
    o_ref[...] = jnp.where(rank < n_top, 1.0, 0.0)


def topk_mask(score, n_top):
    r, ns = score.shape
    tr = 8 if r % 8 == 0 else r
    out = pl.pallas_call(
        functools.partial(_topk_kernel, n_top=n_top),
        out_shape=jax.ShapeDtypeStruct((r, 1, ns), F32),
        grid=(r // tr,),
        in_specs=[pl.BlockSpec((tr, ns, 1), lambda i: (i, 0, 0)), pl.BlockSpec((tr, 1, ns), lambda i: (i, 0, 0))],
        out_specs=pl.BlockSpec((tr, 1, ns), lambda i: (i, 0, 0)),
        compiler_params=_params("parallel"),
        name="nsa_topk",
    )(score.reshape(r, ns, 1), score.reshape(r, 1, ns))
    return out.reshape(r, ns)


def _pack_w_in0(w_in0):
    d = w_in0.shape[0]
    off_fz = 3 * FOX_W
    off_nq = off_fz + FOX_HEADS
    off_kv = off_nq + NSA_W
    off_ng = off_kv + 6 * NSA_KV_W
    parts = [w_in0[:, :off_fz], w_in0[:, off_nq:off_ng], w_in0[:, off_fz:off_nq], w_in0[:, off_ng:]]
    used = sum(p.shape[1] for p in parts)
    parts.append(jnp.zeros((d, L0_PACKED - used), w_in0.dtype))
    return jnp.concatenate(parts, axis=1).astype(BF16)


def _forget_terms(proj3, b_forget):
    fz = proj3[:, :, COL_SMALL:COL_SMALL + FOX_HEADS]
    lf = jax.nn.log_sigmoid(fz + b_forget.astype(F32))
    return lf, jnp.cumsum(lf, axis=1)


def _kv_cols(proj3, idx):
    off = COL_KV + idx * NSA_KV_W
    return proj3[:, :, off:off + NSA_KV_W]


def attn_prompt(h_bf16, b, t, wts):
    m = b * t
    g = NSA_KV_HEADS
    proj = matmul(h_bf16, wts["w_in0"])
    proj3 = proj.reshape(b, t, L0_PACKED)
    lf, c_new = _forget_terms(proj3, wts["b_forget"])
    o_fox = fox_prompt(proj3, c_new)

    kc, vc, ks, vs, kw, vw = [_kv_cols(proj3, i) for i in range(6)]
    nc, ns = t // CMP_BLOCK, t // SEL_BLOCK
    tk = min(NSA_TK, t)
    assert t % (2 * CMP_BLOCK) == 0 and t % tk == 0
    ck = compress(proj, COL_KV // HEAD_DIM, g, wts["cmp_k"]).reshape(g, b, nc, HEAD_DIM)
    cv = compress(proj, (COL_KV + NSA_KV_W) // HEAD_DIM, g, wts["cmp_v"]).reshape(g, b, nc, HEAD_DIM)
    even_odd = lambda a: jnp.concatenate([a[:, :, 0::2], a[:, :, 1::2]], axis=2)
    rel = wts["rel_bias"].astype(F32)
    blk_c = jnp.concatenate([jnp.arange(0, nc, 2, dtype=jnp.int32), jnp.arange(1, nc, 2, dtype=jnp.int32)])
    c_dist = jnp.arange(t, dtype=jnp.int32)[:, None] - ((blk_c + 1) * CMP_BLOCK - 1)[None, :]
    bias_cmp = _t5_bias(rel, c_dist)
    o_cmp, selmask = cmp_prompt(proj3, even_odd(ck), even_odd(cv), bias_cmp)

    ratio = tk // NSA_TQ
    o_far = -(-(_T5_THRESHOLDS[-1] + tk - 1) // NSA_TQ)
    o_slc = nsa_band(proj3, _band_strip(rel, o_far, ratio), tk, COL_KV + 2 * NSA_KV_W, COL_KV + 3 * NSA_KV_W,
                     "causal", selmask)
    o_win_max = ratio * (-(-WINDOW // tk) + 1) - 1
    o_nsa = nsa_band(proj3, _band_strip(rel, o_win_max, ratio, window=WINDOW), tk,
                     COL_KV + 4 * NSA_KV_W, COL_KV + 5 * NSA_KV_W, "window", combine_with=(o_cmp, o_slc))
    mix = (o_fox.reshape(m, FOX_W), o_nsa.reshape(m, NSA_W))
    win_len = wts["win_len"]
    zeros = jnp.zeros((b, WINDOW, NSA_KV_W), F32)
    kw_a = jnp.concatenate([zeros, kw], axis=1)
    vw_a = jnp.concatenate([zeros, vw], axis=1)
    fk = proj3[:, :, COL_FK:COL_FK + FOX_W]
    fv = proj3[:, :, COL_FV:COL_FV + FOX_W]
    new = (fk, fv, lf, kc, vc, ks, vs, kw_a[:, kw_a.shape[1] - win_len:], vw_a[:, vw_a.shape[1] - win_len:])
    return mix, new


def attn_sample(h_bf16, b, t, wts, caches, page_table, state_win_k, state_win_v):
    m = b * t
    g, hg = NSA_KV_HEADS, NSA_GROUP
    n_pages = page_table.shape[1]
    p = n_pages * PAGE_SIZE
    assert t <= PAGE_SIZE and p % SEL_BLOCK == 0
    proj = matmul(h_bf16, wts["w_in0"])
    proj3 = proj.reshape(b, t, L0_PACKED)
    lf, c_new = _forget_terms(proj3, wts["b_forget"])
    kc, vc, ks, vs, kw, vw = [_kv_cols(proj3, i) for i in range(6)]
    fq = proj3[:, :, COL_FQ:COL_FQ + FOX_W]
    fk = proj3[:, :, COL_FK:COL_FK + FOX_W]
    fv = proj3[:, :, COL_FV:COL_FV + FOX_W]
    q_pos = p + np.arange(t)

    cache_fox_k, cache_fox_v, cache_fox_lf, cache_cmp_k, cache_cmp_v, cache_slc_k, cache_slc_v = caches
    assert PAGE_SIZE % CMP_BLOCK == 0
    per_page = PAGE_SIZE // CMP_BLOCK
    n_phys = cache_cmp_k.shape[0]
    plf, ck, cv = gather_pools(
        [cache_fox_lf,
         compress_pool(cache_cmp_k, wts["cmp_k"]).reshape(n_phys, per_page, NSA_KV_W),
         compress_pool(cache_cmp_v, wts["cmp_v"]).reshape(n_phys, per_page, NSA_KV_W)], page_table)

    c_past = plf - lax.cumsum(plf, axis=1, reverse=True)
    tp = -(-t // 8) * 8
    pad_new = lambda a: jnp.concatenate([a, jnp.zeros((b, tp - t) + a.shape[2:], F32)], axis=1)
    o_fox = fox_decode(fq.reshape(b, t * FOX_HEADS, HEAD_DIM).astype(BF16), c_new.reshape(b, t * FOX_HEADS, 1),
                       c_past.reshape(b, 1, p * FOX_HEADS), pad_new(c_new).reshape(b, 1, tp * FOX_HEADS),
                       cache_fox_k, cache_fox_v,
                       pad_new(fk.reshape(b, t, FOX_HEADS, HEAD_DIM)), pad_new(fv.reshape(b, t, FOX_HEADS, HEAD_DIM)),
                       page_table).reshape(m, FOX_W)

    rel = wts["rel_bias"].astype(F32)
    nq = proj3[:, :, COL_NQ:COL_NQ + NSA_W].reshape(b, t, g, hg, HEAD_DIM)
    q_rows = nq.transpose(0, 2, 1, 3, 4).reshape(b, g, t * hg, HEAD_DIM).astype(BF16)

    def head_rows(x):
        tt, ss = x.shape[1], x.shape[2]
        return x.reshape(g, hg, tt, ss).transpose(0, 2, 1, 3).reshape(g, tt * hg, ss)

    l_tot = p + t
    l_pad = -(-l_tot // SEL_BLOCK) * SEL_BLOCK
    nc, ns = l_pad // CMP_BLOCK, l_pad // SEL_BLOCK
    nc_past = p // CMP_BLOCK
    assert nc_past % 2 == 0 and (nc_past + 1) * CMP_BLOCK - 1 > q_pos[-1]
    even_odd = lambda a: jnp.concatenate([a[:, 0::2], a[:, 1::2]], axis=1)
    blk_c = jnp.concatenate([jnp.arange(0, nc_past, 2, dtype=jnp.int32), jnp.arange(1, nc_past, 2, dtype=jnp.int32)])
    c_dist = jnp.asarray(q_pos, jnp.int32)[:, None] - ((blk_c + 1) * CMP_BLOCK - 1)[None, :]
    bias_cmp = jnp.where((c_dist >= 0)[None], _t5_bias(rel, c_dist), NEG_INF)
    o_cmp, imp = cmp_small(q_rows, even_odd(ck), even_odd(cv), head_rows(bias_cmp), t)
    ns_past = nc_past // 2
    imp_s = imp[..., :ns_past] + imp[..., ns_past:]
    imp_s = jnp.concatenate([imp_s, jnp.zeros((b, g, t, ns - ns_past), F32)], axis=-1)
    blk = np.arange(ns)[None, :]
    cur = (q_pos // SEL_BLOCK)[:, None]
    valid = blk * SEL_BLOCK <= q_pos[:, None]
    forced = (blk == 0) | (blk == cur) | (blk == cur - 1)
    score = jnp.where(jnp.asarray(forced), SEL_FORCE, jnp.where(jnp.asarray(valid), imp_s, -1.0))
    sel = topk_mask(score.reshape(b * g * t, ns), min(N_SEL, ns)).reshape(b, g, t, ns)

    qp = jnp.asarray(q_pos, jnp.int32)[:, None]
    n_full = p // SEL_BLOCK
    t_pad = 128
    assert ns - 1 == n_full and t <= SEL_BLOCK

    def slc_bias(dist, visible, sel_keys):
        s = dist.shape[1]
        t5 = head_rows(jnp.where(visible[None], _t5_bias(rel, dist), NEG_INF))
        chosen = jnp.broadcast_to(sel_keys[:, :, :, None, :], (b, g, t, hg, s)).reshape(b, g, t * hg, s)
        return jnp.where(chosen > 0.5, t5[None], NEG_INF)

    d_past = qp - jnp.arange(p, dtype=jnp.int32)[None, :]
    bias_past = slc_bias(d_past, d_past >= 0, jnp.repeat(sel[..., :n_full], SEL_BLOCK, axis=-1))
    d_new = qp - (p + jnp.arange(t_pad, dtype=jnp.int32))[None, :]
    new_ok = (d_new >= 0) & (jnp.arange(t_pad)[None, :] < t)
    bias_new = slc_bias(d_new, new_ok, jnp.broadcast_to(sel[..., ns - 1:ns], (b, g, t, t_pad)))
    pad_rows = lambda a: jnp.concatenate(
        [a.reshape(b, t, g, HEAD_DIM), jnp.zeros((b, t_pad - t, g, HEAD_DIM), F32)], axis=1)
    o_slc = slc_decode(q_rows, bias_past, bias_new, cache_slc_k, cache_slc_v, pad_rows(ks), pad_rows(vs), page_table)

    def head_major(a, rows):
        a = a.reshape(b, t, g, HEAD_DIM).transpose(0, 2, 1, 3)
        return jnp.concatenate([a, jnp.zeros((b, g, rows - t, HEAD_DIM), F32)], axis=2)

    p_w = state_win_k.shape[1]
    s_win = -(-(p_w + t) // 128) * 128
    kw_a = jnp.concatenate([state_win_k.reshape(b, p_w, NSA_KV_W), kw], axis=1)
    vw_a = jnp.concatenate([state_win_v.reshape(b, p_w, NSA_KV_W), vw], axis=1)
    win_major = lambda st, new: jnp.concatenate([st.transpose(0, 2, 1, 3), head_major(new, s_win - p_w)], axis=2)
    w_idx = jnp.arange(s_win, dtype=jnp.int32)[None, :]
    wpos = (p - p_w) + w_idx
    wd = qp - wpos
    wok = (wd >= 0) & (wd < WINDOW) & (wpos >= 0) & (w_idx < p_w + t)
    bias_win = head_rows(jnp.where(wok[None], _t5_bias(rel, wd), NEG_INF))
    bias_win = jnp.broadcast_to(bias_win[None], (b,) + bias_win.shape)
    o_win = flash_small(q_rows, win_major(state_win_k, kw), win_major(state_win_v, vw), bias_win, s_win)

    def to_tokens(o):
        return o.reshape(b, g, t, hg, HEAD_DIM).transpose(0, 2, 1, 3, 4).reshape(m, NSA_W)

    o_nsa = nsa_combine(proj, to_tokens(o_cmp), to_tokens(o_slc), to_tokens(o_win))
    mix = (o_fox.astype(BF16), o_nsa)
    win_len = wts["win_len"]
    new = (fk, fv, lf, kc, vc, ks, vs, kw_a[:, kw_a.shape[1] - win_len:], vw_a[:, vw_a.shape[1] - win_len:])
    return mix, new


def _weight_matmul(wts, key, layer, call):
    cache = wts["bf16"]
    if (key, layer) in cache:
        return call(cache[(key, layer)], None)
    result = call(wts[key], layer)
    if isinstance(result, tuple):
        result, cache[(key, layer)] = result
    return result


def trunk(x, p_emb, wts, lru_conv0, lru_h0, ffn_conv0, past=None):
    b, t, d = x.shape
    m = b * t
    x2 = x.reshape(m, d)
    ffn_bufs = []
    for i in range(2):
        h = rmsnorm(x2, wts["norm_mix"][i], BF16)
        if i == 0:
            if past is None:
                mix, attn_new = attn_prompt(h, b, t, wts)
            else:
                mix, attn_new = attn_sample(h, b, t, wts, *past)
            x2 = _weight_matmul(wts, "w_out0", None, lambda w, _, r=x2: matmul_pair(mix[0], mix[1], w, r))
        else:
            proj1 = _weight_matmul(wts, "w_in1", None, lambda w, _: matmul(h, w)).reshape(b, t, -1)
            tp = -(-t // 8) * 8
            if tp != t:
                proj1 = jnp.concatenate([proj1, jnp.zeros((b, tp - t, proj1.shape[-1]), F32)], axis=1)
            gated, h_last, lru_buf = lru_block(proj1, wts["lru_conv_w"], wts["lru_conv_b"], wts["lru_w_a"],
                                               wts["lru_b_a"], wts["lru_w_x"], wts["lru_b_x"], wts["lru_lambda"],
                                               lru_conv0, lru_h0, t)
            g_in = gated[:, :t].reshape(m, -1)
            x2 = _weight_matmul(wts, "w_out1", None, lambda w, _: matmul(g_in, w, res=x2))
        hf = rmsnorm(x2, wts["norm_ffn"][i], BF16)
        dff = wts["ffn_w_up"].shape[2] // 2
        if t % 8 == 0:
            if ("ffn_w_up", i) not in wts["bf16"]:
                wts["bf16"][("ffn_w_up", i)] = wts["ffn_w_up"][i].astype(BF16)
            act, buf = ffn_up_fused(hf, wts["bf16"][("ffn_w_up", i)], wts["ffn_conv_w"][i], wts["ffn_conv_b"][i],
                                    ffn_conv0[i], t)
        else:
            gu = _weight_matmul(wts, "ffn_w_up", i, lambda w, l: matmul(hf, w, layer=l))
            xp = jnp.concatenate([ffn_conv0[i], gu[:, :dff].reshape(b, t, dff)], axis=1)
            taps = [xp[:, FFN_CONV - 1 - s:FFN_CONV - 1 - s + t].reshape(m, dff) for s in range(FFN_CONV)]
            act = convgate(taps[0], taps[1], taps[2], gu[:, dff:], wts["ffn_conv_w"][i], wts["ffn_conv_b"][i])
            buf = xp[:, xp.shape[1] - (FFN_CONV - 1):]
        ffn_bufs.append(buf)
        x2 = _weight_matmul(wts, "ffn_w_down", i, lambda w, l: matmul(act, w, layer=l, res=x2))
        hp = rmsnorm(x2, wts["ple_norm"][i], BF16)
        p_in = p_emb[i].reshape(m, -1).astype(BF16)
        emb = _weight_matmul(wts, "ple_w_proj", i, lambda w, l: matmul(p_in, w, layer=l))
        x2 = _weight_matmul(wts, "ple_w_gate", i, lambda w, l: matmul(hp, w, layer=l, res=x2, aux=emb))
    y = rmsnorm(x2, wts["final_norm"], F32).reshape(b, t, d)
    return y, attn_new, lru_buf, h_last.reshape(b, -1), jnp.stack(ffn_bufs)


def kernel(x_prompt, x_sample, cache_fox_k, cache_fox_v, cache_fox_lf, cache_cmp_k, cache_cmp_v, cache_slc_k, cache_slc_v, state_win_k, state_win_v, state_lru_conv, state_lru_h, state_ffn_conv, page_table, p_prompt, p_sample, norm_mix, norm_ffn, final_norm, w_in0, b_forget, cmp_w1_k, cmp_pe_k, cmp_w2_k, cmp_w1_v, cmp_pe_v, cmp_w2_v, rel_bias, w_out0, w_in1, lru_conv_w, lru_conv_b, lru_w_a, lru_b_a, lru_w_x, lru_b_x, lru_lambda, w_out1, ffn_w_up, ffn_conv_w, ffn_conv_b, ffn_w_down, ple_w_proj, ple_w_gate, ple_norm):
    depth = norm_mix.shape[0]
    assert depth == 2
    d = x_prompt.shape[-1]
    dff = ffn_w_down.shape[1]
    wts = {
        "norm_mix": norm_mix, "norm_ffn": norm_ffn, "ple_norm": ple_norm, "final_norm": final_norm,
        "w_in0": _pack_w_in0(w_in0), "b_forget": b_forget, "rel_bias": rel_bias,
        "win_len": state_win_k.shape[1],
        "w_out0": w_out0, "w_in1": w_in1, "w_out1": w_out1,
        "lru_conv_w": lru_conv_w, "lru_conv_b": lru_conv_b, "lru_w_a": lru_w_a, "lru_b_a": lru_b_a,
        "lru_w_x": lru_w_x, "lru_b_x": lru_b_x, "lru_lambda": lru_lambda,
        "ffn_w_up": ffn_w_up, "ffn_conv_w": ffn_conv_w, "ffn_conv_b": ffn_conv_b,
        "ffn_w_down": ffn_w_down, "ple_w_proj": ple_w_proj, "ple_w_gate": ple_w_gate,
        "bf16": {},
    }
    wts["cmp_k"] = (cmp_w1_k, cmp_pe_k, cmp_w2_k)
    wts["cmp_v"] = (cmp_w1_v, cmp_pe_v, cmp_w2_v)

    caches = (cache_fox_k, cache_fox_v, cache_fox_lf, cache_cmp_k, cache_cmp_v, cache_slc_k, cache_slc_v)
    y_s, attn_s, lru_conv_s, lru_h_s, ffn_conv_s = trunk(
        x_sample, p_sample, wts, state_lru_conv, state_lru_h, state_ffn_conv,
        past=(caches, page_table, state_win_k, state_win_v))
    bp = x_prompt.shape[0]
    y_p, attn_p, lru_conv_p, lru_h_p, ffn_conv_p = trunk(
        x_prompt, p_prompt, wts,
        jnp.zeros((bp, LRU_CONV - 1, d), F32), jnp.zeros((bp, d), F32),
        jnp.zeros((depth, bp, FFN_CONV - 1, dff), F32))

    def shape_attn(new, b, t):
        fk, fv, lf, kc, vc, ks, vs, wk, wv = new
        h4 = lambda a: a.reshape(b, a.shape[1], FOX_HEADS, HEAD_DIM)
        g4 = lambda a: a.reshape(b, a.shape[1], NSA_KV_HEADS, HEAD_DIM)
        return h4(fk), h4(fv), lf, g4(kc), g4(vc), g4(ks), g4(vs), g4(wk), g4(wv)

    ap = shape_attn(attn_p, bp, x_prompt.shape[1])
    asmp = shape_attn(attn_s, x_sample.shape[0], x_sample.shape[1])
    out = [y_p, y_s]
    for a, s in zip(ap, asmp):
        out += [a, s]
    out += [lru_conv_p, lru_conv_s, lru_h_p, lru_h_s, ffn_conv_p, ffn_conv_s]
    return tuple(out)
```
